```python
import math
import jax, jax.numpy as jnp
from jax import lax
import numpy as np

D_MODEL = 1024
BATCH = 4
SEQ = 4096
DEPTH = 1

ATT_HEADS = 8
ATT_KV_HEADS = 2
ATT_HEAD_DIM = 64
ATT_GROUP = ATT_HEADS // ATT_KV_HEADS
WINDOW = 128
BLOCK = 128
ML_HEADS = 4
ML_HEAD_DIM = 128
ML_CHUNK = 64
CONV_WIDTH = 4
MEM_LEN = 256
X_HEADS = 4
X_HEAD_DIM = 128
N_GROUPS = 4
EXPERTS_PER_GROUP = 8
N_EXPERTS = N_GROUPS * EXPERTS_PER_GROUP
TOP_K = 2
D_EXPERT = 512
MOE_BLOCK = 128
EPS = 1e-6

ATT_Q = ATT_HEADS * ATT_HEAD_DIM
ATT_KV = ATT_KV_HEADS * ATT_HEAD_DIM
ML_W = ML_HEADS * ML_HEAD_DIM
MIX_WIDTH = ATT_Q + ML_W
IN_SPLITS = [ATT_Q, ATT_KV, ATT_KV, 2 * ML_W, ML_W, ML_W, 2 * ML_HEADS]
IN_COLS = sum(IN_SPLITS)
IN_OFFSETS = [int(v) for v in np.cumsum(IN_SPLITS)[:-1]]

kernel_name = 'hybrid_swa_mlstm_hmoe'


def rms_norm(x, g):
    xf = x.astype(jnp.float32)
    y = xf * lax.rsqrt(jnp.mean(xf * xf, axis=-1, keepdims=True) + EPS)
    return (y * g.astype(jnp.float32)).astype(x.dtype)


def alibi_slopes(n):
    return jnp.exp2(-8.0 * jnp.arange(1, n + 1, dtype=jnp.float32) / n)


def causal_depthwise_conv(x, w, b):
    c = x.shape[-1]
    y = lax.conv_general_dilated(x, w[:, None, :], window_strides=(1,),
                                 padding=[(CONV_WIDTH - 1, 0)],
                                 dimension_numbers=('NWC', 'WIO', 'NWC'),
                                 feature_group_count=c)
    return y + b


def sliding_window_gqa(q, k, v, sinks):
    bsz, s = q.shape[0], q.shape[1]
    nb = s // BLOCK
    qb = q.reshape(bsz, nb, BLOCK, ATT_KV_HEADS, ATT_GROUP, ATT_HEAD_DIM)

    def with_prev(t):
        t = t.reshape(bsz, nb, BLOCK, ATT_KV_HEADS, ATT_HEAD_DIM)
        prev = jnp.pad(t[:, :-1], ((0, 0), (1, 0), (0, 0), (0, 0), (0, 0)))
        return jnp.concatenate([prev, t], axis=2)

    kk, vv = with_prev(k), with_prev(v)
    scores = jnp.einsum('bnqkgd,bnskd->bkgnqs', qb, kk).astype(jnp.float32)
    scores = scores * (1.0 / math.sqrt(ATT_HEAD_DIM))
    dist = (jnp.arange(BLOCK)[:, None] + BLOCK) - jnp.arange(2 * BLOCK)[None, :]
    in_band = (dist >= 0) & (dist < WINDOW)
    pad_key = (jnp.arange(nb)[:, None, None] == 0) & (jnp.arange(2 * BLOCK)[None, None, :] < BLOCK)
    valid = in_band[None] & ~pad_key
    slopes = alibi_slopes(ATT_HEADS).reshape(ATT_KV_HEADS, ATT_GROUP)[:, :, None, None, None]
    logits = scores - slopes * dist.astype(jnp.float32)
    logits = jnp.where(valid, logits, -jnp.inf)
    sink = sinks.astype(jnp.float32).reshape(ATT_KV_HEADS, ATT_GROUP)[:, :, None, None, None]
    mx = jnp.maximum(logits.max(axis=-1, keepdims=True), sink)
    e = jnp.exp(logits - mx)
    p = e / (e.sum(axis=-1, keepdims=True) + jnp.exp(sink - mx))
    out = jnp.einsum('bkgnqs,bnskd->bnqkgd', p.astype(v.dtype), vv)
    return out.reshape(bsz, s, ATT_Q)


def mlstm_chunk_step(carry, inp):
    c_st, n_st, m_st = carry
    q, k, v, ig, lf = inp
    L = q.shape[2]
    b = jnp.cumsum(lf, axis=-1)
    causal = jnp.tril(jnp.ones((L, L), dtype=bool))
    log_d = jnp.where(causal, b[..., :, None] - b[..., None, :] + ig[..., None, :], -jnp.inf)
    m_inter = b + m_st[..., None]
    m_t = jnp.maximum(m_inter, log_d.max(axis=-1))
    s = jnp.einsum('bhtd,bhsd->bhts', q, k) * jnp.exp(log_d - m_t[..., None])
    a_inter = jnp.exp(m_inter - m_t)
    num = jnp.einsum('bhts,bhsd->bhtd', s, v) + a_inter[..., None] * jnp.einsum('bhtk,bhkv->bhtv', q, c_st)
    den = s.sum(axis=-1) + a_inter * jnp.einsum('bhtk,bhk->bht', q, n_st)
    h = num / jnp.maximum(jnp.abs(den), jnp.exp(-m_t))[..., None]
    b_last = b[..., -1]
    log_w = b_last[..., None] - b + ig
    m_new = jnp.maximum(b_last + m_st, log_w.max(axis=-1))
    w = jnp.exp(log_w - m_new[..., None])
    decay = jnp.exp(b_last + m_st - m_new)
    c_new = decay[..., None, None] * c_st + jnp.einsum('bhsk,bhsv->bhkv', k * w[..., None], v)
    n_new = decay[..., None] * n_st + jnp.einsum('bhs,bhsk->bhk', w, k)
    return (c_new, n_new, m_new), h


def mlstm(q, k, v, i_pre, f_pre):
    bsz, s, nh, dh = q.shape
    nc = s // ML_CHUNK
    f32 = jnp.float32
    def chunks(t):
        return t.astype(f32).reshape(bsz, nc, ML_CHUNK, nh, dh).transpose(1, 0, 3, 2, 4)
    def gchunks(t):
        return t.reshape(bsz, nc, ML_CHUNK, nh).transpose(1, 0, 3, 2)
    qc = chunks(q)
    kc = chunks(k) * (1.0 / math.sqrt(dh))
    vc = chunks(v)
    igc = gchunks(i_pre.astype(f32))
    lfc = gchunks(jax.nn.log_sigmoid(f_pre.astype(f32)))
    init = (jnp.zeros((bsz, nh, dh, dh), f32), jnp.zeros((bsz, nh, dh), f32), jnp.zeros((bsz, nh), f32))
    _, hs = lax.scan(mlstm_chunk_step, init, (qc, kc, vc, igc, lfc))
    return hs.transpose(1, 0, 3, 2, 4).reshape(bsz, s, nh, dh)


def memory_cross_attention(xn, memn, w_cq, w_ckv, w_co):
    bsz, s, _ = xn.shape
    q = (xn @ w_cq).reshape(bsz, s, X_HEADS, X_HEAD_DIM)
    kv = (memn @ w_ckv).reshape(bsz, memn.shape[1], 2, X_HEADS, X_HEAD_DIM)
    k, v = kv[:, :, 0], kv[:, :, 1]
    sc = jnp.einsum('bshd,bmhd->bhsm', q, k).astype(jnp.float32) * (1.0 / math.sqrt(X_HEAD_DIM))
    p = jax.nn.softmax(sc, axis=-1)
    o = jnp.einsum('bhsm,bmhd->bshd', p.astype(v.dtype), v).reshape(bsz, s, X_HEADS * X_HEAD_DIM)
    return o @ w_co


def hierarchical_moe(xn, w_rg, b_rg, w_re, b_re, w_g, w_u, w_d):
    bsz, s, d = xn.shape
    t = bsz * s
    xt = xn.reshape(t, d)
    g_logits = (xt @ w_rg).astype(jnp.float32) + b_rg.astype(jnp.float32)
    g_prob = jax.nn.softmax(g_logits, axis=-1)
    g_sel = jnp.argmax(g_logits, axis=-1)
    g_w = jnp.take_along_axis(g_prob, g_sel[:, None], axis=-1)
    e_logits = ((xt @ w_re).astype(jnp.float32) + b_re.astype(jnp.float32)).reshape(t, N_GROUPS, EXPERTS_PER_GROUP)
    e_in = jnp.take_along_axis(e_logits, g_sel[:, None, None], axis=1)[:, 0]
    top_v, top_i = lax.top_k(e_in, TOP_K)
    top_p = jax.nn.softmax(top_v, axis=-1) * g_w
    expert_idx = (g_sel[:, None] * EXPERTS_PER_GROUP + top_i).reshape(-1)
    gates = top_p.reshape(-1)
    n_assign = t * TOP_K
    token_idx = jnp.arange(n_assign) // TOP_K
    order = jnp.argsort(expert_idx)
    sorted_e = expert_idx[order]
    counts = jnp.bincount(expert_idx, length=N_EXPERTS)
    padded = (counts + MOE_BLOCK - 1) // MOE_BLOCK * MOE_BLOCK
    pad_end = jnp.cumsum(padded)
    pad_start = pad_end - padded
    start = jnp.cumsum(counts) - counts
    dest = pad_start[sorted_e] + (jnp.arange(n_assign) - start[sorted_e])
    n_blocks = -(-n_assign // MOE_BLOCK) + N_EXPERTS
    x_pad = jnp.zeros((n_blocks * MOE_BLOCK, d), xt.dtype).at[dest].set(xt[token_idx[order]])
    block_e = jnp.minimum(jnp.searchsorted(pad_end, jnp.arange(n_blocks) * MOE_BLOCK, side='right'), N_EXPERTS - 1)

    def expert_block(args):
        xb, e = args
        hb = jax.nn.silu(xb @ w_g[e]) * (xb @ w_u[e])
        return hb @ w_d[e]

    y_pad = lax.map(expert_block, (x_pad.reshape(n_blocks, MOE_BLOCK, d), block_e)).reshape(-1, d)
    y = y_pad[dest] * gates[order][:, None].astype(y_pad.dtype)
    out = jnp.zeros((t, d), xt.dtype).at[token_idx[order]].add(y)
    return out.reshape(bsz, s, d)


def setup_inputs(seed: int = 0) -> dict:
    key = jax.random.key(seed)
    ks = iter(jax.random.split(key, 40))
    f32 = jnp.float32

    def nrm(shape, scale):
        return jax.random.normal(next(ks), shape, f32) * scale

    def gain(shape):
        return 1.0 + nrm(shape, 0.02)

    L = DEPTH
    x = nrm((BATCH, SEQ, D_MODEL), 1.0)
    mem = nrm((BATCH, MEM_LEN, D_MODEL), 1.0)
    b_i = nrm((L, ML_HEADS), 0.1)
    b_f = jnp.linspace(3.0, 6.0, ML_HEADS, dtype=f32)[None, :] + nrm((L, ML_HEADS), 0.01)
    return {
        'x': x,
        'mem': mem,
        'norm_mix': gain((L, D_MODEL)),
        'w_in': nrm((L, D_MODEL, IN_COLS), D_MODEL ** -0.5),
        'b_gates': jnp.concatenate([b_i, b_f], axis=-1),
        'conv_w': nrm((L, CONV_WIDTH, 2 * ML_W), CONV_WIDTH ** -0.5),
        'conv_b': nrm((L, 2 * ML_W), 0.01),
        'att_sinks': nrm((L, ATT_HEADS), 0.5),
        'norm_att_out': gain((L, ATT_Q)),
        'norm_ml_out': gain((L, ML_W)),
        'w_out': nrm((L, MIX_WIDTH, D_MODEL), MIX_WIDTH ** -0.5),
        'norm_cross': gain((L, D_MODEL)),
        'norm_mem': gain((L, D_MODEL)),
        'w_cq': nrm((L, D_MODEL, X_HEADS * X_HEAD_DIM), D_MODEL ** -0.5),
        'w_ckv': nrm((L, D_MODEL, 2 * X_HEADS * X_HEAD_DIM), D_MODEL ** -0.5),
        'w_co': nrm((L, X_HEADS * X_HEAD_DIM, D_MODEL), (X_HEADS * X_HEAD_DIM) ** -0.5),
        'norm_ffn': gain((L, D_MODEL)),
        'w_router_group': nrm((L, D_MODEL, N_GROUPS), D_MODEL ** -0.5),
        'b_router_group': nrm((L, N_GROUPS), 0.01),
        'w_router_expert': nrm((L, D_MODEL, N_EXPERTS), D_MODEL ** -0.5),
        'b_router_expert': nrm((L, N_EXPERTS), 0.01),
        'w_e_gate': nrm((L, N_EXPERTS, D_MODEL, D_EXPERT), D_MODEL ** -0.5),
        'w_e_up': nrm((L, N_EXPERTS, D_MODEL, D_EXPERT), D_MODEL ** -0.5),
        'w_e_down': nrm((L, N_EXPERTS, D_EXPERT, D_MODEL), D_EXPERT ** -0.5),
        'norm_final': gain((D_MODEL,)),
    }


def reference(x, mem, norm_mix, w_in, b_gates, conv_w, conv_b, att_sinks, norm_att_out,
              norm_ml_out, w_out, norm_cross, norm_mem, w_cq, w_ckv, w_co, norm_ffn,
              w_router_group, b_router_group, w_router_expert, b_router_expert,
              w_e_gate, w_e_up, w_e_down, norm_final):
    bsz, s, _ = x.shape
    for l in range(DEPTH):
        h = rms_norm(x, norm_mix[l])
        proj = h @ w_in[l]
        q_a, k_a, v_a, qk_m, v_m, o_m, g_m = jnp.split(proj, IN_OFFSETS, axis=-1)
        att = sliding_window_gqa(q_a.reshape(bsz, s, ATT_HEADS, ATT_HEAD_DIM),
                                 k_a.reshape(bsz, s, ATT_KV_HEADS, ATT_HEAD_DIM),
                                 v_a.reshape(bsz, s, ATT_KV_HEADS, ATT_HEAD_DIM),
                                 att_sinks[l])
        att = rms_norm(att, norm_att_out[l])
        qk_m = jax.nn.silu(causal_depthwise_conv(qk_m, conv_w[l], conv_b[l]))
        q_m, k_m = jnp.split(qk_m, 2, axis=-1)
        g_m = g_m + b_gates[l]
        i_pre, f_pre = g_m[..., :ML_HEADS], g_m[..., ML_HEADS:]
        hm = mlstm(q_m.reshape(bsz, s, ML_HEADS, ML_HEAD_DIM),
                   k_m.reshape(bsz, s, ML_HEADS, ML_HEAD_DIM),
                   v_m.reshape(bsz, s, ML_HEADS, ML_HEAD_DIM), i_pre, f_pre)
        hm = jax.nn.sigmoid(o_m.astype(jnp.float32)).reshape(bsz, s, ML_HEADS, ML_HEAD_DIM) * hm
        hm = rms_norm(hm.astype(x.dtype), norm_ml_out[l].reshape(ML_HEADS, ML_HEAD_DIM)).reshape(bsz, s, ML_W)
        x = x + jnp.concatenate([att, hm], axis=-1) @ w_out[l]
        x = x + memory_cross_attention(rms_norm(x, norm_cross[l]), rms_norm(mem, norm_mem[l]),
                                       w_cq[l], w_ckv[l], w_co[l])
        x = x + hierarchical_moe(rms_norm(x, norm_ffn[l]), w_router_group[l], b_router_group[l],
                                 w_router_expert[l], b_router_expert[l],
                                 w_e_gate[l], w_e_up[l], w_e_down[l])
    return rms_norm(x, norm_final)
```

```python
import functools
import math

import jax
import jax.numpy as jnp
from jax import lax
from jax.experimental import pallas as pl
from jax.experimental.pallas import tpu as pltpu

F32 = jnp.float32
BF16 = jnp.bfloat16
EPS = 1e-6
NEG_INF = float("-inf")

ATT_HEADS = 8
ATT_HEAD_DIM = 64
ATT_Q = ATT_HEADS * ATT_HEAD_DIM
ATT_KV = 128
WINDOW = 128
ML_HEADS = 4
ML_HEAD_DIM = 128
ML_W = ML_HEADS * ML_HEAD_DIM
CONV_WIDTH = 4
X_HEADS = 4
X_HEAD_DIM = 128
N_GROUPS = 4
EXPERTS_PER_GROUP = 8
N_EXPERTS = N_GROUPS * EXPERTS_PER_GROUP
TOP_K = 2

LANES = 128
TM_IN = 512
ML_CHUNK = 128
TM_MID = 256
TD = 512
TC = 256
BM = 256
VMEM_LIMIT = 56 * 1024 * 1024

_NT = (((1,), (1,)), ((), ()))
_TN = (((0,), (0,)), ((), ()))


def _rms(x, g):
    return x * lax.rsqrt(jnp.mean(x * x, axis=-1, keepdims=True) + EPS) * g


def _cparams(n_axes):
    return pltpu.CompilerParams(dimension_semantics=("arbitrary",) * n_axes,
                                vmem_limit_bytes=VMEM_LIMIT)


_SEG = {"qa": (0, 512), "ka": (512, 640), "va": (640, 768), "qk": (768, 1792),
        "vm": (1792, 2304), "om": (2304, 2816), "gi": (2816, 2944), "gf": (2944, 3072)}


def _in_proj_kernel(x_ref, g_ref, w_ref, qa_ref, ka_ref, va_ref, qk_ref, vm_ref, om_ref,
                    gi_ref, gf_ref):
    hb = _rms(x_ref[...], g_ref[...]).astype(BF16)

    def seg(name):
        lo, hi = _SEG[name]
        return jnp.dot(hb, w_ref[:, lo:hi], preferred_element_type=F32)

    qa_ref[...] = seg("qa").astype(BF16)
    ka_ref[...] = seg("ka").astype(BF16)
    va_ref[...] = seg("va").astype(BF16)
    qk_ref[...] = seg("qk")
    vm_ref[...] = seg("vm").astype(BF16)
    om_ref[...] = seg("om")
    gi_ref[...] = seg("gi")
    gf_ref[...] = seg("gf")


def _in_proj(x2d, g, w_packed):
    t, d = x2d.shape
    tm = TM_IN
    widths = [(512, BF16), (128, BF16), (128, BF16), (1024, F32), (512, BF16), (512, F32),
              (128, F32), (128, F32)]
    return pl.pallas_call(
        _in_proj_kernel,
        grid=(t // tm,),
        in_specs=[pl.BlockSpec((tm, d), lambda i: (i, 0)),
                  pl.BlockSpec((1, d), lambda i: (0, 0)),
                  pl.BlockSpec(w_packed.shape, lambda i: (0, 0))],
        out_specs=[pl.BlockSpec((tm, w), lambda i: (i, 0)) for w, _ in widths],
        out_shape=[jax.ShapeDtypeStruct((t, w), dt) for w, dt in widths],
        compiler_params=_cparams(1),
        name="in_proj",
    )(x2d, g, w_packed)


def _swa_kernel(sink_ref, q_ref, kp_ref, kc_ref, vp_ref, vc_ref, gn_ref, o_ref):
    n = pl.program_id(1)
    q = q_ref[...]
    k2 = jnp.concatenate([kp_ref[...], kc_ref[...]], axis=0).astype(F32)
    v2 = jnp.concatenate([vp_ref[...], vc_ref[...]], axis=0).astype(F32)
    lane = lax.broadcasted_iota(jnp.int32, k2.shape, 1)
    low = lane < ATT_HEAD_DIM

    def lane_variants(a):
        ar = pltpu.roll(a, ATT_HEAD_DIM, axis=1)
        z = jnp.zeros_like(a)
        return ((jnp.where(low, a, z).astype(BF16), jnp.where(low, z, ar).astype(BF16)),
                (jnp.where(low, ar, z).astype(BF16), jnp.where(low, z, a).astype(BF16)))

    k_ops = lane_variants(k2)
    v_ops = lane_variants(v2)

    ti = lax.broadcasted_iota(jnp.int32, (WINDOW, 2 * WINDOW), 0)
    si = lax.broadcasted_iota(jnp.int32, (WINDOW, 2 * WINDOW), 1)
    dist = ti + WINDOW - si
    valid = (dist >= 0) & (dist < WINDOW) & ((n > 0) | (si >= WINDOW))
    distf = dist.astype(F32)
    scale = 1.0 / math.sqrt(ATT_HEAD_DIM)

    tiles = []
    for j in range(ATT_Q // LANES):
        qj = q[:, LANES * j:LANES * (j + 1)]
        kv = (2 * j) // (ATT_HEADS // 2)
        acc = jnp.zeros((WINDOW, LANES), F32)
        for par in range(2):
            hidx = 2 * j + par
            slope = 2.0 ** (-8.0 * (hidx + 1) / ATT_HEADS)
            s = lax.dot_general(qj, k_ops[kv][par], _NT, preferred_element_type=F32) * scale
            logits = jnp.where(valid, s - slope * distf, NEG_INF)
            sink = sink_ref[hidx]
            mx = jnp.maximum(jnp.max(logits, axis=-1, keepdims=True), sink)
            e = jnp.exp(logits - mx)
            den = jnp.sum(e, axis=-1, keepdims=True) + jnp.exp(sink - mx)
            p = (e / den).astype(BF16)
            acc = acc + jnp.dot(p, v_ops[kv][par], preferred_element_type=F32)
        tiles.append(acc)
    att = jnp.concatenate(tiles, axis=1)
    o_ref[...] = _rms(att, gn_ref[...]).astype(BF16)


def _swa(qa, ka, va, sinks, gn, bsz, seq):
    nb = seq // WINDOW
    t = bsz * seq
    cur = lambda b, n: (b * nb + n, 0)
    prev = lambda b, n: (b * nb + jnp.maximum(n - 1, 0), 0)
    return pl.pallas_call(
        _swa_kernel,
        grid=(bsz, nb),
        in_specs=[pl.BlockSpec(memory_space=pltpu.SMEM),
                  pl.BlockSpec((WINDOW, ATT_Q), cur),
                  pl.BlockSpec((WINDOW, ATT_KV), prev),
                  pl.BlockSpec((WINDOW, ATT_KV), cur),
                  pl.BlockSpec((WINDOW, ATT_KV), prev),
                  pl.BlockSpec((WINDOW, ATT_KV), cur),
                  pl.BlockSpec((1, ATT_Q), lambda b, n: (0, 0))],
        out_specs=pl.BlockSpec((WINDOW, ATT_Q), cur),
        out_shape=jax.ShapeDtypeStruct((t, ATT_Q), BF16),
        compiler_params=_cparams(2),
        name="swa",
    )(sinks, qa, ka, ka, va, va, gn)


def _split3(a):
    hi = a.astype(BF16)
    r1 = a - hi.astype(F32)
    mid = r1.astype(BF16)
    lo = (r1 - mid.astype(F32)).astype(BF16)
    return hi, mid, lo


def _mlstm_kernel(qk_ref, v_ref, o_ref, gi_ref, gf_ref, cw_ref, cb_ref, bi_ref, bf_ref, gn_ref,
                  out_ref, xbuf, cn_s, m_s):
    c = pl.program_id(1)
    L = qk_ref.shape[0]
    HD = ML_HEAD_DIM

    @pl.when(c == 0)
    def _():
        xbuf[0:8, :] = jnp.zeros((8, xbuf.shape[1]), F32)
        cn_s[...] = jnp.zeros(cn_s.shape, F32)
        m_s[...] = jnp.zeros(m_s.shape, F32)

    xbuf[8:8 + L, :] = qk_ref[...]
    y = cb_ref[...] + cw_ref[3:4, :] * xbuf[8:8 + L, :]
    for j in range(CONV_WIDTH - 1):
        sh = CONV_WIDTH - 1 - j
        y = y + cw_ref[j:j + 1, :] * xbuf[8 - sh:8 - sh + L, :]
    xbuf[0:8, :] = xbuf[L:L + 8, :]
    qk = y * jax.nn.sigmoid(y)
    q = qk[:, :ML_W]
    k = qk[:, ML_W:] * (1.0 / math.sqrt(HD))

    gi = gi_ref[...] + bi_ref[...]
    z = gf_ref[...] + bf_ref[...]
    lf = jnp.minimum(z, 0.0) - jnp.log1p(jnp.exp(-jnp.abs(z)))

    ti = lax.broadcasted_iota(jnp.int32, (L, L), 0)
    si = lax.broadcasted_iota(jnp.int32, (L, L), 1)
    causal = si <= ti
    tril = jnp.where(causal, 1.0, 0.0).astype(BF16)
    b = jnp.zeros((L, LANES), F32)
    for part in _split3(lf):
        b = b + jnp.dot(tril, part, preferred_element_type=F32)

    m_st = m_s[...]
    b_last = b[L - 1:L, :]
    m_inter = b + m_st
    log_w = b_last - b + gi
    m_new = jnp.maximum(b_last + m_st, jnp.max(log_w, axis=0, keepdims=True))
    w = jnp.exp(log_w - m_new)
    decay = jnp.exp(b_last + m_st - m_new)
    rt = jnp.transpose(gi - b)

    lane_l = lax.broadcasted_iota(jnp.int32, (L, HD), 1)
    ones_col = jnp.where(lane_l == 0, 1.0, 0.0).astype(BF16)

    for h in range(ML_HEADS):
        sl = slice(HD * h, HD * (h + 1))
        qh = q[:, sl].astype(BF16)
        kh = k[:, sl]
        vext = jnp.concatenate([v_ref[:, sl], ones_col], axis=1)
        log_d = jnp.where(causal, b[:, h:h + 1] + rt[h:h + 1, :], NEG_INF)
        m_t = jnp.maximum(m_inter[:, h:h + 1], jnp.max(log_d, axis=1, keepdims=True))
        d = jnp.exp(log_d - m_t)
        s = lax.dot_general(qh, kh.astype(BF16), _NT, preferred_element_type=F32) * d
        a_inter = jnp.exp(m_inter[:, h:h + 1] - m_t)
        cn = cn_s[h]
        nd = (jnp.dot(s.astype(BF16), vext, preferred_element_type=F32)
              + a_inter * jnp.dot(qh, cn.astype(BF16), preferred_element_type=F32))
        num = nd[:, :HD]
        den = nd[:, HD:HD + 1]
        hh = num / jnp.maximum(jnp.abs(den), jnp.exp(-m_t))

        kw = (kh * w[:, h:h + 1]).astype(BF16)
        upd = lax.dot_general(kw, vext, _TN, preferred_element_type=F32)
        cn_s[h] = decay[:, h:h + 1] * cn + upd

        hm = jax.nn.sigmoid(o_ref[:, sl]) * hh
        out_ref[:, sl] = _rms(hm, gn_ref[:, sl]).astype(BF16)

    m_s[...] = m_new


def _mlstm(qk, vm, om, gi, gf, cw, cb, bi, bf, gn, bsz, seq):
    L = ML_CHUNK
    nc = seq // L
    t = bsz * seq
    row = lambda b, c: (b * nc + c, 0)
    const = lambda b, c: (0, 0)
    return pl.pallas_call(
        _mlstm_kernel,
        grid=(bsz, nc),
        in_specs=[pl.BlockSpec((L, 2 * ML_W), row),
                  pl.BlockSpec((L, ML_W), row),
                  pl.BlockSpec((L, ML_W), row),
                  pl.BlockSpec((L, LANES), row),
                  pl.BlockSpec((L, LANES), row),
                  pl.BlockSpec((CONV_WIDTH, 2 * ML_W), const),
                  pl.BlockSpec((1, 2 * ML_W), const),
                  pl.BlockSpec((1, LANES), const),
                  pl.BlockSpec((1, LANES), const),
                  pl.BlockSpec((1, ML_W), const)],
        out_specs=pl.BlockSpec((L, ML_W), row),
        out_shape=jax.ShapeDtypeStruct((t, ML_W), BF16),
        scratch_shapes=[pltpu.VMEM((L + 8, 2 * ML_W), F32),
                        pltpu.VMEM((ML_HEADS, ML_HEAD_DIM, 2 * ML_HEAD_DIM), F32),
                        pltpu.VMEM((1, LANES), F32)],
        compiler_params=_cparams(2),
        name="mlstm",
    )(qk, vm, om, gi, gf, cw, cb, bi, bf, gn)


def _kv_kernel(mem_ref, g_ref, w_ref, k_ref, v_ref):
    mn = _rms(mem_ref[...], g_ref[...]).astype(BF16)
    hw = k_ref.shape[1]
    k_ref[...] = jnp.dot(mn, w_ref[:, :hw], preferred_element_type=F32).astype(BF16)
    v_ref[...] = jnp.dot(mn, w_ref[:, hw:], preferred_element_type=F32).astype(BF16)


def _kv_prep(mem2d, g, w_ckv):
    r, d = mem2d.shape
    hw = w_ckv.shape[1] // 2
    tm = 256
    return pl.pallas_call(
        _kv_kernel,
        grid=(r // tm,),
        in_specs=[pl.BlockSpec((tm, d), lambda i: (i, 0)),
                  pl.BlockSpec((1, d), lambda i: (0, 0)),
                  pl.BlockSpec(w_ckv.shape, lambda i: (0, 0))],
        out_specs=[pl.BlockSpec((tm, hw), lambda i: (i, 0))] * 2,
        out_shape=[jax.ShapeDtypeStruct((r, hw), BF16)] * 2,
        compiler_params=_cparams(1),
        name="kv_prep",
    )(mem2d, g, w_ckv)


_R_E0, _R_E1, _R_RANK0, _R_RANK1, _R_P0, _R_P1 = range(6)
_ROUTER_LANE0 = N_GROUPS


def _mid_kernel(x_ref, att_ref, hm_ref, wo_ref, gc_ref, wq_ref, k_ref, v_ref, wco_ref, gf_ref,
                wr_ref, br_ref, x2_ref, xn_ref, route_ref, cnt_ref, carry_s):
    i = pl.program_id(0)
    tm = x_ref.shape[0]

    @pl.when(i == 0)
    def _():
        carry_s[...] = jnp.zeros(carry_s.shape, F32)

    x1 = (x_ref[...]
          + jnp.dot(att_ref[...], wo_ref[0:ATT_Q, :], preferred_element_type=F32)
          + jnp.dot(hm_ref[...], wo_ref[ATT_Q:, :], preferred_element_type=F32))

    xc = _rms(x1, gc_ref[...]).astype(BF16)
    qb = jnp.dot(xc, wq_ref[...], preferred_element_type=F32).astype(BF16)
    heads = []
    for h in range(X_HEADS):
        sl = slice(X_HEAD_DIM * h, X_HEAD_DIM * (h + 1))
        s = lax.dot_general(qb[:, sl], k_ref[:, sl], _NT,
                            preferred_element_type=F32) * (1.0 / math.sqrt(X_HEAD_DIM))
        e = jnp.exp(s - jnp.max(s, axis=-1, keepdims=True))
        p = (e / jnp.sum(e, axis=-1, keepdims=True)).astype(BF16)
        heads.append(jnp.dot(p, v_ref[:, sl], preferred_element_type=F32))
    o = jnp.concatenate(heads, axis=1).astype(BF16)
    x2 = x1 + jnp.dot(o, wco_ref[...], preferred_element_type=F32)
    x2_ref[...] = x2

    xn = _rms(x2, gf_ref[...])
    xn_ref[...] = xn
    xh = xn.astype(BF16)
    xl = (xn - xh.astype(F32)).astype(BF16)
    lg2 = jnp.dot(xh, wr_ref[...], preferred_element_type=F32)
    logits = (lg2[:, :LANES] + lg2[:, LANES:]
              + jnp.dot(xl, wr_ref[:, :LANES], preferred_element_type=F32) + br_ref[...])

    lane = lax.broadcasted_iota(jnp.int32, (tm, LANES), 1)
    lanef = lane.astype(F32)
    big = float(4 * LANES)
    gl = jnp.where(lane < N_GROUPS, logits, NEG_INF)
    gmax = jnp.max(gl, axis=-1, keepdims=True)
    gsel = jnp.min(jnp.where(gl == gmax, lanef, big), axis=-1, keepdims=True)
    gw = 1.0 / jnp.sum(jnp.exp(gl - gmax), axis=-1, keepdims=True)
    lo_lane = _ROUTER_LANE0 + EXPERTS_PER_GROUP * gsel
    in_group = (lanef >= lo_lane) & (lanef < lo_lane + EXPERTS_PER_GROUP)
    el = jnp.where(in_group, logits, NEG_INF)
    v0 = jnp.max(el, axis=-1, keepdims=True)
    i0 = jnp.min(jnp.where(el == v0, lanef, big), axis=-1, keepdims=True)
    el2 = jnp.where(lanef == i0, NEG_INF, el)
    v1 = jnp.max(el2, axis=-1, keepdims=True)
    i1 = jnp.min(jnp.where(el2 == v1, lanef, big), axis=-1, keepdims=True)
    tt = jnp.exp(v1 - v0)
    p0 = gw / (1.0 + tt)
    p1 = gw * tt / (1.0 + tt)

    sel0 = lanef == i0
    sel1 = lanef == i1
    mb = jnp.where(sel0 | sel1, 1.0, 0.0).astype(BF16)
    ti = lax.broadcasted_iota(jnp.int32, (tm, tm), 0)
    si = lax.broadcasted_iota(jnp.int32, (tm, tm), 1)
    strict = jnp.where(si < ti, 1.0, 0.0).astype(BF16)
    carry = carry_s[...]
    pref = jnp.dot(strict, mb, preferred_element_type=F32) + carry
    r0 = jnp.sum(jnp.where(sel0, pref, 0.0), axis=-1, keepdims=True)
    r1 = jnp.sum(jnp.where(sel1, pref, 0.0), axis=-1, keepdims=True)
    carry = carry + jnp.sum(mb.astype(F32), axis=0, keepdims=True)
    carry_s[...] = carry
    cnt_ref[...] = carry

    route = jnp.zeros((tm, LANES), F32)
    for idx, col in ((_R_E0, i0 - _ROUTER_LANE0), (_R_E1, i1 - _ROUTER_LANE0), (_R_RANK0, r0),
                     (_R_RANK1, r1), (_R_P0, p0), (_R_P1, p1)):
        route = jnp.where(lane == idx, col, route)
    route_ref[...] = route


def _mid(x2d, att, hm, wo, gc, wq, kmem, vmem_, wco, gf, wr, br, seq):
    t, d = x2d.shape
    tm = TM_MID
    per_b = seq // tm
    rowmap = lambda i: (i, 0)
    const = lambda i: (0, 0)
    bmap = lambda i: (i // per_b, 0)
    mem_len = kmem.shape[0] // (t // seq)
    return pl.pallas_call(
        _mid_kernel,
        grid=(t // tm,),
        in_specs=[pl.BlockSpec((tm, d), rowmap),
                  pl.BlockSpec((tm, ATT_Q), rowmap),
                  pl.BlockSpec((tm, ML_W), rowmap),
                  pl.BlockSpec(wo.shape, const),
                  pl.BlockSpec((1, d), const),
                  pl.BlockSpec(wq.shape, const),
                  pl.BlockSpec((mem_len, kmem.shape[1]), bmap),
                  pl.BlockSpec((mem_len, vmem_.shape[1]), bmap),
                  pl.BlockSpec(wco.shape, const),
                  pl.BlockSpec((1, d), const),
                  pl.BlockSpec(wr.shape, const),
                  pl.BlockSpec((1, LANES), const)],
        out_specs=[pl.BlockSpec((tm, d), rowmap),
                   pl.BlockSpec((tm, d), rowmap),
                   pl.BlockSpec((tm, LANES), rowmap),
                   pl.BlockSpec((1, LANES), const)],
        out_shape=[jax.ShapeDtypeStruct((t, d), F32),
                   jax.ShapeDtypeStruct((t, d), F32),
                   jax.ShapeDtypeStruct((t, LANES), F32),
                   jax.ShapeDtypeStruct((1, LANES), F32)],
        scratch_shapes=[pltpu.VMEM((1, LANES), F32)],
        compiler_params=_cparams(1),
        name="mid",
    )(x2d, att, hm, wo, gc, wq, kmem, vmem_, wco, gf, wr, br)


def _dispatch_kernel(pe_ref, d0_ref, d1_ref, xn_ref, xpad_ref, zbuf, sem, zsem):
    i = pl.program_id(0)
    td = xn_ref.shape[0]

    def zero_copy(e):
        pend = pe_ref[e]
        pstart = pe_ref[e - 1] if e > 0 else 0
        cp = pltpu.make_async_copy(
            zbuf, xpad_ref.at[pl.ds(pl.multiple_of(pend - BM, BM), BM), :], zsem)
        return pend > pstart, cp

    @pl.when(i == 0)
    def _():
        zbuf[...] = jnp.zeros(zbuf.shape, F32)
        for e in range(N_EXPERTS):
            nonempty, cp = zero_copy(e)
            pl.when(nonempty)(cp.start)
        for e in range(N_EXPERTS):
            nonempty, cp = zero_copy(e)
            pl.when(nonempty)(cp.wait)

        def tail_copy(b):
            return pltpu.make_async_copy(
                zbuf, xpad_ref.at[pl.ds(pl.multiple_of(b * BM, BM), BM), :], zsem)

        first_unused = lax.div(pe_ref[N_EXPERTS - 1], BM)
        n_blocks = xpad_ref.shape[0] // BM
        lax.fori_loop(first_unused, n_blocks, lambda b, c: (tail_copy(b).start(), c)[1], 0)
        lax.fori_loop(first_unused, n_blocks, lambda b, c: (tail_copy(b).wait(), c)[1], 0)

    def row_copy(t, dst):
        return pltpu.make_async_copy(xn_ref.at[pl.ds(t, 1), :], xpad_ref.at[pl.ds(dst, 1), :], sem)

    def start(t, carry):
        row_copy(t, d0_ref[0, 0, t]).start()
        row_copy(t, d1_ref[0, 0, t]).start()
        return carry

    def wait(t, carry):
        row_copy(0, 0).wait()
        row_copy(0, 0).wait()
        return carry

    lax.fori_loop(0, td, start, 0, unroll=8)
    lax.fori_loop(0, td, wait, 0, unroll=8)


def _dispatch(pad_end, d0, d1, xn, rows):
    t, d = xn.shape
    td = TD
    nt = t // td
    smem_blk = pl.BlockSpec((1, 1, td), lambda i, pe: (i, 0, 0), memory_space=pltpu.SMEM)
    return pl.pallas_call(
        _dispatch_kernel,
        grid_spec=pltpu.PrefetchScalarGridSpec(
            num_scalar_prefetch=1,
            grid=(nt,),
            in_specs=[smem_blk, smem_blk,
                      pl.BlockSpec((td, d), lambda i, pe: (i, 0))],
            out_specs=pl.BlockSpec(memory_space=pl.ANY),
            scratch_shapes=[pltpu.VMEM((BM, d), F32),
                            pltpu.SemaphoreType.DMA(()),
                            pltpu.SemaphoreType.DMA(())]),
        out_shape=jax.ShapeDtypeStruct((rows, d), F32),
        compiler_params=_cparams(1),
        name="dispatch",
    )(pad_end, d0.reshape(nt, 1, td), d1.reshape(nt, 1, td), xn)


def _expert_kernel(be_ref, nu_ref, x_ref, wg_ref, wu_ref, wd_ref, y_ref, wg_s, wu_s, wd_s):
    i = pl.program_id(0)

    @pl.when(i < nu_ref[0])
    def _():
        prev = be_ref[jnp.maximum(i - 1, 0)]
        changed = (i == 0) | (be_ref[i] != prev)

        @pl.when(changed)
        def _():
            wg_s[...] = wg_ref[...].astype(BF16)
            wu_s[...] = wu_ref[...].astype(BF16)
            wd_s[...] = wd_ref[...].astype(BF16)

        xb = x_ref[...].astype(BF16)
        g = jnp.dot(xb, wg_s[...], preferred_element_type=F32)
        u = jnp.dot(xb, wu_s[...], preferred_element_type=F32)
        hb = (g * jax.nn.sigmoid(g) * u).astype(BF16)
        y_ref[...] = jnp.dot(hb, wd_s[...], preferred_element_type=F32)

    @pl.when(i >= nu_ref[0])
    def _():
        y_ref[...] = jnp.zeros(y_ref.shape, F32)


def _experts(block_e, nused, xpad, wg, wu, wd):
    rows, d = xpad.shape
    nbk = rows // BM
    de = wg.shape[2]
    blk = lambda i, be, nu: (jnp.minimum(i, nu[0] - 1), 0)
    oblk = lambda i, be, nu: (i, 0)
    wmap = lambda i, be, nu: (be[jnp.minimum(i, nu[0] - 1)], 0, 0)
    return pl.pallas_call(
        _expert_kernel,
        grid_spec=pltpu.PrefetchScalarGridSpec(
            num_scalar_prefetch=2,
            grid=(nbk,),
            in_specs=[pl.BlockSpec((BM, d), blk),
                      pl.BlockSpec((None, d, de), wmap),
                      pl.BlockSpec((None, d, de), wmap),
                      pl.BlockSpec((None, de, d), wmap)],
            out_specs=pl.BlockSpec((BM, d), oblk),
            scratch_shapes=[pltpu.VMEM((d, de), BF16),
                            pltpu.VMEM((d, de), BF16),
                            pltpu.VMEM((de, d), BF16)]),
        out_shape=jax.ShapeDtypeStruct((rows, d), F32),
        compiler_params=_cparams(1),
        name="experts",
    )(block_e, nused, xpad, wg, wu, wd)


def _combine_kernel(d0c_ref, d1c_ref, d0n_ref, d1n_ref, x2_ref, route_ref, g_ref, ypad_ref,
                    out_ref, ybuf, sem):
    i = pl.program_id(0)
    n = pl.num_programs(0)
    tc = x2_ref.shape[0]

    def row_copy(src, slot, which, t):
        return pltpu.make_async_copy(ypad_ref.at[pl.ds(src, 1), :],
                                     ybuf.at[slot, which, pl.ds(t, 1), :], sem.at[slot])

    def issue(d0_ref, d1_ref, slot):
        def body(t, carry):
            row_copy(d0_ref[0, 0, t], slot, 0, t).start()
            row_copy(d1_ref[0, 0, t], slot, 1, t).start()
            return carry
        lax.fori_loop(0, tc, body, 0, unroll=8)

    slot = i % 2

    @pl.when(i == 0)
    def _():
        issue(d0c_ref, d1c_ref, 0)

    @pl.when(i + 1 < n)
    def _():
        issue(d0n_ref, d1n_ref, 1 - slot)

    def wait(t, carry):
        row_copy(0, slot, 0, 0).wait()
        row_copy(0, slot, 1, 0).wait()
        return carry
    lax.fori_loop(0, tc, wait, 0, unroll=8)

    route = route_ref[...]
    p0 = route[:, _R_P0:_R_P0 + 1]
    p1 = route[:, _R_P1:_R_P1 + 1]
    x3 = x2_ref[...] + p0 * ybuf[slot, 0] + p1 * ybuf[slot, 1]
    out_ref[...] = _rms(x3, g_ref[...])


def _combine(d0, d1, x2, route, g, ypad):
    t, d = x2.shape
    tc = TC
    nt = t // tc
    d0r = d0.reshape(nt, 1, tc)
    d1r = d1.reshape(nt, 1, tc)
    cur = pl.BlockSpec((1, 1, tc), lambda i: (i, 0, 0), memory_space=pltpu.SMEM)
    nxt = pl.BlockSpec((1, 1, tc), lambda i: (jnp.minimum(i + 1, nt - 1), 0, 0),
                       memory_space=pltpu.SMEM)
    return pl.pallas_call(
        _combine_kernel,
        grid=(nt,),
        in_specs=[cur, cur, nxt, nxt,
                  pl.BlockSpec((tc, d), lambda i: (i, 0)),
                  pl.BlockSpec((tc, LANES), lambda i: (i, 0)),
                  pl.BlockSpec((1, d), lambda i: (0, 0)),
                  pl.BlockSpec(memory_space=pl.ANY)],
        out_specs=pl.BlockSpec((tc, d), lambda i: (i, 0)),
        out_shape=jax.ShapeDtypeStruct((t, d), F32),
        scratch_shapes=[pltpu.VMEM((2, 2, tc, d), F32),
                        pltpu.SemaphoreType.DMA((2,))],
        compiler_params=_cparams(1),
        name="combine",
    )(d0r, d1r, d0r, d1r, x2, route, g, ypad)


def _pad_cols(a, width):
    return jnp.pad(a, ((0, 0), (0, width - a.shape[1])))


def kernel(x, mem, norm_mix, w_in, b_gates, conv_w, conv_b, att_sinks, norm_att_out, norm_ml_out,
           w_out, norm_cross, norm_mem, w_cq, w_ckv, w_co, norm_ffn, w_router_group,
           b_router_group, w_router_expert, b_router_expert, w_e_gate, w_e_up, w_e_down,
           norm_final):
    bsz, seq, d = x.shape
    t = bsz * seq
    depth = w_in.shape[0]
    assert depth == 1, "the final RMSNorm is fused into the last layer's combine kernel"
    xs = x.reshape(t, d)
    mem2d = mem.reshape(-1, d)

    for l in range(depth):
        n_main = _SEG["om"][1]
        wl = w_in[l]
        w_packed = jnp.concatenate(
            [wl[:, :n_main], _pad_cols(wl[:, n_main:n_main + ML_HEADS], LANES),
             _pad_cols(wl[:, n_main + ML_HEADS:], LANES)], axis=1).astype(BF16)
        bi = _pad_cols(b_gates[l][None, :ML_HEADS], LANES)
        bf = _pad_cols(b_gates[l][None, ML_HEADS:], LANES)
        w_r = _pad_cols(jnp.concatenate([w_router_group[l], w_router_expert[l]], axis=1), LANES)
        w_r_hi = w_r.astype(BF16)
        w_r_lo = (w_r - w_r_hi.astype(F32)).astype(BF16)
        w_r_packed = jnp.concatenate([w_r_hi, w_r_lo], axis=1)
        b_r = _pad_cols(jnp.concatenate([b_router_group[l], b_router_expert[l]])[None, :], LANES)

        qa, ka, va, qk, vm, om, gi, gf = _in_proj(xs, norm_mix[l][None, :], w_packed)
        att = _swa(qa, ka, va, att_sinks[l], norm_att_out[l][None, :], bsz, seq)
        hm = _mlstm(qk, vm, om, gi, gf, conv_w[l], conv_b[l][None, :], bi, bf,
                    norm_ml_out[l][None, :], bsz, seq)

        kmem, vmem_ = _kv_prep(mem2d, norm_mem[l][None, :], w_ckv[l].astype(BF16))
        x2, xn, route, cnt = _mid(xs, att, hm, w_out[l].astype(BF16), norm_cross[l][None, :],
                                  w_cq[l].astype(BF16), kmem, vmem_, w_co[l].astype(BF16),
                                  norm_ffn[l][None, :], w_r_packed, b_r, seq)

        counts = cnt[0, _ROUTER_LANE0:_ROUTER_LANE0 + N_EXPERTS].astype(jnp.int32)
        padded = (counts + BM - 1) // BM * BM
        pad_end = jnp.cumsum(padded).astype(jnp.int32)
        pad_start = pad_end - padded
        e0 = route[:, _R_E0].astype(jnp.int32)
        e1 = route[:, _R_E1].astype(jnp.int32)
        dest0 = pad_start[e0] + route[:, _R_RANK0].astype(jnp.int32)
        dest1 = pad_start[e1] + route[:, _R_RANK1].astype(jnp.int32)
        nbk = (t * TOP_K) // BM + N_EXPERTS
        block_start = jnp.arange(nbk, dtype=jnp.int32) * BM
        block_e = jnp.minimum(
            jnp.sum((pad_end[None, :] <= block_start[:, None]).astype(jnp.int32), axis=1),
            N_EXPERTS - 1)
        nused = (pad_end[-1:] // BM).astype(jnp.int32)

        xpad = _dispatch(pad_end, dest0, dest1, xn, nbk * BM)
        ypad = _experts(block_e, nused, xpad, w_e_gate[l], w_e_up[l], w_e_down[l])
        xs = _combine(dest0, dest1, x2, route, norm_final[None, :], ypad)
    return xs.reshape(bsz, seq, d)
```

```python
import functools
import math

import jax
import jax.numpy as jnp
from jax import lax
from jax.experimental import pallas as pl
from jax.experimental.pallas import tpu as pltpu

F32 = jnp.float32
BF16 = jnp.bfloat16
EPS = 1e-6
NEG_INF = float("-inf")

ATT_HEADS = 8
ATT_HEAD_DIM = 64
ATT_Q = ATT_HEADS * ATT_HEAD_DIM
ATT_KV = 128
WINDOW = 128
ML_HEADS = 4
ML_HEAD_DIM = 128
ML_W = ML_HEADS * ML_HEAD_DIM
CONV_WIDTH = 4
X_HEADS = 4
X_HEAD_DIM = 128
N_GROUPS = 4
EXPERTS_PER_GROUP = 8
N_EXPERTS = N_GROUPS * EXPERTS_PER_GROUP
TOP_K = 2

LANES = 128
TM_IN = 512
ML_CHUNK = 128
TM_MID = 256
TD = 512
TC = 256
BM = 256
VMEM_LIMIT = 56 * 1024 * 1024

_NT = (((1,), (1,)), ((), ()))
_TN = (((0,), (0,)), ((), ()))


def _rms(x, g):
    return x * lax.rsqrt(jnp.mean(x * x, axis=-1, keepdims=True) + EPS) * g


def _cparams(n_axes):
    return pltpu.CompilerParams(dimension_semantics=("arbitrary",) * n_axes,
                                vmem_limit_bytes=VMEM_LIMIT)


_SEG = {"qa": (0, 512), "ka": (512, 640), "va": (640, 768), "qk": (768, 1792),
        "vm": (1792, 2304), "om": (2304, 2816), "gi": (2816, 2944), "gf": (2944, 3072)}


def _in_proj_kernel(x_ref, g_ref, w_ref, qa_ref, ka_ref, va_ref, qk_ref, vm_ref, om_ref,
                    gi_ref, gf_ref):
    hb = _rms(x_ref[...], g_ref[...]).astype(BF16)

    def seg(name):
        lo, hi = _SEG[name]
        return jnp.dot(hb, w_ref[:, lo:hi], preferred_element_type=F32)

    qa_ref[...] = seg("qa").astype(BF16)
    ka_ref[...] = seg("ka").astype(BF16)
    va_ref[...] = seg("va").astype(BF16)
    qk_ref[...] = seg("qk")
    vm_ref[...] = seg("vm").astype(BF16)
    om_ref[...] = seg("om")
    gi_ref[...] = seg("gi")
    gf_ref[...] = seg("gf")


def _in_proj(x2d, g, w_packed):
    t, d = x2d.shape
    tm = TM_IN
    widths = [(512, BF16), (128, BF16), (128, BF16), (1024, F32), (512, BF16), (512, F32),
              (128, F32), (128, F32)]
    return pl.pallas_call(
        _in_proj_kernel,
        grid=(t // tm,),
        in_specs=[pl.BlockSpec((tm, d), lambda i: (i, 0)),
                  pl.BlockSpec((1, d), lambda i: (0, 0)),
                  pl.BlockSpec(w_packed.shape, lambda i: (0, 0))],
        out_specs=[pl.BlockSpec((tm, w), lambda i: (i, 0)) for w, _ in widths],
        out_shape=[jax.ShapeDtypeStruct((t, w), dt) for w, dt in widths],
        compiler_params=_cparams(1),
        name="in_proj",
    )(x2d, g, w_packed)


def _swa_kernel(sink_ref, q_ref, kp_ref, kc_ref, vp_ref, vc_ref, gn_ref, o_ref):
    n = pl.program_id(1)
    q = q_ref[...]
    k2 = jnp.concatenate([kp_ref[...], kc_ref[...]], axis=0).astype(F32)
    v2 = jnp.concatenate([vp_ref[...], vc_ref[...]], axis=0).astype(F32)
    lane = lax.broadcasted_iota(jnp.int32, k2.shape, 1)
    low = lane < ATT_HEAD_DIM

    def lane_variants(a):
        ar = pltpu.roll(a, ATT_HEAD_DIM, axis=1)
        z = jnp.zeros_like(a)
        return ((jnp.where(low, a, z).astype(BF16), jnp.where(low, z, ar).astype(BF16)),
                (jnp.where(low, ar, z).astype(BF16), jnp.where(low, z, a).astype(BF16)))

    k_ops = lane_variants(k2)
    v_ops = lane_variants(v2)

    ti = lax.broadcasted_iota(jnp.int32, (WINDOW, 2 * WINDOW), 0)
    si = lax.broadcasted_iota(jnp.int32, (WINDOW, 2 * WINDOW), 1)
    dist = ti + WINDOW - si
    valid = (dist >= 0) & (dist < WINDOW) & ((n > 0) | (si >= WINDOW))
    distf = dist.astype(F32)
    scale = 1.0 / math.sqrt(ATT_HEAD_DIM)

    tiles = []
    for j in range(ATT_Q // LANES):
        qj = q[:, LANES * j:LANES * (j + 1)]
        kv = (2 * j) // (ATT_HEADS // 2)
        acc = jnp.zeros((WINDOW, LANES), F32)
        for par in range(2):
            hidx = 2 * j + par
            slope = 2.0 ** (-8.0 * (hidx + 1) / ATT_HEADS)
            s = lax.dot_general(qj, k_ops[kv][par], _NT, preferred_element_type=F32) * scale
            logits = jnp.where(valid, s - slope * distf, NEG_INF)
            sink = sink_ref[hidx]
            mx = jnp.maximum(jnp.max(logits, axis=-1, keepdims=True), sink)
            e = jnp.exp(logits - mx)
            den = jnp.sum(e, axis=-1, keepdims=True) + jnp.exp(sink - mx)
            p = (e / den).astype(BF16)
            acc = acc + jnp.dot(p, v_ops[kv][par], preferred_element_type=F32)
        tiles.append(acc)
    att = jnp.concatenate(tiles, axis=1)
    o_ref[...] = _rms(att, gn_ref[...]).astype(BF16)


def _swa(qa, ka, va, sinks, gn, bsz, seq):
    nb = seq // WINDOW
    t = bsz * seq
    cur = lambda b, n: (b * nb + n, 0)
    prev = lambda b, n: (b * nb + jnp.maximum(n - 1, 0), 0)
    return pl.pallas_call(
        _swa_kernel,
        grid=(bsz, nb),
        in_specs=[pl.BlockSpec(memory_space=pltpu.SMEM),
                  pl.BlockSpec((WINDOW, ATT_Q), cur),
                  pl.BlockSpec((WINDOW, ATT_KV), prev),
                  pl.BlockSpec((WINDOW, ATT_KV), cur),
                  pl.BlockSpec((WINDOW, ATT_KV), prev),
                  pl.BlockSpec((WINDOW, ATT_KV), cur),
                  pl.BlockSpec((1, ATT_Q), lambda b, n: (0, 0))],
        out_specs=pl.BlockSpec((WINDOW, ATT_Q), cur),
        out_shape=jax.ShapeDtypeStruct((t, ATT_Q), BF16),
        compiler_params=_cparams(2),
        name="swa",
    )(sinks, qa, ka, ka, va, va, gn)


def _split3(a):
    hi = a.astype(BF16)
    r1 = a - hi.astype(F32)
    mid = r1.astype(BF16)
    lo = (r1 - mid.astype(F32)).astype(BF16)
    return hi, mid, lo


def _mlstm_kernel(qk_ref, v_ref, o_ref, gi_ref, gf_ref, cw_ref, cb_ref, bi_ref, bf_ref, gn_ref,
                  out_ref, xbuf, cn_s, m_s):
    c = pl.program_id(1)
    L = qk_ref.shape[0]
    HD = ML_HEAD_DIM

    @pl.when(c == 0)
    def _():
        xbuf[0:8, :] = jnp.zeros((8, xbuf.shape[1]), F32)
        cn_s[...] = jnp.zeros(cn_s.shape, F32)
        m_s[...] = jnp.zeros(m_s.shape, F32)

    xbuf[8:8 + L, :] = qk_ref[...]
    y = cb_ref[...] + cw_ref[3:4, :] * xbuf[8:8 + L, :]
    for j in range(CONV_WIDTH - 1):
        sh = CONV_WIDTH - 1 - j
        y = y + cw_ref[j:j + 1, :] * xbuf[8 - sh:8 - sh + L, :]
    xbuf[0:8, :] = xbuf[L:L + 8, :]
    qk = y * jax.nn.sigmoid(y)
    q = qk[:, :ML_W]
    k = qk[:, ML_W:] * (1.0 / math.sqrt(HD))

    gi = gi_ref[...] + bi_ref[...]
    z = gf_ref[...] + bf_ref[...]
    lf = jnp.minimum(z, 0.0) - jnp.log1p(jnp.exp(-jnp.abs(z)))

    ti = lax.broadcasted_iota(jnp.int32, (L, L), 0)
    si = lax.broadcasted_iota(jnp.int32, (L, L), 1)
    causal = si <= ti
    tril = jnp.where(causal, 1.0, 0.0).astype(BF16)
    b = jnp.zeros((L, LANES), F32)
    for part in _split3(lf):
        b = b + jnp.dot(tril, part, preferred_element_type=F32)

    m_st = m_s[...]
    b_last = b[L - 1:L, :]
    m_inter = b + m_st
    log_w = b_last - b + gi
    m_new = jnp.maximum(b_last + m_st, jnp.max(log_w, axis=0, keepdims=True))
    w = jnp.exp(log_w - m_new)
    decay = jnp.exp(b_last + m_st - m_new)
    rt = jnp.transpose(gi - b)

    lane_l = lax.broadcasted_iota(jnp.int32, (L, HD), 1)
    ones_col = jnp.where(lane_l == 0, 1.0, 0.0).astype(BF16)

    for h in range(ML_HEADS):
        sl = slice(HD * h, HD * (h + 1))
        qh = q[:, sl].astype(BF16)
        kh = k[:, sl]
        vext = jnp.concatenate([v_ref[:, sl], ones_col], axis=1)
        log_d = jnp.where(causal, b[:, h:h + 1] + rt[h:h + 1, :], NEG_INF)
        m_t = jnp.maximum(m_inter[:, h:h + 1], jnp.max(log_d, axis=1, keepdims=True))
        d = jnp.exp(log_d - m_t)
        s = lax.dot_general(qh, kh.astype(BF16), _NT, preferred_element_type=F32) * d
        a_inter = jnp.exp(m_inter[:, h:h + 1] - m_t)
        cn = cn_s[h]
        nd = (jnp.dot(s.astype(BF16), vext, preferred_element_type=F32)
              + a_inter * jnp.dot(qh, cn.astype(BF16), preferred_element_type=F32))
        num = nd[:, :HD]
        den = nd[:, HD:HD + 1]
        hh = num / jnp.maximum(jnp.abs(den), jnp.exp(-m_t))

        kw = (kh * w[:, h:h + 1]).astype(BF16)
        upd = lax.dot_general(kw, vext, _TN, preferred_element_type=F32)
        cn_s[h] = decay[:, h:h + 1] * cn + upd

        hm = jax.nn.sigmoid(o_ref[:, sl]) * hh
        out_ref[:, sl] = _rms(hm, gn_ref[:, sl]).astype(BF16)

    m_s[...] = m_new


def _mlstm(qk, vm, om, gi, gf, cw, cb, bi, bf, gn, bsz, seq):
    L = ML_CHUNK
    nc = seq // L
    t = bsz * seq
    row = lambda b, c: (b * nc + c, 0)
    const = lambda b, c: (0, 0)
    return pl.pallas_call(
        _mlstm_kernel,
        grid=(bsz, nc),
        in_specs=[pl.BlockSpec((L, 2 * ML_W), row),
                  pl.BlockSpec((L, ML_W), row),
                  pl.BlockSpec((L, ML_W), row),
                  pl.BlockSpec((L, LANES), row),
                  pl.BlockSpec((L, LANES), row),
                  pl.BlockSpec((CONV_WIDTH, 2 * ML_W), const),
                  pl.BlockSpec((1, 2 * ML_W), const),
                  pl.BlockSpec((1, LANES), const),
                  pl.BlockSpec((1, LANES), const),
                  pl.BlockSpec((1, ML_W), const)],
        out_specs=pl.BlockSpec((L, ML_W), row),
        out_shape=jax.ShapeDtypeStruct((t, ML_W), BF16),
        scratch_shapes=[pltpu.VMEM((L + 8, 2 * ML_W), F32),
                        pltpu.VMEM((ML_HEADS, ML_HEAD_DIM, 2 * ML_HEAD_DIM), F32),
                        pltpu.VMEM((1, LANES), F32)],
        compiler_params=_cparams(2),
        name="mlstm",
    )(qk, vm, om, gi, gf, cw, cb, bi, bf, gn)


def _kv_kernel(mem_ref, g_ref, w_ref, k_ref, v_ref):
    mn = _rms(mem_ref[...], g_ref[...]).astype(BF16)
    hw = k_ref.shape[1]
    k_ref[...] = jnp.dot(mn, w_ref[:, :hw], preferred_element_type=F32).astype(BF16)
    v_ref[...] = jnp.dot(mn, w_ref[:, hw:], preferred_element_type=F32).astype(BF16)


def _kv_prep(mem2d, g, w_ckv):
    r, d = mem2d.shape
    hw = w_ckv.shape[1] // 2
    tm = 256
    return pl.pallas_call(
        _kv_kernel,
        grid=(r // tm,),
        in_specs=[pl.BlockSpec((tm, d), lambda i: (i, 0)),
                  pl.BlockSpec((1, d), lambda i: (0, 0)),
                  pl.BlockSpec(w_ckv.shape, lambda i: (0, 0))],
        out_specs=[pl.BlockSpec((tm, hw), lambda i: (i, 0))] * 2,
        out_shape=[jax.ShapeDtypeStruct((r, hw), BF16)] * 2,
        compiler_params=_cparams(1),
        name="kv_prep",
    )(mem2d, g, w_ckv)


_R_E0, _R_E1, _R_RANK0, _R_RANK1, _R_P0, _R_P1 = range(6)
_ROUTER_LANE0 = N_GROUPS


_PACK_ROWS = 8


def _store_slabs(ref, v, rows, stage):
    for r in range(_PACK_ROWS):
        stage[pl.ds(r, rows, stride=_PACK_ROWS), :] = v[:, LANES * r:LANES * (r + 1)]
    ref[...] = stage[...].astype(ref.dtype)


def _load_slabs(ref, rows, stage):
    stage[...] = ref[...].astype(F32)
    return jnp.concatenate(
        [stage[pl.ds(r, rows, stride=_PACK_ROWS), :] for r in range(_PACK_ROWS)], axis=1)


def _mid_kernel(x_ref, att_ref, hm_ref, wo_ref, gc_ref, wq_ref, k_ref, v_ref, wco_ref, gf_ref,
                wr_ref, br_ref, x2_ref, xn_ref, route_ref, idx_ref, cnt_ref, carry_s, stage_s):
    i = pl.program_id(0)
    tm = x_ref.shape[0]

    @pl.when(i == 0)
    def _():
        carry_s[...] = jnp.zeros(carry_s.shape, F32)

    x1 = (x_ref[...]
          + jnp.dot(att_ref[...], wo_ref[0:ATT_Q, :], preferred_element_type=F32)
          + jnp.dot(hm_ref[...], wo_ref[ATT_Q:, :], preferred_element_type=F32))

    xc = _rms(x1, gc_ref[...]).astype(BF16)
    qb = jnp.dot(xc, wq_ref[...], preferred_element_type=F32).astype(BF16)
    heads = []
    for h in range(X_HEADS):
        sl = slice(X_HEAD_DIM * h, X_HEAD_DIM * (h + 1))
        s = lax.dot_general(qb[:, sl], k_ref[:, sl], _NT,
                            preferred_element_type=F32) * (1.0 / math.sqrt(X_HEAD_DIM))
        e = jnp.exp(s - jnp.max(s, axis=-1, keepdims=True))
        p = (e / jnp.sum(e, axis=-1, keepdims=True)).astype(BF16)
        heads.append(jnp.dot(p, v_ref[:, sl], preferred_element_type=F32))
    o = jnp.concatenate(heads, axis=1).astype(BF16)
    x2 = x1 + jnp.dot(o, wco_ref[...], preferred_element_type=F32)
    x2_ref[...] = x2

    xn = _rms(x2, gf_ref[...])
    xh = xn.astype(BF16)
    _store_slabs(xn_ref, xn, tm, stage_s)
    xl = (xn - xh.astype(F32)).astype(BF16)
    lg2 = jnp.dot(xh, wr_ref[...], preferred_element_type=F32)
    logits = (lg2[:, :LANES] + lg2[:, LANES:]
              + jnp.dot(xl, wr_ref[:, :LANES], preferred_element_type=F32) + br_ref[...])

    lane = lax.broadcasted_iota(jnp.int32, (tm, LANES), 1)
    lanef = lane.astype(F32)
    big = float(4 * LANES)
    gl = jnp.where(lane < N_GROUPS, logits, NEG_INF)
    gmax = jnp.max(gl, axis=-1, keepdims=True)
    gsel = jnp.min(jnp.where(gl == gmax, lanef, big), axis=-1, keepdims=True)
    gw = 1.0 / jnp.sum(jnp.exp(gl - gmax), axis=-1, keepdims=True)
    lo_lane = _ROUTER_LANE0 + EXPERTS_PER_GROUP * gsel
    in_group = (lanef >= lo_lane) & (lanef < lo_lane + EXPERTS_PER_GROUP)
    el = jnp.where(in_group, logits, NEG_INF)
    v0 = jnp.max(el, axis=-1, keepdims=True)
    i0 = jnp.min(jnp.where(el == v0, lanef, big), axis=-1, keepdims=True)
    el2 = jnp.where(lanef == i0, NEG_INF, el)
    v1 = jnp.max(el2, axis=-1, keepdims=True)
    i1 = jnp.min(jnp.where(el2 == v1, lanef, big), axis=-1, keepdims=True)
    tt = jnp.exp(v1 - v0)
    p0 = gw / (1.0 + tt)
    p1 = gw * tt / (1.0 + tt)

    sel0 = lanef == i0
    sel1 = lanef == i1
    mb = jnp.where(sel0 | sel1, 1.0, 0.0).astype(BF16)
    ti = lax.broadcasted_iota(jnp.int32, (tm, tm), 0)
    si = lax.broadcasted_iota(jnp.int32, (tm, tm), 1)
    strict = jnp.where(si < ti, 1.0, 0.0).astype(BF16)
    carry = carry_s[...]
    pref = jnp.dot(strict, mb, preferred_element_type=F32) + carry
    r0 = jnp.sum(jnp.where(sel0, pref, 0.0), axis=-1, keepdims=True)
    r1 = jnp.sum(jnp.where(sel1, pref, 0.0), axis=-1, keepdims=True)
    carry = carry + jnp.sum(mb.astype(F32), axis=0, keepdims=True)
    carry_s[...] = carry
    cnt_ref[...] = carry

    route = jnp.zeros((tm, LANES), F32)
    for idx, col in ((_R_E0, i0 - _ROUTER_LANE0), (_R_E1, i1 - _ROUTER_LANE0), (_R_RANK0, r0),
                     (_R_RANK1, r1), (_R_P0, p0), (_R_P1, p1)):
        route = jnp.where(lane == idx, col, route)
    route_ref[...] = route
    idx_ref[...] = jnp.transpose(route)[0:8, :].astype(jnp.int32)


def _mid(x2d, att, hm, wo, gc, wq, kmem, vmem_, wco, gf, wr, br, seq):
    t, d = x2d.shape
    tm = TM_MID
    per_b = seq // tm
    rowmap = lambda i: (i, 0)
    const = lambda i: (0, 0)
    bmap = lambda i: (i // per_b, 0)
    mem_len = kmem.shape[0] // (t // seq)
    return pl.pallas_call(
        _mid_kernel,
        grid=(t // tm,),
        in_specs=[pl.BlockSpec((tm, d), rowmap),
                  pl.BlockSpec((tm, ATT_Q), rowmap),
                  pl.BlockSpec((tm, ML_W), rowmap),
                  pl.BlockSpec(wo.shape, const),
                  pl.BlockSpec((1, d), const),
                  pl.BlockSpec(wq.shape, const),
                  pl.BlockSpec((mem_len, kmem.shape[1]), bmap),
                  pl.BlockSpec((mem_len, vmem_.shape[1]), bmap),
                  pl.BlockSpec(wco.shape, const),
                  pl.BlockSpec((1, d), const),
                  pl.BlockSpec(wr.shape, const),
                  pl.BlockSpec((1, LANES), const)],
        out_specs=[pl.BlockSpec((tm, d), rowmap),
                   pl.BlockSpec((tm * _PACK_ROWS, LANES), rowmap),
                   pl.BlockSpec((tm, LANES), rowmap),
                   pl.BlockSpec((8, tm), lambda i: (0, i)),
                   pl.BlockSpec((1, LANES), const)],
        out_shape=[jax.ShapeDtypeStruct((t, d), F32),
                   jax.ShapeDtypeStruct((t * _PACK_ROWS, LANES), BF16),
                   jax.ShapeDtypeStruct((t, LANES), F32),
                   jax.ShapeDtypeStruct((8, t), jnp.int32),
                   jax.ShapeDtypeStruct((1, LANES), F32)],
        scratch_shapes=[pltpu.VMEM((1, LANES), F32),
                        pltpu.VMEM((tm * _PACK_ROWS, LANES), F32)],
        compiler_params=_cparams(1),
        name="mid",
    )(x2d, att, hm, wo, gc, wq, kmem, vmem_, wco, gf, wr, br)


_M_BLOCK_E, _M_PAD_END, _M_NUSED = range(3)
_META_LANES = 2 * LANES


def _plan_kernel(cnt_ref, idx_ref, dest_ref, meta_ref):
    cnt = cnt_ref[...]
    lane = lax.broadcasted_iota(jnp.int32, (1, LANES), 1)
    is_expert = (lane >= _ROUTER_LANE0) & (lane < _ROUTER_LANE0 + N_EXPERTS)
    nblk = jnp.where(is_expert, jnp.floor((cnt + (BM - 1)) * (1.0 / BM)), 0.0)
    jj = lax.broadcasted_iota(jnp.int32, (LANES, LANES), 0)
    kk = lax.broadcasted_iota(jnp.int32, (LANES, LANES), 1)
    upper = jnp.where(jj <= kk, 1.0, 0.0).astype(BF16)
    pend_blk = jnp.dot(jnp.broadcast_to(nblk, (8, LANES)).astype(BF16), upper,
                       preferred_element_type=F32)[0:1, :]
    pstart_rows = (pend_blk - nblk) * BM
    pend_rows = pend_blk * BM

    idx = idx_ref[...]
    off = jnp.zeros(idx.shape, F32)
    for e in range(N_EXPERTS):
        lane_e = _ROUTER_LANE0 + e
        off = jnp.where(idx == e, pstart_rows[:, lane_e:lane_e + 1], off)
    ranks = pltpu.roll(idx, idx.shape[0] - 2, axis=0)
    dest_ref[...] = ranks + off.astype(jnp.int32)

    blk = lax.broadcasted_iota(jnp.int32, (1, _META_LANES), 1).astype(F32)
    block_e = jnp.zeros((1, _META_LANES), F32)
    for e in range(N_EXPERTS):
        lane_e = _ROUTER_LANE0 + e
        block_e = block_e + jnp.where(pend_blk[:, lane_e:lane_e + 1] <= blk, 1.0, 0.0)
    block_e = jnp.minimum(block_e, N_EXPERTS - 1.0)
    last = _ROUTER_LANE0 + N_EXPERTS - 1
    nused = pend_blk[:, last:last + 1]
    pend_wide = jnp.concatenate([pend_rows, jnp.zeros((1, _META_LANES - LANES), F32)], axis=1)
    sub = lax.broadcasted_iota(jnp.int32, (8, _META_LANES), 0)
    meta = jnp.where(sub == _M_BLOCK_E, block_e, jnp.where(sub == _M_PAD_END, pend_wide, nused))
    meta_ref[...] = meta.astype(jnp.int32)


def _plan(cnt, idx):
    t = idx.shape[1]
    return pl.pallas_call(
        _plan_kernel,
        grid=(1,),
        in_specs=[pl.BlockSpec(cnt.shape, lambda i: (0, 0)),
                  pl.BlockSpec(idx.shape, lambda i: (0, 0))],
        out_specs=[pl.BlockSpec((8, t), lambda i: (0, 0)),
                   pl.BlockSpec((8, _META_LANES), lambda i: (0, 0))],
        out_shape=[jax.ShapeDtypeStruct((8, t), jnp.int32),
                   jax.ShapeDtypeStruct((8, _META_LANES), jnp.int32)],
        compiler_params=_cparams(1),
        name="plan",
    )(cnt, idx)


_SLAB = BM * _PACK_ROWS


def _dispatch_kernel(meta_ref, d_ref, xn_ref, xpad_ref, zbuf, sem, zsem):
    i = pl.program_id(0)
    td = xn_ref.shape[0] // _PACK_ROWS

    def zero_copy(e):
        pend = meta_ref[_M_PAD_END, _ROUTER_LANE0 + e]
        pstart = meta_ref[_M_PAD_END, _ROUTER_LANE0 + e - 1]
        first = pl.multiple_of((pend - BM) * _PACK_ROWS, _SLAB)
        cp = pltpu.make_async_copy(zbuf, xpad_ref.at[pl.ds(first, _SLAB), :], zsem)
        return pend > pstart, cp

    @pl.when(i == 0)
    def _():
        zbuf[...] = jnp.zeros(zbuf.shape, zbuf.dtype)
        for e in range(N_EXPERTS):
            nonempty, cp = zero_copy(e)
            pl.when(nonempty)(cp.start)
        for e in range(N_EXPERTS):
            nonempty, cp = zero_copy(e)
            pl.when(nonempty)(cp.wait)

        def tail_copy(b):
            return pltpu.make_async_copy(
                zbuf, xpad_ref.at[pl.ds(pl.multiple_of(b * _SLAB, _SLAB), _SLAB), :], zsem)

        first_unused = meta_ref[_M_NUSED, 0]
        n_blocks = xpad_ref.shape[0] // _SLAB
        lax.fori_loop(first_unused, n_blocks, lambda b, c: (tail_copy(b).start(), c)[1], 0)
        lax.fori_loop(first_unused, n_blocks, lambda b, c: (tail_copy(b).wait(), c)[1], 0)

    def row_copy(t, dst):
        src = pl.ds(pl.multiple_of(t * _PACK_ROWS, _PACK_ROWS), _PACK_ROWS)
        dsl = pl.ds(pl.multiple_of(dst * _PACK_ROWS, _PACK_ROWS), _PACK_ROWS)
        return pltpu.make_async_copy(xn_ref.at[src, :], xpad_ref.at[dsl, :], sem)

    def start(t, carry):
        row_copy(t, d_ref[0, t]).start(priority=0)
        row_copy(t, d_ref[1, t]).start(priority=1)
        return carry

    def wait(t, carry):
        row_copy(0, 0).wait()
        row_copy(0, 0).wait()
        return carry

    lax.fori_loop(0, td, start, 0, unroll=8)
    lax.fori_loop(0, td, wait, 0, unroll=8)


def _dispatch(meta, dest, xn_packed, rows):
    t = dest.shape[1]
    td = TD
    return pl.pallas_call(
        _dispatch_kernel,
        grid_spec=pltpu.PrefetchScalarGridSpec(
            num_scalar_prefetch=1,
            grid=(t // td,),
            in_specs=[pl.BlockSpec((8, td), lambda i, m: (0, i), memory_space=pltpu.SMEM),
                      pl.BlockSpec((td * _PACK_ROWS, LANES), lambda i, m: (i, 0))],
            out_specs=pl.BlockSpec(memory_space=pl.ANY),
            scratch_shapes=[pltpu.VMEM((_SLAB, LANES), BF16),
                            pltpu.SemaphoreType.DMA(()),
                            pltpu.SemaphoreType.DMA(())]),
        out_shape=jax.ShapeDtypeStruct((rows * _PACK_ROWS, LANES), BF16),
        compiler_params=_cparams(1),
        name="dispatch",
    )(meta, dest, xn_packed)


def _expert_kernel(meta_ref, x_ref, wg_ref, wu_ref, wd_ref, y_ref, wg_s, wu_s, wd_s, xstage_s,
                   ystage_s):
    i = pl.program_id(0)
    nused = meta_ref[_M_NUSED, 0]

    @pl.when(i < nused)
    def _():
        prev = meta_ref[_M_BLOCK_E, jnp.maximum(i - 1, 0)]
        changed = (i == 0) | (meta_ref[_M_BLOCK_E, i] != prev)

        @pl.when(changed)
        def _():
            wg_s[...] = wg_ref[...].astype(BF16)
            wu_s[...] = wu_ref[...].astype(BF16)
            wd_s[...] = wd_ref[...].astype(BF16)

        xb = _load_slabs(x_ref, BM, xstage_s).astype(BF16)
        g = jnp.dot(xb, wg_s[...], preferred_element_type=F32)
        u = jnp.dot(xb, wu_s[...], preferred_element_type=F32)
        hb = (g * jax.nn.sigmoid(g) * u).astype(BF16)
        y = jnp.dot(hb, wd_s[...], preferred_element_type=F32)
        _store_slabs(y_ref, y, BM, ystage_s)

    @pl.when(i >= nused)
    def _():
        y_ref[...] = jnp.zeros(y_ref.shape, y_ref.dtype)


def _experts(meta, xpad, wg, wu, wd):
    nbk = xpad.shape[0] // _SLAB
    d, de = wg.shape[1], wg.shape[2]
    last_used = lambda i, m: jnp.minimum(i, m[_M_NUSED, 0] - 1)
    blk = lambda i, m: (last_used(i, m), 0)
    oblk = lambda i, m: (i, 0)
    wmap = lambda i, m: (m[_M_BLOCK_E, last_used(i, m)], 0, 0)
    return pl.pallas_call(
        _expert_kernel,
        grid_spec=pltpu.PrefetchScalarGridSpec(
            num_scalar_prefetch=1,
            grid=(nbk,),
            in_specs=[pl.BlockSpec((_SLAB, LANES), blk),
                      pl.BlockSpec((None, d, de), wmap),
                      pl.BlockSpec((None, d, de), wmap),
                      pl.BlockSpec((None, de, d), wmap)],
            out_specs=pl.BlockSpec((_SLAB, LANES), oblk),
            scratch_shapes=[pltpu.VMEM((d, de), BF16),
                            pltpu.VMEM((d, de), BF16),
                            pltpu.VMEM((de, d), BF16),
                            pltpu.VMEM((_SLAB, LANES), F32),
                            pltpu.VMEM((_SLAB, LANES), F32)]),
        out_shape=jax.ShapeDtypeStruct(xpad.shape, BF16),
        compiler_params=_cparams(1),
        name="experts",
    )(meta, xpad, wg, wu, wd)


def _combine_kernel(dc_ref, dn_ref, x2_ref, route_ref, g_ref, ypad_ref, out_ref, ybuf, sem,
                    stage0_s, stage1_s):
    i = pl.program_id(0)
    n = pl.num_programs(0)
    tc = x2_ref.shape[0]

    def row_copy(src, slot, which, t):
        ssl = pl.ds(pl.multiple_of(src * _PACK_ROWS, _PACK_ROWS), _PACK_ROWS)
        dsl = pl.ds(pl.multiple_of(t * _PACK_ROWS, _PACK_ROWS), _PACK_ROWS)
        return pltpu.make_async_copy(ypad_ref.at[ssl, :], ybuf.at[slot, which, dsl, :],
                                     sem.at[slot])

    def issue(d_ref, slot):
        def body(t, carry):
            row_copy(d_ref[0, t], slot, 0, t).start(priority=0)
            row_copy(d_ref[1, t], slot, 1, t).start(priority=1)
            return carry
        lax.fori_loop(0, tc, body, 0, unroll=8)

    slot = i % 2

    @pl.when(i == 0)
    def _():
        issue(dc_ref, 0)

    @pl.when(i + 1 < n)
    def _():
        issue(dn_ref, 1 - slot)

    def wait(t, carry):
        row_copy(0, slot, 0, 0).wait()
        row_copy(0, slot, 1, 0).wait()
        return carry
    lax.fori_loop(0, tc, wait, 0, unroll=8)

    route = route_ref[...]
    p0 = route[:, _R_P0:_R_P0 + 1]
    p1 = route[:, _R_P1:_R_P1 + 1]
    y0 = _load_slabs(ybuf.at[slot, 0], tc, stage0_s)
    y1 = _load_slabs(ybuf.at[slot, 1], tc, stage1_s)
    x3 = x2_ref[...] + p0 * y0 + p1 * y1
    out_ref[...] = _rms(x3, g_ref[...])


def _combine(dest, x2, route, g, ypad):
    t, d = x2.shape
    tc = TC
    nt = t // tc
    cur = pl.BlockSpec((8, tc), lambda i: (0, i), memory_space=pltpu.SMEM)
    nxt = pl.BlockSpec((8, tc), lambda i: (0, jnp.minimum(i + 1, nt - 1)),
                       memory_space=pltpu.SMEM)
    return pl.pallas_call(
        _combine_kernel,
        grid=(nt,),
        in_specs=[cur, nxt,
                  pl.BlockSpec((tc, d), lambda i: (i, 0)),
                  pl.BlockSpec((tc, LANES), lambda i: (i, 0)),
                  pl.BlockSpec((1, d), lambda i: (0, 0)),
                  pl.BlockSpec(memory_space=pl.ANY)],
        out_specs=pl.BlockSpec((tc, d), lambda i: (i, 0)),
        out_shape=jax.ShapeDtypeStruct((t, d), F32),
        scratch_shapes=[pltpu.VMEM((2, 2, tc * _PACK_ROWS, LANES), BF16),
                        pltpu.SemaphoreType.DMA((2,)),
                        pltpu.VMEM((tc * _PACK_ROWS, LANES), F32),
                        pltpu.VMEM((tc * _PACK_ROWS, LANES), F32)],
        compiler_params=_cparams(1),
        name="combine",
    )(dest, dest, x2, route, g, ypad)


def _pad_cols(a, width):
    return jnp.pad(a, ((0, 0), (0, width - a.shape[1])))


def kernel(x, mem, norm_mix, w_in, b_gates, conv_w, conv_b, att_sinks, norm_att_out, norm_ml_out,
           w_out, norm_cross, norm_mem, w_cq, w_ckv, w_co, norm_ffn, w_router_group,
           b_router_group, w_router_expert, b_router_expert, w_e_gate, w_e_up, w_e_down,
           norm_final):
    bsz, seq, d = x.shape
    t = bsz * seq
    depth = w_in.shape[0]
    assert depth == 1, "the final RMSNorm is fused into the last layer's combine kernel"
    xs = x.reshape(t, d)
    mem2d = mem.reshape(-1, d)

    for l in range(depth):
        n_main = _SEG["om"][1]
        wl = w_in[l]
        w_packed = jnp.concatenate(
            [wl[:, :n_main], _pad_cols(wl[:, n_main:n_main + ML_HEADS], LANES),
             _pad_cols(wl[:, n_main + ML_HEADS:], LANES)], axis=1).astype(BF16)
        bi = _pad_cols(b_gates[l][None, :ML_HEADS], LANES)
        bf = _pad_cols(b_gates[l][None, ML_HEADS:], LANES)
        w_r = _pad_cols(jnp.concatenate([w_router_group[l], w_router_expert[l]], axis=1), LANES)
        w_r_hi = w_r.astype(BF16)
        w_r_lo = (w_r - w_r_hi.astype(F32)).astype(BF16)
        w_r_packed = jnp.concatenate([w_r_hi, w_r_lo], axis=1)
        b_r = _pad_cols(jnp.concatenate([b_router_group[l], b_router_expert[l]])[None, :], LANES)

        qa, ka, va, qk, vm, om, gi, gf = _in_proj(xs, norm_mix[l][None, :], w_packed)
        att = _swa(qa, ka, va, att_sinks[l], norm_att_out[l][None, :], bsz, seq)
        hm = _mlstm(qk, vm, om, gi, gf, conv_w[l], conv_b[l][None, :], bi, bf,
                    norm_ml_out[l][None, :], bsz, seq)

        kmem, vmem_ = _kv_prep(mem2d, norm_mem[l][None, :], w_ckv[l].astype(BF16))
        x2, xn, route, idx, cnt = _mid(xs, att, hm, w_out[l].astype(BF16), norm_cross[l][None, :],
                                       w_cq[l].astype(BF16), kmem, vmem_, w_co[l].astype(BF16),
                                       norm_ffn[l][None, :], w_r_packed, b_r, seq)

        nbk = (t * TOP_K) // BM + N_EXPERTS
        assert nbk <= _META_LANES
        dest, meta = _plan(cnt, idx)
        xpad = _dispatch(meta, dest, xn, nbk * BM)
        ypad = _experts(meta, xpad, w_e_gate[l], w_e_up[l], w_e_down[l])
        xs = _combine(dest, x2, route, norm_final[None, :], ypad)
    return xs.reshape(bsz, seq, d)
```

```python
import functools
import math

import jax
import jax.numpy as jnp
from jax import lax
from jax.experimental import pallas as pl
from jax.experimental.pallas import tpu as pltpu

F32 = jnp.float32
BF16 = jnp.bfloat16
EPS = 1e-6
NEG_INF = float("-inf")

ATT_HEADS = 8
ATT_HEAD_DIM = 64
ATT_Q = ATT_HEADS * ATT_HEAD_DIM
ATT_KV = 128
WINDOW = 128
ML_HEADS = 4
ML_HEAD_DIM = 128
ML_W = ML_HEADS * ML_HEAD_DIM
CONV_WIDTH = 4
X_HEADS = 4
X_HEAD_DIM = 128
N_GROUPS = 4
EXPERTS_PER_GROUP = 8
N_EXPERTS = N_GROUPS * EXPERTS_PER_GROUP
TOP_K = 2

LANES = 128
TM_IN = 512
SWA_QB = 2
ML_CHUNK = 128
ML_NB = 1
TM_MID = 512
MID_PARTS = 2
TD = 512
TC = 256
BM = 256
VMEM_LIMIT = 56 * 1024 * 1024

_NT = (((1,), (1,)), ((), ()))
_TN = (((0,), (0,)), ((), ()))


def _rms(x, g):
    return x * lax.rsqrt(jnp.mean(x * x, axis=-1, keepdims=True) + EPS) * g


def _cparams(n_axes):
    return pltpu.CompilerParams(dimension_semantics=("arbitrary",) * n_axes,
                                vmem_limit_bytes=VMEM_LIMIT)


_SEG = {"qa": (0, 512), "ka": (512, 640), "va": (640, 768), "qk": (768, 1792),
        "vm": (1792, 2304), "om": (2304, 2816), "gi": (2816, 2944), "gf": (2944, 3072)}


def _in_proj_kernel(x_ref, g_ref, w_ref, qa_ref, ka_ref, va_ref, qk_ref, vm_ref, om_ref,
                    gi_ref, gf_ref):
    hb = _rms(x_ref[...], g_ref[...]).astype(BF16)

    def seg(name):
        lo, hi = _SEG[name]
        return jnp.dot(hb, w_ref[:, lo:hi], preferred_element_type=F32)

    qa_ref[...] = seg("qa").astype(BF16)
    ka_ref[...] = seg("ka").astype(BF16)
    va_ref[...] = seg("va").astype(BF16)
    qk_ref[...] = seg("qk")
    vm_ref[...] = seg("vm").astype(BF16)
    om_ref[...] = seg("om")
    gi_ref[...] = seg("gi")
    gf_ref[...] = seg("gf")


def _in_proj(x2d, g, w_packed):
    t, d = x2d.shape
    tm = TM_IN
    widths = [(512, BF16), (128, BF16), (128, BF16), (1024, F32), (512, BF16), (512, F32),
              (128, F32), (128, F32)]
    return pl.pallas_call(
        _in_proj_kernel,
        grid=(t // tm,),
        in_specs=[pl.BlockSpec((tm, d), lambda i: (i, 0)),
                  pl.BlockSpec((1, d), lambda i: (0, 0)),
                  pl.BlockSpec(w_packed.shape, lambda i: (0, 0))],
        out_specs=[pl.BlockSpec((tm, w), lambda i: (i, 0)) for w, _ in widths],
        out_shape=[jax.ShapeDtypeStruct((t, w), dt) for w, dt in widths],
        compiler_params=_cparams(1),
        name="in_proj",
    )(x2d, g, w_packed)


def _swa_kernel(sink_ref, q_ref, kp_ref, kc_ref, vp_ref, vc_ref, gn_ref, o_ref, bias_s):
    b = pl.program_id(0)
    n = pl.program_id(1)

    @pl.when((b == 0) & (n == 0))
    def _():
        ti = lax.broadcasted_iota(jnp.int32, (WINDOW, 2 * WINDOW), 0)
        si = lax.broadcasted_iota(jnp.int32, (WINDOW, 2 * WINDOW), 1)
        dist = ti + WINDOW - si
        band = (dist >= 0) & (dist < WINDOW)
        distf = dist.astype(F32)
        for hidx in range(ATT_HEADS):
            slope = 2.0 ** (-8.0 * (hidx + 1) / ATT_HEADS)
            full = jnp.where(band, -slope * distf, NEG_INF)
            bias_s[1, hidx] = full
            bias_s[0, hidx] = jnp.where(si >= WINDOW, full, NEG_INF)

    scale = 1.0 / math.sqrt(ATT_HEAD_DIM)
    k_all = jnp.concatenate([kp_ref[...], kc_ref[...]], axis=0).astype(F32) * scale
    v_all = jnp.concatenate([vp_ref[...], vc_ref[...]], axis=0).astype(F32)
    lane = lax.broadcasted_iota(jnp.int32, k_all.shape, 1)
    low = lane < ATT_HEAD_DIM

    def lane_variants(a):
        ar = pltpu.roll(a, ATT_HEAD_DIM, axis=1)
        z = jnp.zeros_like(a)
        return ((jnp.where(low, a, z).astype(BF16), jnp.where(low, z, ar).astype(BF16)),
                (jnp.where(low, ar, z).astype(BF16), jnp.where(low, z, a).astype(BF16)))

    k_ops = lane_variants(k_all)
    v_ops = lane_variants(v_all)
    n_tiles = ATT_Q // LANES
    kv_of = lambda j: (2 * j) // (ATT_HEADS // 2)
    blocks = range(SWA_QB)
    rows = [slice(WINDOW * i, WINDOW * (i + 1)) for i in blocks]
    keys = [slice(WINDOW * i, WINDOW * (i + 2)) for i in blocks]
    bias_slot = [jnp.where(n == 0, 0, 1) if i == 0 else 1 for i in blocks]

    scores = [[[lax.dot_general(q_ref[rows[i], LANES * j:LANES * (j + 1)],
                                k_ops[kv_of(j)][par][keys[i]], _NT, preferred_element_type=F32)
                for par in range(2)] for j in range(n_tiles)] for i in blocks]
    probs = [[[None, None] for _ in range(n_tiles)] for _ in blocks]
    rinv = [[[None, None] for _ in range(n_tiles)] for _ in blocks]
    for i in blocks:
        for j in range(n_tiles):
            for par in range(2):
                hidx = 2 * j + par
                logits = scores[i][j][par] + bias_s[bias_slot[i], hidx]
                sink = sink_ref[hidx]
                mx = jnp.maximum(jnp.max(logits, axis=-1, keepdims=True), sink)
                e = jnp.exp(logits - mx)
                den = jnp.sum(e, axis=-1, keepdims=True) + jnp.exp(sink - mx)
                probs[i][j][par] = e.astype(BF16)
                rinv[i][j][par] = 1.0 / den
    lane_o = lax.broadcasted_iota(jnp.int32, (WINDOW, LANES), 1)
    for i in blocks:
        tiles = []
        for j in range(n_tiles):
            acc = (jnp.dot(probs[i][j][0], v_ops[kv_of(j)][0][keys[i]], preferred_element_type=F32)
                   + jnp.dot(probs[i][j][1], v_ops[kv_of(j)][1][keys[i]],
                             preferred_element_type=F32))
            tiles.append(acc * jnp.where(lane_o < ATT_HEAD_DIM, rinv[i][j][0], rinv[i][j][1]))
        att = jnp.concatenate(tiles, axis=1)
        o_ref[rows[i], :] = _rms(att, gn_ref[...]).astype(BF16)


def _swa(qa, ka, va, sinks, gn, bsz, seq):
    nb = seq // WINDOW
    ns = nb // SWA_QB
    t = bsz * seq
    cur = lambda b, n: (b * ns + n, 0)
    prev = lambda b, n: (b * nb + jnp.maximum(n * SWA_QB - 1, 0), 0)
    return pl.pallas_call(
        _swa_kernel,
        grid=(bsz, ns),
        in_specs=[pl.BlockSpec(memory_space=pltpu.SMEM),
                  pl.BlockSpec((SWA_QB * WINDOW, ATT_Q), cur),
                  pl.BlockSpec((WINDOW, ATT_KV), prev),
                  pl.BlockSpec((SWA_QB * WINDOW, ATT_KV), cur),
                  pl.BlockSpec((WINDOW, ATT_KV), prev),
                  pl.BlockSpec((SWA_QB * WINDOW, ATT_KV), cur),
                  pl.BlockSpec((1, ATT_Q), lambda b, n: (0, 0))],
        out_specs=pl.BlockSpec((SWA_QB * WINDOW, ATT_Q), cur),
        out_shape=jax.ShapeDtypeStruct((t, ATT_Q), BF16),
        scratch_shapes=[pltpu.VMEM((2, ATT_HEADS, WINDOW, 2 * WINDOW), F32)],
        compiler_params=_cparams(2),
        name="swa",
    )(sinks, qa, ka, ka, va, va, gn)


def _split3(a):
    hi = a.astype(BF16)
    r1 = a - hi.astype(F32)
    mid = r1.astype(BF16)
    lo = (r1 - mid.astype(F32)).astype(BF16)
    return hi, mid, lo


def _mlstm_kernel(qk_ref, v_ref, o_ref, gi_ref, gf_ref, cw_ref, cb_ref, bi_ref, bf_ref, gn_ref,
                  out_ref, xbuf, cn_s, m_s):
    c = pl.program_id(1)
    nseq, L = qk_ref.shape[0], qk_ref.shape[1]
    HD = ML_HEAD_DIM
    seqs = range(nseq)
    heads = range(ML_HEADS)
    pairs = [(s, h) for s in seqs for h in heads]
    sls = [slice(HD * h, HD * (h + 1)) for h in heads]

    @pl.when(c == 0)
    def _():
        xbuf[:, 0:8, :] = jnp.zeros((nseq, 8, xbuf.shape[2]), F32)
        cn_s[...] = jnp.zeros(cn_s.shape, F32)
        m_s[...] = jnp.zeros(m_s.shape, F32)

    ti = lax.broadcasted_iota(jnp.int32, (L, L), 0)
    si = lax.broadcasted_iota(jnp.int32, (L, L), 1)
    causal = si <= ti
    tril = jnp.where(causal, 1.0, 0.0).astype(BF16)
    lane_l = lax.broadcasted_iota(jnp.int32, (L, HD), 1)
    ones_col = jnp.where(lane_l == 0, 1.0, 0.0).astype(BF16)

    gi, b = [], []
    for s in seqs:
        gi.append(gi_ref[s] + bi_ref[...])
        z = gf_ref[s] + bf_ref[...]
        lf = jnp.minimum(z, 0.0) - jnp.log1p(jnp.exp(-jnp.abs(z)))
        acc = jnp.zeros((L, LANES), F32)
        for part in _split3(lf):
            acc = acc + jnp.dot(tril, part, preferred_element_type=F32)
        b.append(acc)

    q, k = [], []
    for s in seqs:
        xbuf[s, 8:8 + L, :] = qk_ref[s]
        y = cb_ref[...] + cw_ref[3:4, :] * xbuf[s, 8:8 + L, :]
        for j in range(CONV_WIDTH - 1):
            sh = CONV_WIDTH - 1 - j
            y = y + cw_ref[j:j + 1, :] * xbuf[s, 8 - sh:8 - sh + L, :]
        xbuf[s, 0:8, :] = xbuf[s, L:L + 8, :]
        qk = y * jax.nn.sigmoid(y)
        q.append(qk[:, :ML_W])
        k.append(qk[:, ML_W:] * (1.0 / math.sqrt(HD)))

    qb = {(s, h): q[s][:, sls[h]].astype(BF16) for s, h in pairs}
    kf = {(s, h): k[s][:, sls[h]] for s, h in pairs}
    vext = {(s, h): jnp.concatenate([v_ref[s, :, sls[h]], ones_col], axis=1) for s, h in pairs}
    cn = {(s, h): cn_s[s * ML_HEADS + h] for s, h in pairs}
    qk_d = {p: lax.dot_general(qb[p], kf[p].astype(BF16), _NT, preferred_element_type=F32)
            for p in pairs}
    qc = {p: jnp.dot(qb[p], cn[p].astype(BF16), preferred_element_type=F32) for p in pairs}

    m_inter, m_new, w, decay, rt = [], [], [], [], []
    for s in seqs:
        m_st = m_s[s]
        b_last = b[s][L - 1:L, :]
        m_inter.append(b[s] + m_st)
        log_w = b_last - b[s] + gi[s]
        m_new.append(jnp.maximum(b_last + m_st, jnp.max(log_w, axis=0, keepdims=True)))
        w.append(jnp.exp(log_w - m_new[s]))
        decay.append(jnp.exp(b_last + m_st - m_new[s]))
        rt.append(jnp.transpose(gi[s] - b[s]))

    s_b, kw_b, m_ts, a_inters = {}, {}, {}, {}
    for s, h in pairs:
        log_d = jnp.where(causal, b[s][:, h:h + 1] + rt[s][h:h + 1, :], NEG_INF)
        m_t = jnp.maximum(m_inter[s][:, h:h + 1], jnp.max(log_d, axis=1, keepdims=True))
        s_b[s, h] = (qk_d[s, h] * jnp.exp(log_d - m_t)).astype(BF16)
        kw_b[s, h] = (kf[s, h] * w[s][:, h:h + 1]).astype(BF16)
        m_ts[s, h] = m_t
        a_inters[s, h] = jnp.exp(m_inter[s][:, h:h + 1] - m_t)

    sv = {p: jnp.dot(s_b[p], vext[p], preferred_element_type=F32) for p in pairs}
    upd = {p: lax.dot_general(kw_b[p], vext[p], _TN, preferred_element_type=F32) for p in pairs}

    for s, h in pairs:
        nd = sv[s, h] + a_inters[s, h] * qc[s, h]
        num = nd[:, :HD]
        den = nd[:, HD:HD + 1]
        hh = num * (1.0 / jnp.maximum(jnp.abs(den), jnp.exp(-m_ts[s, h])))
        cn_s[s * ML_HEADS + h] = decay[s][:, h:h + 1] * cn[s, h] + upd[s, h]
        hm = jax.nn.sigmoid(o_ref[s, :, sls[h]]) * hh
        out_ref[s, :, sls[h]] = _rms(hm, gn_ref[:, sls[h]]).astype(BF16)

    for s in seqs:
        m_s[s] = m_new[s]


def _mlstm(qk, vm, om, gi, gf, cw, cb, bi, bf, gn, bsz, seq):
    L = ML_CHUNK
    nc = seq // L
    nseq = ML_NB
    t = bsz * seq
    seq3 = lambda a: a.reshape(bsz, seq, a.shape[-1])
    row = lambda b, c: (b, c, 0)
    const = lambda b, c: (0, 0)
    out = pl.pallas_call(
        _mlstm_kernel,
        grid=(bsz // nseq, nc),
        in_specs=[pl.BlockSpec((nseq, L, 2 * ML_W), row),
                  pl.BlockSpec((nseq, L, ML_W), row),
                  pl.BlockSpec((nseq, L, ML_W), row),
                  pl.BlockSpec((nseq, L, LANES), row),
                  pl.BlockSpec((nseq, L, LANES), row),
                  pl.BlockSpec((CONV_WIDTH, 2 * ML_W), const),
                  pl.BlockSpec((1, 2 * ML_W), const),
                  pl.BlockSpec((1, LANES), const),
                  pl.BlockSpec((1, LANES), const),
                  pl.BlockSpec((1, ML_W), const)],
        out_specs=pl.BlockSpec((nseq, L, ML_W), row),
        out_shape=jax.ShapeDtypeStruct((bsz, seq, ML_W), BF16),
        scratch_shapes=[pltpu.VMEM((nseq, L + 8, 2 * ML_W), F32),
                        pltpu.VMEM((nseq * ML_HEADS, ML_HEAD_DIM, 2 * ML_HEAD_DIM), F32),
                        pltpu.VMEM((nseq, 1, LANES), F32)],
        compiler_params=_cparams(2),
        name="mlstm",
    )(seq3(qk), seq3(vm), seq3(om), seq3(gi), seq3(gf), cw, cb, bi, bf, gn)
    return out.reshape(t, ML_W)


def _kv_kernel(mem_ref, g_ref, w_ref, k_ref, v_ref):
    mn = _rms(mem_ref[...], g_ref[...]).astype(BF16)
    hw = k_ref.shape[1]
    k_ref[...] = jnp.dot(mn, w_ref[:, :hw], preferred_element_type=F32).astype(BF16)
    v_ref[...] = jnp.dot(mn, w_ref[:, hw:], preferred_element_type=F32).astype(BF16)


def _kv_prep(mem2d, g, w_ckv):
    r, d = mem2d.shape
    hw = w_ckv.shape[1] // 2
    tm = 256
    return pl.pallas_call(
        _kv_kernel,
        grid=(r // tm,),
        in_specs=[pl.BlockSpec((tm, d), lambda i: (i, 0)),
                  pl.BlockSpec((1, d), lambda i: (0, 0)),
                  pl.BlockSpec(w_ckv.shape, lambda i: (0, 0))],
        out_specs=[pl.BlockSpec((tm, hw), lambda i: (i, 0))] * 2,
        out_shape=[jax.ShapeDtypeStruct((r, hw), BF16)] * 2,
        compiler_params=_cparams(1),
        name="kv_prep",
    )(mem2d, g, w_ckv)


_R_E0, _R_E1, _R_RANK0, _R_RANK1, _R_P0, _R_P1 = range(6)
_ROUTER_LANE0 = N_GROUPS


_PACK_ROWS = 8


def _store_slabs(ref, v, rows, stage, first=0):
    base = first * _PACK_ROWS
    for r in range(_PACK_ROWS):
        stage[pl.ds(base + r, rows, stride=_PACK_ROWS), :] = v[:, LANES * r:LANES * (r + 1)]
    span = pl.ds(base, rows * _PACK_ROWS)
    ref[span, :] = stage[span, :].astype(ref.dtype)


def _load_slabs(ref, rows, stage):
    stage[...] = ref[...].astype(F32)
    return jnp.concatenate(
        [stage[pl.ds(r, rows, stride=_PACK_ROWS), :] for r in range(_PACK_ROWS)], axis=1)


def _mid_kernel(x_ref, att_ref, hm_ref, wo_ref, gc_ref, wq_ref, k_ref, v_ref, wco_ref, gf_ref,
                wr_ref, br_ref, x2_ref, xn_ref, route_ref, idx_ref, cnt_ref, carry_s, stage_s):
    i = pl.program_id(0)
    tm = x_ref.shape[0] // MID_PARTS
    parts = range(MID_PARTS)
    rows = [pl.ds(p * tm, tm) for p in parts]

    @pl.when(i == 0)
    def _():
        carry_s[...] = jnp.zeros(carry_s.shape, F32)

    x1 = [x_ref[rows[p], :]
          + jnp.dot(att_ref[rows[p], :], wo_ref[0:ATT_Q, :], preferred_element_type=F32)
          + jnp.dot(hm_ref[rows[p], :], wo_ref[ATT_Q:, :], preferred_element_type=F32)
          for p in parts]

    xc = [_rms(x1[p], gc_ref[...]).astype(BF16) for p in parts]
    qb = [jnp.dot(xc[p], wq_ref[...], preferred_element_type=F32).astype(BF16) for p in parts]
    sls = [slice(X_HEAD_DIM * h, X_HEAD_DIM * (h + 1)) for h in range(X_HEADS)]
    sc = [[lax.dot_general(qb[p][:, sl], k_ref[:, sl], _NT, preferred_element_type=F32)
           for sl in sls] for p in parts]
    es = [[None] * X_HEADS for _ in parts]
    rinv = [[None] * X_HEADS for _ in parts]
    for p in parts:
        for h in range(X_HEADS):
            s = sc[p][h] * (1.0 / math.sqrt(X_HEAD_DIM))
            e = jnp.exp(s - jnp.max(s, axis=-1, keepdims=True))
            es[p][h] = e.astype(BF16)
            rinv[p][h] = 1.0 / jnp.sum(e, axis=-1, keepdims=True)
    o = [jnp.concatenate(
        [jnp.dot(es[p][h], v_ref[:, sls[h]], preferred_element_type=F32) * rinv[p][h]
         for h in range(X_HEADS)], axis=1).astype(BF16) for p in parts]
    x2 = [x1[p] + jnp.dot(o[p], wco_ref[...], preferred_element_type=F32) for p in parts]

    logits_p = []
    for p in parts:
        x2_ref[rows[p], :] = x2[p]
        xn = _rms(x2[p], gf_ref[...])
        xh = xn.astype(BF16)
        _store_slabs(xn_ref, xn, tm, stage_s, first=p * tm)
        xl = (xn - xh.astype(F32)).astype(BF16)
        lg2 = jnp.dot(xh, wr_ref[...], preferred_element_type=F32)
        logits_p.append(lg2[:, :LANES] + lg2[:, LANES:]
                        + jnp.dot(xl, wr_ref[:, :LANES], preferred_element_type=F32) + br_ref[...])

    for p in parts:
        _route_part(logits_p[p], rows[p], p * tm, route_ref, idx_ref, cnt_ref, carry_s)


def _route_part(logits, rows, first, route_ref, idx_ref, cnt_ref, carry_s):
    tm = logits.shape[0]
    lane = lax.broadcasted_iota(jnp.int32, (tm, LANES), 1)
    lanef = lane.astype(F32)
    big = float(4 * LANES)
    gl = jnp.where(lane < N_GROUPS, logits, NEG_INF)
    gmax = jnp.max(gl, axis=-1, keepdims=True)
    gsel = jnp.min(jnp.where(gl == gmax, lanef, big), axis=-1, keepdims=True)
    gw = 1.0 / jnp.sum(jnp.exp(gl - gmax), axis=-1, keepdims=True)
    lo_lane = _ROUTER_LANE0 + EXPERTS_PER_GROUP * gsel
    in_group = (lanef >= lo_lane) & (lanef < lo_lane + EXPERTS_PER_GROUP)
    el = jnp.where(in_group, logits, NEG_INF)
    v0 = jnp.max(el, axis=-1, keepdims=True)
    i0 = jnp.min(jnp.where(el == v0, lanef, big), axis=-1, keepdims=True)
    el2 = jnp.where(lanef == i0, NEG_INF, el)
    v1 = jnp.max(el2, axis=-1, keepdims=True)
    i1 = jnp.min(jnp.where(el2 == v1, lanef, big), axis=-1, keepdims=True)
    tt = jnp.exp(v1 - v0)
    p0 = gw / (1.0 + tt)
    p1 = gw * tt / (1.0 + tt)

    sel0 = lanef == i0
    sel1 = lanef == i1
    mb = jnp.where(sel0 | sel1, 1.0, 0.0).astype(BF16)
    ti = lax.broadcasted_iota(jnp.int32, (tm, tm), 0)
    si = lax.broadcasted_iota(jnp.int32, (tm, tm), 1)
    strict = jnp.where(si < ti, 1.0, 0.0).astype(BF16)
    carry = carry_s[...]
    pref = jnp.dot(strict, mb, preferred_element_type=F32) + carry
    r0 = jnp.sum(jnp.where(sel0, pref, 0.0), axis=-1, keepdims=True)
    r1 = jnp.sum(jnp.where(sel1, pref, 0.0), axis=-1, keepdims=True)
    carry = carry + jnp.sum(mb.astype(F32), axis=0, keepdims=True)
    carry_s[...] = carry
    cnt_ref[...] = carry

    route = jnp.zeros((tm, LANES), F32)
    for idx, col in ((_R_E0, i0 - _ROUTER_LANE0), (_R_E1, i1 - _ROUTER_LANE0), (_R_RANK0, r0),
                     (_R_RANK1, r1), (_R_P0, p0), (_R_P1, p1)):
        route = jnp.where(lane == idx, col, route)
    route_ref[rows, :] = route
    idx_ref[:, pl.ds(first, tm)] = jnp.transpose(route)[0:8, :].astype(jnp.int32)


def _mid(x2d, att, hm, wo, gc, wq, kmem, vmem_, wco, gf, wr, br, seq):
    t, d = x2d.shape
    tm = TM_MID
    per_b = seq // tm
    rowmap = lambda i: (i, 0)
    const = lambda i: (0, 0)
    bmap = lambda i: (i // per_b, 0)
    mem_len = kmem.shape[0] // (t // seq)
    return pl.pallas_call(
        _mid_kernel,
        grid=(t // tm,),
        in_specs=[pl.BlockSpec((tm, d), rowmap),
                  pl.BlockSpec((tm, ATT_Q), rowmap),
                  pl.BlockSpec((tm, ML_W), rowmap),
                  pl.BlockSpec(wo.shape, const),
                  pl.BlockSpec((1, d), const),
                  pl.BlockSpec(wq.shape, const),
                  pl.BlockSpec((mem_len, kmem.shape[1]), bmap),
                  pl.BlockSpec((mem_len, vmem_.shape[1]), bmap),
                  pl.BlockSpec(wco.shape, const),
                  pl.BlockSpec((1, d), const),
                  pl.BlockSpec(wr.shape, const),
                  pl.BlockSpec((1, LANES), const)],
        out_specs=[pl.BlockSpec((tm, d), rowmap),
                   pl.BlockSpec((tm * _PACK_ROWS, LANES), rowmap),
                   pl.BlockSpec((tm, LANES), rowmap),
                   pl.BlockSpec((8, tm), lambda i: (0, i)),
                   pl.BlockSpec((1, LANES), const)],
        out_shape=[jax.ShapeDtypeStruct((t, d), F32),
                   jax.ShapeDtypeStruct((t * _PACK_ROWS, LANES), BF16),
                   jax.ShapeDtypeStruct((t, LANES), F32),
                   jax.ShapeDtypeStruct((8, t), jnp.int32),
                   jax.ShapeDtypeStruct((1, LANES), F32)],
        scratch_shapes=[pltpu.VMEM((1, LANES), F32),
                        pltpu.VMEM((tm * _PACK_ROWS, LANES), F32)],
        compiler_params=_cparams(1),
        name="mid",
    )(x2d, att, hm, wo, gc, wq, kmem, vmem_, wco, gf, wr, br)


_M_BLOCK_E, _M_PAD_END, _M_NUSED = range(3)
_META_LANES = 2 * LANES


def _plan_kernel(cnt_ref, idx_ref, dest_ref, meta_ref):
    cnt = cnt_ref[...]
    lane = lax.broadcasted_iota(jnp.int32, (1, LANES), 1)
    is_expert = (lane >= _ROUTER_LANE0) & (lane < _ROUTER_LANE0 + N_EXPERTS)
    nblk = jnp.where(is_expert, jnp.floor((cnt + (BM - 1)) * (1.0 / BM)), 0.0)
    jj = lax.broadcasted_iota(jnp.int32, (LANES, LANES), 0)
    kk = lax.broadcasted_iota(jnp.int32, (LANES, LANES), 1)
    upper = jnp.where(jj <= kk, 1.0, 0.0).astype(BF16)
    pend_blk = jnp.dot(jnp.broadcast_to(nblk, (8, LANES)).astype(BF16), upper,
                       preferred_element_type=F32)[0:1, :]
    pstart_rows = (pend_blk - nblk) * BM
    pend_rows = pend_blk * BM

    idx = idx_ref[...]
    off = jnp.zeros(idx.shape, F32)
    for e in range(N_EXPERTS):
        lane_e = _ROUTER_LANE0 + e
        off = jnp.where(idx == e, pstart_rows[:, lane_e:lane_e + 1], off)
    ranks = pltpu.roll(idx, idx.shape[0] - 2, axis=0)
    dest_ref[...] = ranks + off.astype(jnp.int32)

    blk = lax.broadcasted_iota(jnp.int32, (1, _META_LANES), 1).astype(F32)
    block_e = jnp.zeros((1, _META_LANES), F32)
    for e in range(N_EXPERTS):
        lane_e = _ROUTER_LANE0 + e
        block_e = block_e + jnp.where(pend_blk[:, lane_e:lane_e + 1] <= blk, 1.0, 0.0)
    block_e = jnp.minimum(block_e, N_EXPERTS - 1.0)
    last = _ROUTER_LANE0 + N_EXPERTS - 1
    nused = pend_blk[:, last:last + 1]
    pend_wide = jnp.concatenate([pend_rows, jnp.zeros((1, _META_LANES - LANES), F32)], axis=1)
    sub = lax.broadcasted_iota(jnp.int32, (8, _META_LANES), 0)
    meta = jnp.where(sub == _M_BLOCK_E, block_e, jnp.where(sub == _M_PAD_END, pend_wide, nused))
    meta_ref[...] = meta.astype(jnp.int32)


def _plan(cnt, idx):
    t = idx.shape[1]
    return pl.pallas_call(
        _plan_kernel,
        grid=(1,),
        in_specs=[pl.BlockSpec(cnt.shape, lambda i: (0, 0)),
                  pl.BlockSpec(idx.shape, lambda i: (0, 0))],
        out_specs=[pl.BlockSpec((8, t), lambda i: (0, 0)),
                   pl.BlockSpec((8, _META_LANES), lambda i: (0, 0))],
        out_shape=[jax.ShapeDtypeStruct((8, t), jnp.int32),
                   jax.ShapeDtypeStruct((8, _META_LANES), jnp.int32)],
        compiler_params=_cparams(1),
        name="plan",
    )(cnt, idx)


_SLAB = BM * _PACK_ROWS


def _dispatch_kernel(meta_ref, d_ref, xn_ref, xpad_ref, zbuf, sem, zsem):
    i = pl.program_id(0)
    td = xn_ref.shape[0] // _PACK_ROWS

    def zero_copy(e):
        pend = meta_ref[_M_PAD_END, _ROUTER_LANE0 + e]
        pstart = meta_ref[_M_PAD_END, _ROUTER_LANE0 + e - 1]
        first = pl.multiple_of((pend - BM) * _PACK_ROWS, _SLAB)
        cp = pltpu.make_async_copy(zbuf, xpad_ref.at[pl.ds(first, _SLAB), :], zsem)
        return pend > pstart, cp

    @pl.when(i == 0)
    def _():
        zbuf[...] = jnp.zeros(zbuf.shape, zbuf.dtype)
        for e in range(N_EXPERTS):
            nonempty, cp = zero_copy(e)
            pl.when(nonempty)(cp.start)
        for e in range(N_EXPERTS):
            nonempty, cp = zero_copy(e)
            pl.when(nonempty)(cp.wait)

        def tail_copy(b):
            return pltpu.make_async_copy(
                zbuf, xpad_ref.at[pl.ds(pl.multiple_of(b * _SLAB, _SLAB), _SLAB), :], zsem)

        first_unused = meta_ref[_M_NUSED, 0]
        n_blocks = xpad_ref.shape[0] // _SLAB
        lax.fori_loop(first_unused, n_blocks, lambda b, c: (tail_copy(b).start(), c)[1], 0)
        lax.fori_loop(first_unused, n_blocks, lambda b, c: (tail_copy(b).wait(), c)[1], 0)

    def row_copy(t, dst):
        src = pl.ds(pl.multiple_of(t * _PACK_ROWS, _PACK_ROWS), _PACK_ROWS)
        dsl = pl.ds(pl.multiple_of(dst * _PACK_ROWS, _PACK_ROWS), _PACK_ROWS)
        return pltpu.make_async_copy(xn_ref.at[src, :], xpad_ref.at[dsl, :], sem)

    def start(t, carry):
        row_copy(t, d_ref[0, t]).start(priority=0)
        row_copy(t, d_ref[1, t]).start(priority=1)
        return carry

    def wait(t, carry):
        row_copy(0, 0).wait()
        row_copy(0, 0).wait()
        return carry

    lax.fori_loop(0, td, start, 0, unroll=8)
    lax.fori_loop(0, td, wait, 0, unroll=8)


def _dispatch(meta, dest, xn_packed, rows):
    t = dest.shape[1]
    td = TD
    return pl.pallas_call(
        _dispatch_kernel,
        grid_spec=pltpu.PrefetchScalarGridSpec(
            num_scalar_prefetch=1,
            grid=(t // td,),
            in_specs=[pl.BlockSpec((8, td), lambda i, m: (0, i), memory_space=pltpu.SMEM),
                      pl.BlockSpec((td * _PACK_ROWS, LANES), lambda i, m: (i, 0))],
            out_specs=pl.BlockSpec(memory_space=pl.ANY),
            scratch_shapes=[pltpu.VMEM((_SLAB, LANES), BF16),
                            pltpu.SemaphoreType.DMA(()),
                            pltpu.SemaphoreType.DMA(())]),
        out_shape=jax.ShapeDtypeStruct((rows * _PACK_ROWS, LANES), BF16),
        compiler_params=_cparams(1),
        name="dispatch",
    )(meta, dest, xn_packed)


def _expert_kernel(meta_ref, x_ref, wg_ref, wu_ref, wd_ref, y_ref, wg_s, wu_s, wd_s, xstage_s,
                   ystage_s):
    i = pl.program_id(0)
    nused = meta_ref[_M_NUSED, 0]

    @pl.when(i < nused)
    def _():
        prev = meta_ref[_M_BLOCK_E, jnp.maximum(i - 1, 0)]
        changed = (i == 0) | (meta_ref[_M_BLOCK_E, i] != prev)

        @pl.when(changed)
        def _():
            wg_s[...] = wg_ref[...].astype(BF16)
            wu_s[...] = wu_ref[...].astype(BF16)
            wd_s[...] = wd_ref[...].astype(BF16)

        xb = _load_slabs(x_ref, BM, xstage_s).astype(BF16)
        g = jnp.dot(xb, wg_s[...], preferred_element_type=F32)
        u = jnp.dot(xb, wu_s[...], preferred_element_type=F32)
        hb = (g * jax.nn.sigmoid(g) * u).astype(BF16)
        y = jnp.dot(hb, wd_s[...], preferred_element_type=F32)
        _store_slabs(y_ref, y, BM, ystage_s)

    @pl.when(i >= nused)
    def _():
        y_ref[...] = jnp.zeros(y_ref.shape, y_ref.dtype)


def _experts(meta, xpad, wg, wu, wd):
    nbk = xpad.shape[0] // _SLAB
    d, de = wg.shape[1], wg.shape[2]
    last_used = lambda i, m: jnp.minimum(i, m[_M_NUSED, 0] - 1)
    blk = lambda i, m: (last_used(i, m), 0)
    oblk = lambda i, m: (i, 0)
    wmap = lambda i, m: (m[_M_BLOCK_E, last_used(i, m)], 0, 0)
    return pl.pallas_call(
        _expert_kernel,
        grid_spec=pltpu.PrefetchScalarGridSpec(
            num_scalar_prefetch=1,
            grid=(nbk,),
            in_specs=[pl.BlockSpec((_SLAB, LANES), blk),
                      pl.BlockSpec((None, d, de), wmap),
                      pl.BlockSpec((None, d, de), wmap),
                      pl.BlockSpec((None, de, d), wmap)],
            out_specs=pl.BlockSpec((_SLAB, LANES), oblk),
            scratch_shapes=[pltpu.VMEM((d, de), BF16),
                            pltpu.VMEM((d, de), BF16),
                            pltpu.VMEM((de, d), BF16),
                            pltpu.VMEM((_SLAB, LANES), F32),
                            pltpu.VMEM((_SLAB, LANES), F32)]),
        out_shape=jax.ShapeDtypeStruct(xpad.shape, BF16),
        compiler_params=_cparams(1),
        name="experts",
    )(meta, xpad, wg, wu, wd)


def _combine_kernel(dc_ref, dn_ref, x2_ref, route_ref, g_ref, ypad_ref, out_ref, ybuf, sem,
                    stage0_s, stage1_s):
    i = pl.program_id(0)
    n = pl.num_programs(0)
    tc = x2_ref.shape[0]

    def row_copy(src, slot, which, t):
        ssl = pl.ds(pl.multiple_of(src * _PACK_ROWS, _PACK_ROWS), _PACK_ROWS)
        dsl = pl.ds(pl.multiple_of(t * _PACK_ROWS, _PACK_ROWS), _PACK_ROWS)
        return pltpu.make_async_copy(ypad_ref.at[ssl, :], ybuf.at[slot, which, dsl, :],
                                     sem.at[slot])

    def issue(d_ref, slot):
        def body(t, carry):
            row_copy(d_ref[0, t], slot, 0, t).start(priority=0)
            row_copy(d_ref[1, t], slot, 1, t).start(priority=1)
            return carry
        lax.fori_loop(0, tc, body, 0, unroll=8)

    slot = i % 2

    @pl.when(i == 0)
    def _():
        issue(dc_ref, 0)

    @pl.when(i + 1 < n)
    def _():
        issue(dn_ref, 1 - slot)

    def wait(t, carry):
        row_copy(0, slot, 0, 0).wait()
        row_copy(0, slot, 1, 0).wait()
        return carry
    lax.fori_loop(0, tc, wait, 0, unroll=8)

    route = route_ref[...]
    p0 = route[:, _R_P0:_R_P0 + 1]
    p1 = route[:, _R_P1:_R_P1 + 1]
    y0 = _load_slabs(ybuf.at[slot, 0], tc, stage0_s)
    y1 = _load_slabs(ybuf.at[slot, 1], tc, stage1_s)
    x3 = x2_ref[...] + p0 * y0 + p1 * y1
    out_ref[...] = _rms(x3, g_ref[...])


def _combine(dest, x2, route, g, ypad):
    t, d = x2.shape
    tc = TC
    nt = t // tc
    cur = pl.BlockSpec((8, tc), lambda i: (0, i), memory_space=pltpu.SMEM)
    nxt = pl.BlockSpec((8, tc), lambda i: (0, jnp.minimum(i + 1, nt - 1)),
                       memory_space=pltpu.SMEM)
    return pl.pallas_call(
        _combine_kernel,
        grid=(nt,),
        in_specs=[cur, nxt,
                  pl.BlockSpec((tc, d), lambda i: (i, 0)),
                  pl.BlockSpec((tc, LANES), lambda i: (i, 0)),
                  pl.BlockSpec((1, d), lambda i: (0, 0)),
                  pl.BlockSpec(memory_space=pl.ANY)],
        out_specs=pl.BlockSpec((tc, d), lambda i: (i, 0)),
        out_shape=jax.ShapeDtypeStruct((t, d), F32),
        scratch_shapes=[pltpu.VMEM((2, 2, tc * _PACK_ROWS, LANES), BF16),
                        pltpu.SemaphoreType.DMA((2,)),
                        pltpu.VMEM((tc * _PACK_ROWS, LANES), F32),
                        pltpu.VMEM((tc * _PACK_ROWS, LANES), F32)],
        compiler_params=_cparams(1),
        name="combine",
    )(dest, dest, x2, route, g, ypad)


def _pad_cols(a, width):
    return jnp.pad(a, ((0, 0), (0, width - a.shape[1])))


def kernel(x, mem, norm_mix, w_in, b_gates, conv_w, conv_b, att_sinks, norm_att_out, norm_ml_out,
           w_out, norm_cross, norm_mem, w_cq, w_ckv, w_co, norm_ffn, w_router_group,
           b_router_group, w_router_expert, b_router_expert, w_e_gate, w_e_up, w_e_down,
           norm_final):
    bsz, seq, d = x.shape
    t = bsz * seq
    depth = w_in.shape[0]
    assert depth == 1, "the final RMSNorm is fused into the last layer's combine kernel"
    xs = x.reshape(t, d)
    mem2d = mem.reshape(-1, d)

    for l in range(depth):
        n_main = _SEG["om"][1]
        wl = w_in[l]
        w_packed = jnp.concatenate(
            [wl[:, :n_main], _pad_cols(wl[:, n_main:n_main + ML_HEADS], LANES),
             _pad_cols(wl[:, n_main + ML_HEADS:], LANES)], axis=1).astype(BF16)
        bi = _pad_cols(b_gates[l][None, :ML_HEADS], LANES)
        bf = _pad_cols(b_gates[l][None, ML_HEADS:], LANES)
        w_r = _pad_cols(jnp.concatenate([w_router_group[l], w_router_expert[l]], axis=1), LANES)
        w_r_hi = w_r.astype(BF16)
        w_r_lo = (w_r - w_r_hi.astype(F32)).astype(BF16)
        w_r_packed = jnp.concatenate([w_r_hi, w_r_lo], axis=1)
        b_r = _pad_cols(jnp.concatenate([b_router_group[l], b_router_expert[l]])[None, :], LANES)

        qa, ka, va, qk, vm, om, gi, gf = _in_proj(xs, norm_mix[l][None, :], w_packed)
        att = _swa(qa, ka, va, att_sinks[l], norm_att_out[l][None, :], bsz, seq)
        hm = _mlstm(qk, vm, om, gi, gf, conv_w[l], conv_b[l][None, :], bi, bf,
                    norm_ml_out[l][None, :], bsz, seq)

        kmem, vmem_ = _kv_prep(mem2d, norm_mem[l][None, :], w_ckv[l].astype(BF16))
        x2, xn, route, idx, cnt = _mid(xs, att, hm, w_out[l].astype(BF16), norm_cross[l][None, :],
                                       w_cq[l].astype(BF16), kmem, vmem_, w_co[l].astype(BF16),
                                       norm_ffn[l][None, :], w_r_packed, b_r, seq)

        nbk = (t * TOP_K) // BM + N_EXPERTS
        assert nbk <= _META_LANES
        dest, meta = _plan(cnt, idx)
        xpad = _dispatch(meta, dest, xn, nbk * BM)
        ypad = _experts(meta, xpad, w_e_gate[l], w_e_up[l], w_e_down[l])
        xs = _combine(dest, x2, route, norm_final[None, :], ypad)
    return xs.reshape(bsz, seq, d)
```

```python
import functools
import math

import jax
import jax.numpy as jnp
from jax import lax
from jax.experimental import pallas as pl
from jax.experimental.pallas import tpu as pltpu

F32 = jnp.float32
BF16 = jnp.bfloat16
EPS = 1e-6
NEG_INF = float("-inf")

ATT_HEADS = 8
ATT_HEAD_DIM = 64
ATT_Q = ATT_HEADS * ATT_HEAD_DIM
ATT_KV = 128
WINDOW = 128
ML_HEADS = 4
ML_HEAD_DIM = 128
ML_W = ML_HEADS * ML_HEAD_DIM
CONV_WIDTH = 4
X_HEADS = 4
X_HEAD_DIM = 128
N_GROUPS = 4
EXPERTS_PER_GROUP = 8
N_EXPERTS = N_GROUPS * EXPERTS_PER_GROUP
TOP_K = 2

LANES = 128
TM_IN = 512
IN_PARTS = 2
SWA_QB = 2
ML_CHUNK = 128
ML_NB = 1
TM_MID = 512
MID_PARTS = 2
TD = 512
TC = 256
BM = 256
VMEM_LIMIT = 56 * 1024 * 1024

_NT = (((1,), (1,)), ((), ()))
_TN = (((0,), (0,)), ((), ()))


def _rms(x, g):
    return x * lax.rsqrt(jnp.mean(x * x, axis=-1, keepdims=True) + EPS) * g


def _cparams(n_axes):
    return pltpu.CompilerParams(dimension_semantics=("arbitrary",) * n_axes,
                                vmem_limit_bytes=VMEM_LIMIT)


_SEG = {"qa": (0, 512), "ka": (512, 640), "va": (640, 768), "qk": (768, 1792),
        "vm": (1792, 2304), "om": (2304, 2816), "gi": (2816, 2944), "gf": (2944, 3072)}


def _in_proj_kernel(x_ref, g_ref, w_ref, qa_ref, ka_ref, va_ref, qk_ref, vm_ref, om_ref,
                    gi_ref, gf_ref):
    tm = x_ref.shape[0] // IN_PARTS
    rows = [pl.ds(p * tm, tm) for p in range(IN_PARTS)]
    hb = [_rms(x_ref[r, :], g_ref[...]).astype(BF16) for r in rows]
    outs = {"qa": qa_ref, "ka": ka_ref, "va": va_ref, "qk": qk_ref, "vm": vm_ref, "om": om_ref,
            "gi": gi_ref, "gf": gf_ref}
    for p, r in enumerate(rows):
        for name, ref in outs.items():
            lo, hi = _SEG[name]
            ref[r, :] = jnp.dot(hb[p], w_ref[:, lo:hi], preferred_element_type=F32).astype(ref.dtype)


def _in_proj(x2d, g, w_packed):
    t, d = x2d.shape
    tm = TM_IN
    widths = [(512, BF16), (128, BF16), (128, BF16), (1024, F32), (512, BF16), (512, F32),
              (128, F32), (128, F32)]
    return pl.pallas_call(
        _in_proj_kernel,
        grid=(t // tm,),
        in_specs=[pl.BlockSpec((tm, d), lambda i: (i, 0)),
                  pl.BlockSpec((1, d), lambda i: (0, 0)),
                  pl.BlockSpec(w_packed.shape, lambda i: (0, 0))],
        out_specs=[pl.BlockSpec((tm, w), lambda i: (i, 0)) for w, _ in widths],
        out_shape=[jax.ShapeDtypeStruct((t, w), dt) for w, dt in widths],
        compiler_params=_cparams(1),
        name="in_proj",
    )(x2d, g, w_packed)


def _swa_kernel(sink_ref, q_ref, kp_ref, kc_ref, vp_ref, vc_ref, gn_ref, o_ref, bias_s):
    b = pl.program_id(0)
    n = pl.program_id(1)

    @pl.when((b == 0) & (n == 0))
    def _():
        ti = lax.broadcasted_iota(jnp.int32, (WINDOW, 2 * WINDOW), 0)
        si = lax.broadcasted_iota(jnp.int32, (WINDOW, 2 * WINDOW), 1)
        dist = ti + WINDOW - si
        band = (dist >= 0) & (dist < WINDOW)
        distf = dist.astype(F32)
        for hidx in range(ATT_HEADS):
            slope = 2.0 ** (-8.0 * (hidx + 1) / ATT_HEADS)
            full = jnp.where(band, -slope * distf, NEG_INF)
            bias_s[1, hidx] = full
            bias_s[0, hidx] = jnp.where(si >= WINDOW, full, NEG_INF)

    scale = 1.0 / math.sqrt(ATT_HEAD_DIM)
    k_all = jnp.concatenate([kp_ref[...], kc_ref[...]], axis=0).astype(F32) * scale
    v_all = jnp.concatenate([vp_ref[...], vc_ref[...]], axis=0).astype(F32)
    lane = lax.broadcasted_iota(jnp.int32, k_all.shape, 1)
    low = lane < ATT_HEAD_DIM

    def lane_variants(a):
        ar = pltpu.roll(a, ATT_HEAD_DIM, axis=1)
        z = jnp.zeros_like(a)
        return ((jnp.where(low, a, z).astype(BF16), jnp.where(low, z, ar).astype(BF16)),
                (jnp.where(low, ar, z).astype(BF16), jnp.where(low, z, a).astype(BF16)))

    k_ops = lane_variants(k_all)
    v_ops = lane_variants(v_all)
    n_tiles = ATT_Q // LANES
    kv_of = lambda j: (2 * j) // (ATT_HEADS // 2)
    blocks = range(SWA_QB)
    rows = [slice(WINDOW * i, WINDOW * (i + 1)) for i in blocks]
    keys = [slice(WINDOW * i, WINDOW * (i + 2)) for i in blocks]
    bias_slot = [jnp.where(n == 0, 0, 1) if i == 0 else 1 for i in blocks]

    scores = [[[lax.dot_general(q_ref[rows[i], LANES * j:LANES * (j + 1)],
                                k_ops[kv_of(j)][par][keys[i]], _NT, preferred_element_type=F32)
                for par in range(2)] for j in range(n_tiles)] for i in blocks]
    probs = [[[None, None] for _ in range(n_tiles)] for _ in blocks]
    rinv = [[[None, None] for _ in range(n_tiles)] for _ in blocks]
    for i in blocks:
        for j in range(n_tiles):
            for par in range(2):
                hidx = 2 * j + par
                logits = scores[i][j][par] + bias_s[bias_slot[i], hidx]
                sink = sink_ref[hidx]
                mx = jnp.maximum(jnp.max(logits, axis=-1, keepdims=True), sink)
                e = jnp.exp(logits - mx)
                den = jnp.sum(e, axis=-1, keepdims=True) + jnp.exp(sink - mx)
                probs[i][j][par] = e.astype(BF16)
                rinv[i][j][par] = 1.0 / den
    lane_o = lax.broadcasted_iota(jnp.int32, (WINDOW, LANES), 1)
    for i in blocks:
        tiles = []
        for j in range(n_tiles):
            acc = (jnp.dot(probs[i][j][0], v_ops[kv_of(j)][0][keys[i]], preferred_element_type=F32)
                   + jnp.dot(probs[i][j][1], v_ops[kv_of(j)][1][keys[i]],
                             preferred_element_type=F32))
            tiles.append(acc * jnp.where(lane_o < ATT_HEAD_DIM, rinv[i][j][0], rinv[i][j][1]))
        att = jnp.concatenate(tiles, axis=1)
        o_ref[rows[i], :] = _rms(att, gn_ref[...]).astype(BF16)


def _swa(qa, ka, va, sinks, gn, bsz, seq):
    nb = seq // WINDOW
    ns = nb // SWA_QB
    t = bsz * seq
    cur = lambda b, n: (b * ns + n, 0)
    prev = lambda b, n: (b * nb + jnp.maximum(n * SWA_QB - 1, 0), 0)
    return pl.pallas_call(
        _swa_kernel,
        grid=(bsz, ns),
        in_specs=[pl.BlockSpec(memory_space=pltpu.SMEM),
                  pl.BlockSpec((SWA_QB * WINDOW, ATT_Q), cur),
                  pl.BlockSpec((WINDOW, ATT_KV), prev),
                  pl.BlockSpec((SWA_QB * WINDOW, ATT_KV), cur),
                  pl.BlockSpec((WINDOW, ATT_KV), prev),
                  pl.BlockSpec((SWA_QB * WINDOW, ATT_KV), cur),
                  pl.BlockSpec((1, ATT_Q), lambda b, n: (0, 0))],
        out_specs=pl.BlockSpec((SWA_QB * WINDOW, ATT_Q), cur),
        out_shape=jax.ShapeDtypeStruct((t, ATT_Q), BF16),
        scratch_shapes=[pltpu.VMEM((2, ATT_HEADS, WINDOW, 2 * WINDOW), F32)],
        compiler_params=_cparams(2),
        name="swa",
    )(sinks, qa, ka, ka, va, va, gn)


def _split3(a):
    hi = a.astype(BF16)
    r1 = a - hi.astype(F32)
    mid = r1.astype(BF16)
    lo = (r1 - mid.astype(F32)).astype(BF16)
    return hi, mid, lo


def _mlstm_kernel(qk_ref, v_ref, o_ref, gi_ref, gf_ref, cw_ref, cb_ref, bi_ref, bf_ref, gn_ref,
                  out_ref, xbuf, cn_s, m_s):
    c = pl.program_id(1)
    nseq, L = qk_ref.shape[0], qk_ref.shape[1]
    HD = ML_HEAD_DIM
    seqs = range(nseq)
    heads = range(ML_HEADS)
    pairs = [(s, h) for s in seqs for h in heads]
    sls = [slice(HD * h, HD * (h + 1)) for h in heads]

    @pl.when(c == 0)
    def _():
        xbuf[:, 0:8, :] = jnp.zeros((nseq, 8, xbuf.shape[2]), F32)
        cn_s[...] = jnp.zeros(cn_s.shape, F32)
        m_s[...] = jnp.zeros(m_s.shape, F32)

    ti = lax.broadcasted_iota(jnp.int32, (L, L), 0)
    si = lax.broadcasted_iota(jnp.int32, (L, L), 1)
    causal = si <= ti
    tril = jnp.where(causal, 1.0, 0.0).astype(BF16)
    lane_l = lax.broadcasted_iota(jnp.int32, (L, HD), 1)
    ones_col = jnp.where(lane_l == 0, 1.0, 0.0).astype(BF16)

    gi, b = [], []
    for s in seqs:
        gi.append(gi_ref[s] + bi_ref[...])
        z = gf_ref[s] + bf_ref[...]
        lf = jnp.minimum(z, 0.0) - jnp.log1p(jnp.exp(-jnp.abs(z)))
        acc = jnp.zeros((L, LANES), F32)
        for part in _split3(lf):
            acc = acc + jnp.dot(tril, part, preferred_element_type=F32)
        b.append(acc)

    q, k = [], []
    for s in seqs:
        xbuf[s, 8:8 + L, :] = qk_ref[s]
        y = cb_ref[...] + cw_ref[3:4, :] * xbuf[s, 8:8 + L, :]
        for j in range(CONV_WIDTH - 1):
            sh = CONV_WIDTH - 1 - j
            y = y + cw_ref[j:j + 1, :] * xbuf[s, 8 - sh:8 - sh + L, :]
        xbuf[s, 0:8, :] = xbuf[s, L:L + 8, :]
        qk = y * jax.nn.sigmoid(y)
        q.append(qk[:, :ML_W])
        k.append(qk[:, ML_W:] * (1.0 / math.sqrt(HD)))

    qb = {(s, h): q[s][:, sls[h]].astype(BF16) for s, h in pairs}
    kf = {(s, h): k[s][:, sls[h]] for s, h in pairs}
    vext = {(s, h): jnp.concatenate([v_ref[s, :, sls[h]], ones_col], axis=1) for s, h in pairs}
    cn = {(s, h): cn_s[s * ML_HEADS + h] for s, h in pairs}
    qk_d = {p: lax.dot_general(qb[p], kf[p].astype(BF16), _NT, preferred_element_type=F32)
            for p in pairs}
    qc = {p: jnp.dot(qb[p], cn[p].astype(BF16), preferred_element_type=F32) for p in pairs}

    m_inter, m_new, w, decay, rt = [], [], [], [], []
    for s in seqs:
        m_st = m_s[s]
        b_last = b[s][L - 1:L, :]
        m_inter.append(b[s] + m_st)
        log_w = b_last - b[s] + gi[s]
        m_new.append(jnp.maximum(b_last + m_st, jnp.max(log_w, axis=0, keepdims=True)))
        w.append(jnp.exp(log_w - m_new[s]))
        decay.append(jnp.exp(b_last + m_st - m_new[s]))
        rt.append(jnp.transpose(gi[s] - b[s]))

    s_b, kw_b, m_ts, a_inters = {}, {}, {}, {}
    for s, h in pairs:
        log_d = jnp.where(causal, b[s][:, h:h + 1] + rt[s][h:h + 1, :], NEG_INF)
        m_t = jnp.maximum(m_inter[s][:, h:h + 1], jnp.max(log_d, axis=1, keepdims=True))
        s_b[s, h] = (qk_d[s, h] * jnp.exp(log_d - m_t)).astype(BF16)
        kw_b[s, h] = (kf[s, h] * w[s][:, h:h + 1]).astype(BF16)
        m_ts[s, h] = m_t
        a_inters[s, h] = jnp.exp(m_inter[s][:, h:h + 1] - m_t)

    sv = {p: jnp.dot(s_b[p], vext[p], preferred_element_type=F32) for p in pairs}
    upd = {p: lax.dot_general(kw_b[p], vext[p], _TN, preferred_element_type=F32) for p in pairs}

    for s, h in pairs:
        nd = sv[s, h] + a_inters[s, h] * qc[s, h]
        num = nd[:, :HD]
        den = nd[:, HD:HD + 1]
        hh = num * (1.0 / jnp.maximum(jnp.abs(den), jnp.exp(-m_ts[s, h])))
        cn_s[s * ML_HEADS + h] = decay[s][:, h:h + 1] * cn[s, h] + upd[s, h]
        hm = jax.nn.sigmoid(o_ref[s, :, sls[h]]) * hh
        out_ref[s, :, sls[h]] = _rms(hm, gn_ref[:, sls[h]]).astype(BF16)

    for s in seqs:
        m_s[s] = m_new[s]


def _mlstm(qk, vm, om, gi, gf, cw, cb, bi, bf, gn, bsz, seq):
    L = ML_CHUNK
    nc = seq // L
    nseq = ML_NB
    t = bsz * seq
    seq3 = lambda a: a.reshape(bsz, seq, a.shape[-1])
    row = lambda b, c: (b, c, 0)
    const = lambda b, c: (0, 0)
    out = pl.pallas_call(
        _mlstm_kernel,
        grid=(bsz // nseq, nc),
        in_specs=[pl.BlockSpec((nseq, L, 2 * ML_W), row),
                  pl.BlockSpec((nseq, L, ML_W), row),
                  pl.BlockSpec((nseq, L, ML_W), row),
                  pl.BlockSpec((nseq, L, LANES), row),
                  pl.BlockSpec((nseq, L, LANES), row),
                  pl.BlockSpec((CONV_WIDTH, 2 * ML_W), const),
                  pl.BlockSpec((1, 2 * ML_W), const),
                  pl.BlockSpec((1, LANES), const),
                  pl.BlockSpec((1, LANES), const),
                  pl.BlockSpec((1, ML_W), const)],
        out_specs=pl.BlockSpec((nseq, L, ML_W), row),
        out_shape=jax.ShapeDtypeStruct((bsz, seq, ML_W), BF16),
        scratch_shapes=[pltpu.VMEM((nseq, L + 8, 2 * ML_W), F32),
                        pltpu.VMEM((nseq * ML_HEADS, ML_HEAD_DIM, 2 * ML_HEAD_DIM), F32),
                        pltpu.VMEM((nseq, 1, LANES), F32)],
        compiler_params=_cparams(2),
        name="mlstm",
    )(seq3(qk), seq3(vm), seq3(om), seq3(gi), seq3(gf), cw, cb, bi, bf, gn)
    return out.reshape(t, ML_W)


def _kv_kernel(mem_ref, g_ref, w_ref, k_ref, v_ref):
    mn = _rms(mem_ref[...], g_ref[...]).astype(BF16)
    hw = k_ref.shape[1]
    k_ref[...] = jnp.dot(mn, w_ref[:, :hw], preferred_element_type=F32).astype(BF16)
    v_ref[...] = jnp.dot(mn, w_ref[:, hw:], preferred_element_type=F32).astype(BF16)


def _kv_prep(mem2d, g, w_ckv):
    r, d = mem2d.shape
    hw = w_ckv.shape[1] // 2
    tm = 256
    return pl.pallas_call(
        _kv_kernel,
        grid=(r // tm,),
        in_specs=[pl.BlockSpec((tm, d), lambda i: (i, 0)),
                  pl.BlockSpec((1, d), lambda i: (0, 0)),
                  pl.BlockSpec(w_ckv.shape, lambda i: (0, 0))],
        out_specs=[pl.BlockSpec((tm, hw), lambda i: (i, 0))] * 2,
        out_shape=[jax.ShapeDtypeStruct((r, hw), BF16)] * 2,
        compiler_params=_cparams(1),
        name="kv_prep",
    )(mem2d, g, w_ckv)


_R_E0, _R_E1, _R_RANK0, _R_RANK1, _R_P0, _R_P1 = range(6)
_ROUTER_LANE0 = N_GROUPS


_PACK_ROWS = 8


def _store_slabs(ref, v, rows, stage, first=0):
    base = first * _PACK_ROWS
    for r in range(_PACK_ROWS):
        stage[pl.ds(base + r, rows, stride=_PACK_ROWS), :] = v[:, LANES * r:LANES * (r + 1)]
    span = pl.ds(base, rows * _PACK_ROWS)
    ref[span, :] = stage[span, :].astype(ref.dtype)


def _load_slabs(ref, rows, stage):
    stage[...] = ref[...].astype(F32)
    return jnp.concatenate(
        [stage[pl.ds(r, rows, stride=_PACK_ROWS), :] for r in range(_PACK_ROWS)], axis=1)


def _mid_kernel(x_ref, att_ref, hm_ref, wo_ref, gc_ref, wq_ref, k_ref, v_ref, wco_ref, gf_ref,
                wr_ref, br_ref, x2_ref, xn_ref, route_ref, idx_ref, cnt_ref, carry_s, stage_s):
    i = pl.program_id(0)
    tm = x_ref.shape[0] // MID_PARTS
    parts = range(MID_PARTS)
    rows = [pl.ds(p * tm, tm) for p in parts]

    @pl.when(i == 0)
    def _():
        carry_s[...] = jnp.zeros(carry_s.shape, F32)

    x1 = [x_ref[rows[p], :]
          + jnp.dot(att_ref[rows[p], :], wo_ref[0:ATT_Q, :], preferred_element_type=F32)
          + jnp.dot(hm_ref[rows[p], :], wo_ref[ATT_Q:, :], preferred_element_type=F32)
          for p in parts]

    xc = [_rms(x1[p], gc_ref[...]).astype(BF16) for p in parts]
    qb = [jnp.dot(xc[p], wq_ref[...], preferred_element_type=F32).astype(BF16) for p in parts]
    sls = [slice(X_HEAD_DIM * h, X_HEAD_DIM * (h + 1)) for h in range(X_HEADS)]
    sc = [[lax.dot_general(qb[p][:, sl], k_ref[:, sl], _NT, preferred_element_type=F32)
           for sl in sls] for p in parts]
    es = [[None] * X_HEADS for _ in parts]
    rinv = [[None] * X_HEADS for _ in parts]
    for p in parts:
        for h in range(X_HEADS):
            s = sc[p][h] * (1.0 / math.sqrt(X_HEAD_DIM))
            e = jnp.exp(s - jnp.max(s, axis=-1, keepdims=True))
            es[p][h] = e.astype(BF16)
            rinv[p][h] = 1.0 / jnp.sum(e, axis=-1, keepdims=True)
    o = [jnp.concatenate(
        [jnp.dot(es[p][h], v_ref[:, sls[h]], preferred_element_type=F32) * rinv[p][h]
         for h in range(X_HEADS)], axis=1).astype(BF16) for p in parts]
    x2 = [x1[p] + jnp.dot(o[p], wco_ref[...], preferred_element_type=F32) for p in parts]

    logits_p = []
    for p in parts:
        x2_ref[rows[p], :] = x2[p]
        xn = _rms(x2[p], gf_ref[...])
        xh = xn.astype(BF16)
        _store_slabs(xn_ref, xn, tm, stage_s, first=p * tm)
        xl = (xn - xh.astype(F32)).astype(BF16)
        lg2 = jnp.dot(xh, wr_ref[...], preferred_element_type=F32)
        logits_p.append(lg2[:, :LANES] + lg2[:, LANES:]
                        + jnp.dot(xl, wr_ref[:, :LANES], preferred_element_type=F32) + br_ref[...])

    for p in parts:
        _route_part(logits_p[p], rows[p], p * tm, route_ref, idx_ref, cnt_ref, carry_s)


def _route_part(logits, rows, first, route_ref, idx_ref, cnt_ref, carry_s):
    tm = logits.shape[0]
    lane = lax.broadcasted_iota(jnp.int32, (tm, LANES), 1)
    lanef = lane.astype(F32)
    big = float(4 * LANES)
    gl = jnp.where(lane < N_GROUPS, logits, NEG_INF)
    gmax = jnp.max(gl, axis=-1, keepdims=True)
    gsel = jnp.min(jnp.where(gl == gmax, lanef, big), axis=-1, keepdims=True)
    gw = 1.0 / jnp.sum(jnp.exp(gl - gmax), axis=-1, keepdims=True)
    lo_lane = _ROUTER_LANE0 + EXPERTS_PER_GROUP * gsel
    in_group = (lanef >= lo_lane) & (lanef < lo_lane + EXPERTS_PER_GROUP)
    el = jnp.where(in_group, logits, NEG_INF)
    v0 = jnp.max(el, axis=-1, keepdims=True)
    i0 = jnp.min(jnp.where(el == v0, lanef, big), axis=-1, keepdims=True)
    el2 = jnp.where(lanef == i0, NEG_INF, el)
    v1 = jnp.max(el2, axis=-1, keepdims=True)
    i1 = jnp.min(jnp.where(el2 == v1, lanef, big), axis=-1, keepdims=True)
    tt = jnp.exp(v1 - v0)
    p0 = gw / (1.0 + tt)
    p1 = gw * tt / (1.0 + tt)

    sel0 = lanef == i0
    sel1 = lanef == i1
    mb = jnp.where(sel0 | sel1, 1.0, 0.0).astype(BF16)
    ti = lax.broadcasted_iota(jnp.int32, (tm, tm), 0)
    si = lax.broadcasted_iota(jnp.int32, (tm, tm), 1)
    strict = jnp.where(si < ti, 1.0, 0.0).astype(BF16)
    carry = carry_s[...]
    pref = jnp.dot(strict, mb, preferred_element_type=F32) + carry
    r0 = jnp.sum(jnp.where(sel0, pref, 0.0), axis=-1, keepdims=True)
    r1 = jnp.sum(jnp.where(sel1, pref, 0.0), axis=-1, keepdims=True)
    carry = carry + jnp.sum(mb.astype(F32), axis=0, keepdims=True)
    carry_s[...] = carry
    cnt_ref[...] = carry

    route = jnp.zeros((tm, LANES), F32)
    for idx, col in ((_R_E0, i0 - _ROUTER_LANE0), (_R_E1, i1 - _ROUTER_LANE0), (_R_RANK0, r0),
                     (_R_RANK1, r1), (_R_P0, p0), (_R_P1, p1)):
        route = jnp.where(lane == idx, col, route)
    route_ref[rows, :] = route
    idx_ref[:, pl.ds(first, tm)] = jnp.transpose(route)[0:8, :].astype(jnp.int32)


def _mid(x2d, att, hm, wo, gc, wq, kmem, vmem_, wco, gf, wr, br, seq):
    t, d = x2d.shape
    tm = TM_MID
    per_b = seq // tm
    rowmap = lambda i: (i, 0)
    const = lambda i: (0, 0)
    bmap = lambda i: (i // per_b, 0)
    mem_len = kmem.shape[0] // (t // seq)
    return pl.pallas_call(
        _mid_kernel,
        grid=(t // tm,),
        in_specs=[pl.BlockSpec((tm, d), rowmap),
                  pl.BlockSpec((tm, ATT_Q), rowmap),
                  pl.BlockSpec((tm, ML_W), rowmap),
                  pl.BlockSpec(wo.shape, const),
                  pl.BlockSpec((1, d), const),
                  pl.BlockSpec(wq.shape, const),
                  pl.BlockSpec((mem_len, kmem.shape[1]), bmap),
                  pl.BlockSpec((mem_len, vmem_.shape[1]), bmap),
                  pl.BlockSpec(wco.shape, const),
                  pl.BlockSpec((1, d), const),
                  pl.BlockSpec(wr.shape, const),
                  pl.BlockSpec((1, LANES), const)],
        out_specs=[pl.BlockSpec((tm, d), rowmap),
                   pl.BlockSpec((tm * _PACK_ROWS, LANES), rowmap),
                   pl.BlockSpec((tm, LANES), rowmap),
                   pl.BlockSpec((8, tm), lambda i: (0, i)),
                   pl.BlockSpec((1, LANES), const)],
        out_shape=[jax.ShapeDtypeStruct((t, d), F32),
                   jax.ShapeDtypeStruct((t * _PACK_ROWS, LANES), BF16),
                   jax.ShapeDtypeStruct((t, LANES), F32),
                   jax.ShapeDtypeStruct((8, t), jnp.int32),
                   jax.ShapeDtypeStruct((1, LANES), F32)],
        scratch_shapes=[pltpu.VMEM((1, LANES), F32),
                        pltpu.VMEM((tm * _PACK_ROWS, LANES), F32)],
        compiler_params=_cparams(1),
        name="mid",
    )(x2d, att, hm, wo, gc, wq, kmem, vmem_, wco, gf, wr, br)


_M_BLOCK_E, _M_PAD_END, _M_NUSED = range(3)
_META_LANES = 2 * LANES


def _plan_kernel(cnt_ref, idx_ref, dest_ref, meta_ref):
    cnt = cnt_ref[...]
    lane = lax.broadcasted_iota(jnp.int32, (1, LANES), 1)
    is_expert = (lane >= _ROUTER_LANE0) & (lane < _ROUTER_LANE0 + N_EXPERTS)
    nblk = jnp.where(is_expert, jnp.floor((cnt + (BM - 1)) * (1.0 / BM)), 0.0)
    jj = lax.broadcasted_iota(jnp.int32, (LANES, LANES), 0)
    kk = lax.broadcasted_iota(jnp.int32, (LANES, LANES), 1)
    upper = jnp.where(jj <= kk, 1.0, 0.0).astype(BF16)
    pend_blk = jnp.dot(jnp.broadcast_to(nblk, (8, LANES)).astype(BF16), upper,
                       preferred_element_type=F32)[0:1, :]
    pstart_rows = (pend_blk - nblk) * BM
    pend_rows = pend_blk * BM

    idx = idx_ref[...]
    off = jnp.zeros(idx.shape, F32)
    for e in range(N_EXPERTS):
        lane_e = _ROUTER_LANE0 + e
        off = jnp.where(idx == e, pstart_rows[:, lane_e:lane_e + 1], off)
    ranks = pltpu.roll(idx, idx.shape[0] - 2, axis=0)
    dest_ref[...] = ranks + off.astype(jnp.int32)

    blk = lax.broadcasted_iota(jnp.int32, (1, _META_LANES), 1).astype(F32)
    block_e = jnp.zeros((1, _META_LANES), F32)
    for e in range(N_EXPERTS):
        lane_e = _ROUTER_LANE0 + e
        block_e = block_e + jnp.where(pend_blk[:, lane_e:lane_e + 1] <= blk, 1.0, 0.0)
    block_e = jnp.minimum(block_e, N_EXPERTS - 1.0)
    last = _ROUTER_LANE0 + N_EXPERTS - 1
    nused = pend_blk[:, last:last + 1]
    pend_wide = jnp.concatenate([pend_rows, jnp.zeros((1, _META_LANES - LANES), F32)], axis=1)
    sub = lax.broadcasted_iota(jnp.int32, (8, _META_LANES), 0)
    meta = jnp.where(sub == _M_BLOCK_E, block_e, jnp.where(sub == _M_PAD_END, pend_wide, nused))
    meta_ref[...] = meta.astype(jnp.int32)


def _plan(cnt, idx):
    t = idx.shape[1]
    return pl.pallas_call(
        _plan_kernel,
        grid=(1,),
        in_specs=[pl.BlockSpec(cnt.shape, lambda i: (0, 0)),
                  pl.BlockSpec(idx.shape, lambda i: (0, 0))],
        out_specs=[pl.BlockSpec((8, t), lambda i: (0, 0)),
                   pl.BlockSpec((8, _META_LANES), lambda i: (0, 0))],
        out_shape=[jax.ShapeDtypeStruct((8, t), jnp.int32),
                   jax.ShapeDtypeStruct((8, _META_LANES), jnp.int32)],
        compiler_params=_cparams(1),
        name="plan",
    )(cnt, idx)


_SLAB = BM * _PACK_ROWS


def _dispatch_kernel(meta_ref, d_ref, xn_ref, xpad_ref, zbuf, sem, zsem):
    i = pl.program_id(0)
    td = xn_ref.shape[0] // _PACK_ROWS

    def zero_copy(e):
        pend = meta_ref[_M_PAD_END, _ROUTER_LANE0 + e]
        pstart = meta_ref[_M_PAD_END, _ROUTER_LANE0 + e - 1]
        first = pl.multiple_of((pend - BM) * _PACK_ROWS, _SLAB)
        cp = pltpu.make_async_copy(zbuf, xpad_ref.at[pl.ds(first, _SLAB), :], zsem)
        return pend > pstart, cp

    @pl.when(i == 0)
    def _():
        zbuf[...] = jnp.zeros(zbuf.shape, zbuf.dtype)
        for e in range(N_EXPERTS):
            nonempty, cp = zero_copy(e)
            pl.when(nonempty)(cp.start)
        for e in range(N_EXPERTS):
            nonempty, cp = zero_copy(e)
            pl.when(nonempty)(cp.wait)

        def tail_copy(b):
            return pltpu.make_async_copy(
                zbuf, xpad_ref.at[pl.ds(pl.multiple_of(b * _SLAB, _SLAB), _SLAB), :], zsem)

        first_unused = meta_ref[_M_NUSED, 0]
        n_blocks = xpad_ref.shape[0] // _SLAB
        lax.fori_loop(first_unused, n_blocks, lambda b, c: (tail_copy(b).start(), c)[1], 0)
        lax.fori_loop(first_unused, n_blocks, lambda b, c: (tail_copy(b).wait(), c)[1], 0)

    def row_copy(t, dst):
        src = pl.ds(pl.multiple_of(t * _PACK_ROWS, _PACK_ROWS), _PACK_ROWS)
        dsl = pl.ds(pl.multiple_of(dst * _PACK_ROWS, _PACK_ROWS), _PACK_ROWS)
        return pltpu.make_async_copy(xn_ref.at[src, :], xpad_ref.at[dsl, :], sem)

    def start(t, carry):
        row_copy(t, d_ref[0, t]).start(priority=0)
        row_copy(t, d_ref[1, t]).start(priority=1)
        return carry

    def wait(t, carry):
        row_copy(0, 0).wait()
        row_copy(0, 0).wait()
        return carry

    lax.fori_loop(0, td, start, 0, unroll=8)
    lax.fori_loop(0, td, wait, 0, unroll=8)


def _dispatch(meta, dest, xn_packed, rows):
    t = dest.shape[1]
    td = TD
    return pl.pallas_call(
        _dispatch_kernel,
        grid_spec=pltpu.PrefetchScalarGridSpec(
            num_scalar_prefetch=1,
            grid=(t // td,),
            in_specs=[pl.BlockSpec((8, td), lambda i, m: (0, i), memory_space=pltpu.SMEM),
                      pl.BlockSpec((td * _PACK_ROWS, LANES), lambda i, m: (i, 0))],
            out_specs=pl.BlockSpec(memory_space=pl.ANY),
            scratch_shapes=[pltpu.VMEM((_SLAB, LANES), BF16),
                            pltpu.SemaphoreType.DMA(()),
                            pltpu.SemaphoreType.DMA(())]),
        out_shape=jax.ShapeDtypeStruct((rows * _PACK_ROWS, LANES), BF16),
        compiler_params=_cparams(1),
        name="dispatch",
    )(meta, dest, xn_packed)


def _expert_kernel(meta_ref, xpad_ref, wg_ref, wu_ref, wd_ref, ypad_ref, wg_s, wu_s, wd_s,
                   xbuf, ybuf, xstage_s, ystage_s, xsem, ysem):
    e = pl.program_id(0)
    nused = meta_ref[_M_NUSED, 0]
    lane_e = _ROUTER_LANE0 + e
    first_blk = lax.div(meta_ref[_M_PAD_END, lane_e - 1], BM)
    n_blk = lax.div(meta_ref[_M_PAD_END, lane_e], BM) - first_blk

    def block_rows(g):
        return pl.ds(pl.multiple_of(g * _SLAB, _SLAB), _SLAB)

    def x_copy(g, slot):
        return pltpu.make_async_copy(xpad_ref.at[block_rows(g), :], xbuf.at[slot], xsem.at[slot])

    def y_copy(g, slot):
        return pltpu.make_async_copy(ybuf.at[slot], ypad_ref.at[block_rows(g), :], ysem.at[slot])

    @pl.when(e == 0)
    def _():
        x_copy(0, 0).start()

    @pl.when(n_blk > 0)
    def _():
        wg_s[...] = wg_ref[...].astype(BF16)
        wu_s[...] = wu_ref[...].astype(BF16)
        wd_s[...] = wd_ref[...].astype(BF16)

    def body(j, carry):
        g = first_blk + j
        slot = lax.rem(g, 2)
        x_copy(g, slot).wait()

        @pl.when(g + 1 < nused)
        def _():
            x_copy(g + 1, 1 - slot).start()

        xb = _load_slabs(xbuf.at[slot], BM, xstage_s).astype(BF16)
        gate = jnp.dot(xb, wg_s[...], preferred_element_type=F32)
        up = jnp.dot(xb, wu_s[...], preferred_element_type=F32)
        hb = (gate * jax.nn.sigmoid(gate) * up).astype(BF16)
        y = jnp.dot(hb, wd_s[...], preferred_element_type=F32)

        @pl.when(g >= 2)
        def _():
            y_copy(g - 2, slot).wait()

        _store_slabs(ybuf.at[slot], y, BM, ystage_s)
        y_copy(g, slot).start()
        return carry

    lax.fori_loop(0, n_blk, body, 0)

    @pl.when(e == pl.num_programs(0) - 1)
    def _():
        for back in (2, 1):
            @pl.when(nused >= back)
            def _():
                y_copy(nused - back, lax.rem(nused - back, 2)).wait()

        ybuf[0] = jnp.zeros(ybuf.shape[1:], ybuf.dtype)
        n_blocks = ypad_ref.shape[0] // _SLAB
        lax.fori_loop(nused, n_blocks, lambda b, c: (y_copy(b, 0).start(), c)[1], 0)
        lax.fori_loop(nused, n_blocks, lambda b, c: (y_copy(b, 0).wait(), c)[1], 0)


def _experts(meta, xpad, wg, wu, wd):
    n_exp, d, de = wg.shape
    wmap = lambda e, m: (e, 0, 0)
    return pl.pallas_call(
        _expert_kernel,
        grid_spec=pltpu.PrefetchScalarGridSpec(
            num_scalar_prefetch=1,
            grid=(n_exp,),
            in_specs=[pl.BlockSpec(memory_space=pl.ANY),
                      pl.BlockSpec((None, d, de), wmap),
                      pl.BlockSpec((None, d, de), wmap),
                      pl.BlockSpec((None, de, d), wmap)],
            out_specs=pl.BlockSpec(memory_space=pl.ANY),
            scratch_shapes=[pltpu.VMEM((d, de), BF16),
                            pltpu.VMEM((d, de), BF16),
                            pltpu.VMEM((de, d), BF16),
                            pltpu.VMEM((2, _SLAB, LANES), BF16),
                            pltpu.VMEM((2, _SLAB, LANES), BF16),
                            pltpu.VMEM((_SLAB, LANES), F32),
                            pltpu.VMEM((_SLAB, LANES), F32),
                            pltpu.SemaphoreType.DMA((2,)),
                            pltpu.SemaphoreType.DMA((2,))]),
        out_shape=jax.ShapeDtypeStruct(xpad.shape, BF16),
        compiler_params=_cparams(1),
        name="experts",
    )(meta, xpad, wg, wu, wd)


def _combine_kernel(dc_ref, dn_ref, x2_ref, route_ref, g_ref, ypad_ref, out_ref, ybuf, sem,
                    stage0_s, stage1_s):
    i = pl.program_id(0)
    n = pl.num_programs(0)
    tc = x2_ref.shape[0]

    def row_copy(src, slot, which, t):
        ssl = pl.ds(pl.multiple_of(src * _PACK_ROWS, _PACK_ROWS), _PACK_ROWS)
        dsl = pl.ds(pl.multiple_of(t * _PACK_ROWS, _PACK_ROWS), _PACK_ROWS)
        return pltpu.make_async_copy(ypad_ref.at[ssl, :], ybuf.at[slot, which, dsl, :],
                                     sem.at[slot])

    def issue(d_ref, slot):
        def body(t, carry):
            row_copy(d_ref[0, t], slot, 0, t).start(priority=0)
            row_copy(d_ref[1, t], slot, 1, t).start(priority=1)
            return carry
        lax.fori_loop(0, tc, body, 0, unroll=8)

    slot = i % 2

    @pl.when(i == 0)
    def _():
        issue(dc_ref, 0)

    @pl.when(i + 1 < n)
    def _():
        issue(dn_ref, 1 - slot)

    def wait(t, carry):
        row_copy(0, slot, 0, 0).wait()
        row_copy(0, slot, 1, 0).wait()
        return carry
    lax.fori_loop(0, tc, wait, 0, unroll=8)

    route = route_ref[...]
    p0 = route[:, _R_P0:_R_P0 + 1]
    p1 = route[:, _R_P1:_R_P1 + 1]
    y0 = _load_slabs(ybuf.at[slot, 0], tc, stage0_s)
    y1 = _load_slabs(ybuf.at[slot, 1], tc, stage1_s)
    x3 = x2_ref[...] + p0 * y0 + p1 * y1
    out_ref[...] = _rms(x3, g_ref[...])


def _combine(dest, x2, route, g, ypad):
    t, d = x2.shape
    tc = TC
    nt = t // tc
    cur = pl.BlockSpec((8, tc), lambda i: (0, i), memory_space=pltpu.SMEM)
    nxt = pl.BlockSpec((8, tc), lambda i: (0, jnp.minimum(i + 1, nt - 1)),
                       memory_space=pltpu.SMEM)
    return pl.pallas_call(
        _combine_kernel,
        grid=(nt,),
        in_specs=[cur, nxt,
                  pl.BlockSpec((tc, d), lambda i: (i, 0)),
                  pl.BlockSpec((tc, LANES), lambda i: (i, 0)),
                  pl.BlockSpec((1, d), lambda i: (0, 0)),
                  pl.BlockSpec(memory_space=pl.ANY)],
        out_specs=pl.BlockSpec((tc, d), lambda i: (i, 0)),
        out_shape=jax.ShapeDtypeStruct((t, d), F32),
        scratch_shapes=[pltpu.VMEM((2, 2, tc * _PACK_ROWS, LANES), BF16),
                        pltpu.SemaphoreType.DMA((2,)),
                        pltpu.VMEM((tc * _PACK_ROWS, LANES), F32),
                        pltpu.VMEM((tc * _PACK_ROWS, LANES), F32)],
        compiler_params=_cparams(1),
        name="combine",
    )(dest, dest, x2, route, g, ypad)


def _pad_cols(a, width):
    return jnp.pad(a, ((0, 0), (0, width - a.shape[1])))


def kernel(x, mem, norm_mix, w_in, b_gates, conv_w, conv_b, att_sinks, norm_att_out, norm_ml_out,
           w_out, norm_cross, norm_mem, w_cq, w_ckv, w_co, norm_ffn, w_router_group,
           b_router_group, w_router_expert, b_router_expert, w_e_gate, w_e_up, w_e_down,
           norm_final):
    bsz, seq, d = x.shape
    t = bsz * seq
    depth = w_in.shape[0]
    assert depth == 1, "the final RMSNorm is fused into the last layer's combine kernel"
    xs = x.reshape(t, d)
    mem2d = mem.reshape(-1, d)

    for l in range(depth):
        n_main = _SEG["om"][1]
        wl = w_in[l]
        w_packed = jnp.concatenate(
            [wl[:, :n_main], _pad_cols(wl[:, n_main:n_main + ML_HEADS], LANES),
             _pad_cols(wl[:, n_main + ML_HEADS:], LANES)], axis=1).astype(BF16)
        bi = _pad_cols(b_gates[l][None, :ML_HEADS], LANES)
        bf = _pad_cols(b_gates[l][None, ML_HEADS:], LANES)
        w_r = _pad_cols(jnp.concatenate([w_router_group[l], w_router_expert[l]], axis=1), LANES)
        w_r_hi = w_r.astype(BF16)
        w_r_lo = (w_r - w_r_hi.astype(F32)).astype(BF16)
        w_r_packed = jnp.concatenate([w_r_hi, w_r_lo], axis=1)
        b_r = _pad_cols(jnp.concatenate([b_router_group[l], b_router_expert[l]])[None, :], LANES)

        qa, ka, va, qk, vm, om, gi, gf = _in_proj(xs, norm_mix[l][None, :], w_packed)
        att = _swa(qa, ka, va, att_sinks[l], norm_att_out[l][None, :], bsz, seq)
        hm = _mlstm(qk, vm, om, gi, gf, conv_w[l], conv_b[l][None, :], bi, bf,
                    norm_ml_out[l][None, :], bsz, seq)

        kmem, vmem_ = _kv_prep(mem2d, norm_mem[l][None, :], w_ckv[l].astype(BF16))
        x2, xn, route, idx, cnt = _mid(xs, att, hm, w_out[l].astype(BF16), norm_cross[l][None, :],
                                       w_cq[l].astype(BF16), kmem, vmem_, w_co[l].astype(BF16),
                                       norm_ffn[l][None, :], w_r_packed, b_r, seq)

        nbk = (t * TOP_K) // BM + N_EXPERTS
        assert nbk <= _META_LANES
        dest, meta = _plan(cnt, idx)
        xpad = _dispatch(meta, dest, xn, nbk * BM)
        ypad = _experts(meta, xpad, w_e_gate[l], w_e_up[l], w_e_down[l])
        xs = _combine(dest, x2, route, norm_final[None, :], ypad)
    return xs.reshape(bsz, seq, d)
```

```python
import functools
import math

import jax
import jax.numpy as jnp
from jax import lax
from jax.experimental import pallas as pl
from jax.experimental.pallas import tpu as pltpu

F32 = jnp.float32
BF16 = jnp.bfloat16
EPS = 1e-6
NEG_INF = float("-inf")

ATT_HEADS = 8
ATT_HEAD_DIM = 64
ATT_Q = ATT_HEADS * ATT_HEAD_DIM
ATT_KV = 128
WINDOW = 128
ML_HEADS = 4
ML_HEAD_DIM = 128
ML_W = ML_HEADS * ML_HEAD_DIM
CONV_WIDTH = 4
X_HEADS = 4
X_HEAD_DIM = 128
N_GROUPS = 4
EXPERTS_PER_GROUP = 8
N_EXPERTS = N_GROUPS * EXPERTS_PER_GROUP
TOP_K = 2

LANES = 128
TM_IN = 512
IN_PARTS = 2
SWA_QB = 2
ML_CHUNK = 256
ML_NB = 1
TM_MID = 512
MID_PARTS = 2
TD = 512
TC = 256
BM = 256
VMEM_LIMIT = 56 * 1024 * 1024

_NT = (((1,), (1,)), ((), ()))
_TN = (((0,), (0,)), ((), ()))


def _rms(x, g):
    return x * lax.rsqrt(jnp.mean(x * x, axis=-1, keepdims=True) + EPS) * g


def _cparams(n_axes):
    return pltpu.CompilerParams(dimension_semantics=("arbitrary",) * n_axes,
                                vmem_limit_bytes=VMEM_LIMIT)


_SEG = {"qa": (0, 512), "ka": (512, 640), "va": (640, 768), "qk": (768, 1792),
        "vm": (1792, 2304), "om": (2304, 2816), "gi": (2816, 2944), "gf": (2944, 3072)}


def _in_proj_kernel(x_ref, g_ref, w_ref, qa_ref, ka_ref, va_ref, qk_ref, vm_ref, og_ref,
                    gi_ref, gf_ref):
    tm = x_ref.shape[0] // IN_PARTS
    rows = [pl.ds(p * tm, tm) for p in range(IN_PARTS)]
    hb = [_rms(x_ref[r, :], g_ref[...]).astype(BF16) for r in rows]

    def seg(p, name):
        lo, hi = _SEG[name]
        return jnp.dot(hb[p], w_ref[:, lo:hi], preferred_element_type=F32)

    plain = {"qa": qa_ref, "ka": ka_ref, "va": va_ref, "qk": qk_ref, "vm": vm_ref, "gi": gi_ref,
             "gf": gf_ref}
    for p, r in enumerate(rows):
        for name, ref in plain.items():
            ref[r, :] = seg(p, name).astype(ref.dtype)
        og_ref[r, :] = jax.nn.sigmoid(seg(p, "om")).astype(og_ref.dtype)


def _in_proj(x2d, g, w_packed):
    t, d = x2d.shape
    tm = TM_IN
    widths = [(512, BF16), (128, BF16), (128, BF16), (1024, F32), (512, BF16), (512, BF16),
              (128, F32), (128, F32)]
    const = lambda i: (0, 0)
    return pl.pallas_call(
        _in_proj_kernel,
        grid=(t // tm,),
        in_specs=[pl.BlockSpec((tm, d), lambda i: (i, 0)),
                  pl.BlockSpec((1, d), const),
                  pl.BlockSpec(w_packed.shape, const)],
        out_specs=[pl.BlockSpec((tm, w), lambda i: (i, 0)) for w, _ in widths],
        out_shape=[jax.ShapeDtypeStruct((t, w), dt) for w, dt in widths],
        compiler_params=_cparams(1),
        name="in_proj",
    )(x2d, g, w_packed)


def _swa_kernel(sink_ref, q_ref, kp_ref, kc_ref, vp_ref, vc_ref, gn_ref, o_ref, bias_s):
    b = pl.program_id(0)
    n = pl.program_id(1)

    @pl.when((b == 0) & (n == 0))
    def _():
        ti = lax.broadcasted_iota(jnp.int32, (WINDOW, 2 * WINDOW), 0)
        si = lax.broadcasted_iota(jnp.int32, (WINDOW, 2 * WINDOW), 1)
        dist = ti + WINDOW - si
        band = (dist >= 0) & (dist < WINDOW)
        distf = dist.astype(F32)
        for hidx in range(ATT_HEADS):
            slope = 2.0 ** (-8.0 * (hidx + 1) / ATT_HEADS)
            full = jnp.where(band, -slope * distf, NEG_INF)
            bias_s[1, hidx] = full
            bias_s[0, hidx] = jnp.where(si >= WINDOW, full, NEG_INF)

    scale = 1.0 / math.sqrt(ATT_HEAD_DIM)
    k_all = jnp.concatenate([kp_ref[...], kc_ref[...]], axis=0).astype(F32) * scale
    v_all = jnp.concatenate([vp_ref[...], vc_ref[...]], axis=0).astype(F32)
    lane = lax.broadcasted_iota(jnp.int32, k_all.shape, 1)
    low = lane < ATT_HEAD_DIM

    def lane_variants(a):
        ar = pltpu.roll(a, ATT_HEAD_DIM, axis=1)
        z = jnp.zeros_like(a)
        return ((jnp.where(low, a, z).astype(BF16), jnp.where(low, z, ar).astype(BF16)),
                (jnp.where(low, ar, z).astype(BF16), jnp.where(low, z, a).astype(BF16)))

    k_ops = lane_variants(k_all)
    v_ops = lane_variants(v_all)
    n_tiles = ATT_Q // LANES
    kv_of = lambda j: (2 * j) // (ATT_HEADS // 2)
    blocks = range(SWA_QB)
    rows = [slice(WINDOW * i, WINDOW * (i + 1)) for i in blocks]
    keys = [slice(WINDOW * i, WINDOW * (i + 2)) for i in blocks]
    bias_slot = [jnp.where(n == 0, 0, 1) if i == 0 else 1 for i in blocks]

    scores = [[[lax.dot_general(q_ref[rows[i], LANES * j:LANES * (j + 1)],
                                k_ops[kv_of(j)][par][keys[i]], _NT, preferred_element_type=F32)
                for par in range(2)] for j in range(n_tiles)] for i in blocks]
    probs = [[[None, None] for _ in range(n_tiles)] for _ in blocks]
    rinv = [[[None, None] for _ in range(n_tiles)] for _ in blocks]
    for i in blocks:
        for j in range(n_tiles):
            for par in range(2):
                hidx = 2 * j + par
                logits = scores[i][j][par] + bias_s[bias_slot[i], hidx]
                sink = sink_ref[hidx]
                mx = jnp.maximum(jnp.max(logits, axis=-1, keepdims=True), sink)
                e = jnp.exp(logits - mx)
                den = jnp.sum(e, axis=-1, keepdims=True) + jnp.exp(sink - mx)
                probs[i][j][par] = e.astype(BF16)
                rinv[i][j][par] = 1.0 / den
    lane_o = lax.broadcasted_iota(jnp.int32, (WINDOW, LANES), 1)
    for i in blocks:
        tiles = []
        for j in range(n_tiles):
            acc = (jnp.dot(probs[i][j][0], v_ops[kv_of(j)][0][keys[i]], preferred_element_type=F32)
                   + jnp.dot(probs[i][j][1], v_ops[kv_of(j)][1][keys[i]],
                             preferred_element_type=F32))
            tiles.append(acc * jnp.where(lane_o < ATT_HEAD_DIM, rinv[i][j][0], rinv[i][j][1]))
        att = jnp.concatenate(tiles, axis=1)
        o_ref[rows[i], :] = _rms(att, gn_ref[...]).astype(BF16)


def _swa(qa, ka, va, sinks, gn, bsz, seq):
    nb = seq // WINDOW
    ns = nb // SWA_QB
    t = bsz * seq
    cur = lambda b, n: (b * ns + n, 0)
    prev = lambda b, n: (b * nb + jnp.maximum(n * SWA_QB - 1, 0), 0)
    return pl.pallas_call(
        _swa_kernel,
        grid=(bsz, ns),
        in_specs=[pl.BlockSpec(memory_space=pltpu.SMEM),
                  pl.BlockSpec((SWA_QB * WINDOW, ATT_Q), cur),
                  pl.BlockSpec((WINDOW, ATT_KV), prev),
                  pl.BlockSpec((SWA_QB * WINDOW, ATT_KV), cur),
                  pl.BlockSpec((WINDOW, ATT_KV), prev),
                  pl.BlockSpec((SWA_QB * WINDOW, ATT_KV), cur),
                  pl.BlockSpec((1, ATT_Q), lambda b, n: (0, 0))],
        out_specs=pl.BlockSpec((SWA_QB * WINDOW, ATT_Q), cur),
        out_shape=jax.ShapeDtypeStruct((t, ATT_Q), BF16),
        scratch_shapes=[pltpu.VMEM((2, ATT_HEADS, WINDOW, 2 * WINDOW), F32)],
        compiler_params=_cparams(2),
        name="swa",
    )(sinks, qa, ka, ka, va, va, gn)


def _split3(a):
    hi = a.astype(BF16)
    r1 = a - hi.astype(F32)
    mid = r1.astype(BF16)
    lo = (r1 - mid.astype(F32)).astype(BF16)
    return hi, mid, lo


def _mlstm_kernel(qk_ref, v_ref, og_ref, gi_ref, gf_ref, cw_ref, cb_ref, bi_ref, bf_ref, gn_ref,
                  out_ref, xbuf, cn_s, m_s):
    c = pl.program_id(1)
    nseq, L = qk_ref.shape[0], qk_ref.shape[1]
    HD = ML_HEAD_DIM
    seqs = range(nseq)
    heads = range(ML_HEADS)
    pairs = [(s, h) for s in seqs for h in heads]
    sls = [slice(HD * h, HD * (h + 1)) for h in heads]

    @pl.when(c == 0)
    def _():
        xbuf[:, 0:8, :] = jnp.zeros((nseq, 8, xbuf.shape[2]), F32)
        cn_s[...] = jnp.zeros(cn_s.shape, F32)
        m_s[...] = jnp.zeros(m_s.shape, F32)

    ti = lax.broadcasted_iota(jnp.int32, (L, L), 0)
    si = lax.broadcasted_iota(jnp.int32, (L, L), 1)
    causal = si <= ti
    tril = jnp.where(causal, 1.0, 0.0).astype(BF16)
    lane_l = lax.broadcasted_iota(jnp.int32, (L, HD), 1)
    ones_col = jnp.where(lane_l == 0, 1.0, 0.0).astype(BF16)

    gi, b = [], []
    for s in seqs:
        gi.append(gi_ref[s] + bi_ref[...])
        z = gf_ref[s] + bf_ref[...]
        lf = jnp.minimum(z, 0.0) - jnp.log1p(jnp.exp(-jnp.abs(z)))
        acc = jnp.zeros((L, LANES), F32)
        for part in _split3(lf):
            acc = acc + jnp.dot(tril, part, preferred_element_type=F32)
        b.append(acc)

    qb, kf = {}, {}
    for s in seqs:
        xbuf[s, 8:8 + L, :] = qk_ref[s]
        y = cb_ref[...] + cw_ref[CONV_WIDTH - 1:CONV_WIDTH, :] * xbuf[s, 8:8 + L, :]
        for j in range(CONV_WIDTH - 1):
            sh = CONV_WIDTH - 1 - j
            y = y + cw_ref[j:j + 1, :] * xbuf[s, 8 - sh:8 - sh + L, :]
        xbuf[s, 0:8, :] = xbuf[s, L:L + 8, :]
        qk = y * jax.nn.sigmoid(y)
        for h in heads:
            qb[s, h] = qk[:, sls[h]].astype(BF16)
            kf[s, h] = qk[:, ML_W + HD * h:ML_W + HD * (h + 1)] * (1.0 / math.sqrt(HD))
    kb = {p: kf[p].astype(BF16) for p in pairs}
    vext = {(s, h): jnp.concatenate([v_ref[s, :, sls[h]], ones_col], axis=1) for s, h in pairs}
    cn = {(s, h): cn_s[s * ML_HEADS + h] for s, h in pairs}
    qk_d = {p: lax.dot_general(qb[p], kb[p], _NT, preferred_element_type=F32) for p in pairs}
    qc = {p: jnp.dot(qb[p], cn[p].astype(BF16), preferred_element_type=F32) for p in pairs}

    m_inter, m_new, w, decay, rt = [], [], [], [], []
    for s in seqs:
        m_st = m_s[s]
        b_last = b[s][L - 1:L, :]
        m_inter.append(b[s] + m_st)
        log_w = b_last - b[s] + gi[s]
        m_new.append(jnp.maximum(b_last + m_st, jnp.max(log_w, axis=0, keepdims=True)))
        w.append(jnp.exp(log_w - m_new[s]))
        decay.append(jnp.exp(b_last + m_st - m_new[s]))
        rt.append(jnp.transpose(gi[s] - b[s]))

    s_b, kw_b, m_ts, a_inters = {}, {}, {}, {}
    for s, h in pairs:
        log_d = jnp.where(causal, b[s][:, h:h + 1] + rt[s][h:h + 1, :], NEG_INF)
        m_t = jnp.maximum(m_inter[s][:, h:h + 1], jnp.max(log_d, axis=1, keepdims=True))
        s_b[s, h] = (qk_d[s, h] * jnp.exp(log_d - m_t)).astype(BF16)
        kw_b[s, h] = (kf[s, h] * w[s][:, h:h + 1]).astype(BF16)
        m_ts[s, h] = m_t
        a_inters[s, h] = jnp.exp(m_inter[s][:, h:h + 1] - m_t)

    sv = {p: jnp.dot(s_b[p], vext[p], preferred_element_type=F32) for p in pairs}
    upd = {p: lax.dot_general(kw_b[p], vext[p], _TN, preferred_element_type=F32) for p in pairs}

    for s, h in pairs:
        nd = sv[s, h] + a_inters[s, h] * qc[s, h]
        num = nd[:, :HD]
        den = nd[:, HD:HD + 1]
        hh = num * (1.0 / jnp.maximum(jnp.abs(den), jnp.exp(-m_ts[s, h])))
        cn_s[s * ML_HEADS + h] = decay[s][:, h:h + 1] * cn[s, h] + upd[s, h]
        hm = og_ref[s, :, sls[h]].astype(F32) * hh
        out_ref[s, :, sls[h]] = _rms(hm, gn_ref[:, sls[h]]).astype(BF16)

    for s in seqs:
        m_s[s] = m_new[s]


def _mlstm(qk, vm, og, gi, gf, cw, cb, bi, bf, gn, bsz, seq):
    L = ML_CHUNK
    nc = seq // L
    nseq = ML_NB
    t = bsz * seq
    seq3 = lambda a: a.reshape(bsz, seq, a.shape[-1])
    row = lambda b, c: (b, c, 0)
    const = lambda b, c: (0, 0)
    out = pl.pallas_call(
        _mlstm_kernel,
        grid=(bsz // nseq, nc),
        in_specs=[pl.BlockSpec((nseq, L, 2 * ML_W), row),
                  pl.BlockSpec((nseq, L, ML_W), row),
                  pl.BlockSpec((nseq, L, ML_W), row),
                  pl.BlockSpec((nseq, L, LANES), row),
                  pl.BlockSpec((nseq, L, LANES), row),
                  pl.BlockSpec((CONV_WIDTH, 2 * ML_W), const),
                  pl.BlockSpec((1, 2 * ML_W), const),
                  pl.BlockSpec((1, LANES), const),
                  pl.BlockSpec((1, LANES), const),
                  pl.BlockSpec((1, ML_W), const)],
        out_specs=pl.BlockSpec((nseq, L, ML_W), row),
        out_shape=jax.ShapeDtypeStruct((bsz, seq, ML_W), BF16),
        scratch_shapes=[pltpu.VMEM((nseq, L + 8, 2 * ML_W), F32),
                        pltpu.VMEM((nseq * ML_HEADS, ML_HEAD_DIM, 2 * ML_HEAD_DIM), F32),
                        pltpu.VMEM((nseq, 1, LANES), F32)],
        compiler_params=_cparams(2),
        name="mlstm",
    )(seq3(qk), seq3(vm), seq3(og), seq3(gi), seq3(gf), cw, cb, bi, bf, gn)
    return out.reshape(t, ML_W)


def _kv_kernel(mem_ref, g_ref, w_ref, k_ref, v_ref):
    mn = _rms(mem_ref[...], g_ref[...]).astype(BF16)
    hw = k_ref.shape[1]
    k_ref[...] = jnp.dot(mn, w_ref[:, :hw], preferred_element_type=F32).astype(BF16)
    v_ref[...] = jnp.dot(mn, w_ref[:, hw:], preferred_element_type=F32).astype(BF16)


def _kv_prep(mem2d, g, w_ckv):
    r, d = mem2d.shape
    hw = w_ckv.shape[1] // 2
    tm = 256
    return pl.pallas_call(
        _kv_kernel,
        grid=(r // tm,),
        in_specs=[pl.BlockSpec((tm, d), lambda i: (i, 0)),
                  pl.BlockSpec((1, d), lambda i: (0, 0)),
                  pl.BlockSpec(w_ckv.shape, lambda i: (0, 0))],
        out_specs=[pl.BlockSpec((tm, hw), lambda i: (i, 0))] * 2,
        out_shape=[jax.ShapeDtypeStruct((r, hw), BF16)] * 2,
        compiler_params=_cparams(1),
        name="kv_prep",
    )(mem2d, g, w_ckv)


_R_E0, _R_E1, _R_RANK0, _R_RANK1, _R_P0, _R_P1 = range(6)
_ROUTER_LANE0 = N_GROUPS


_PACK_ROWS = 8


def _store_slabs(ref, v, rows, stage, first=0):
    base = first * _PACK_ROWS
    for r in range(_PACK_ROWS):
        stage[pl.ds(base + r, rows, stride=_PACK_ROWS), :] = v[:, LANES * r:LANES * (r + 1)]
    span = pl.ds(base, rows * _PACK_ROWS)
    ref[span, :] = stage[span, :].astype(ref.dtype)


def _load_slabs(ref, rows, stage):
    stage[...] = ref[...].astype(F32)
    return jnp.concatenate(
        [stage[pl.ds(r, rows, stride=_PACK_ROWS), :] for r in range(_PACK_ROWS)], axis=1)


def _mid_kernel(x_ref, att_ref, hm_ref, wo_ref, gc_ref, wq_ref, k_ref, v_ref, wco_ref, gf_ref,
                wr_ref, br_ref, x2_ref, xn_ref, route_ref, idx_ref, cnt_ref, carry_s, stage_s):
    i = pl.program_id(0)
    tm = x_ref.shape[0] // MID_PARTS
    parts = range(MID_PARTS)
    rows = [pl.ds(p * tm, tm) for p in parts]

    @pl.when(i == 0)
    def _():
        carry_s[...] = jnp.zeros(carry_s.shape, F32)

    x1 = [x_ref[rows[p], :]
          + jnp.dot(att_ref[rows[p], :], wo_ref[0:ATT_Q, :], preferred_element_type=F32)
          + jnp.dot(hm_ref[rows[p], :], wo_ref[ATT_Q:, :], preferred_element_type=F32)
          for p in parts]

    xc = [_rms(x1[p], gc_ref[...]).astype(BF16) for p in parts]
    qb = [jnp.dot(xc[p], wq_ref[...], preferred_element_type=F32).astype(BF16) for p in parts]
    sls = [slice(X_HEAD_DIM * h, X_HEAD_DIM * (h + 1)) for h in range(X_HEADS)]
    sc = [[lax.dot_general(qb[p][:, sl], k_ref[:, sl], _NT, preferred_element_type=F32)
           for sl in sls] for p in parts]
    es = [[None] * X_HEADS for _ in parts]
    rinv = [[None] * X_HEADS for _ in parts]
    for p in parts:
        for h in range(X_HEADS):
            s = sc[p][h] * (1.0 / math.sqrt(X_HEAD_DIM))
            e = jnp.exp(s - jnp.max(s, axis=-1, keepdims=True))
            es[p][h] = e.astype(BF16)
            rinv[p][h] = 1.0 / jnp.sum(e, axis=-1, keepdims=True)
    o = [jnp.concatenate(
        [jnp.dot(es[p][h], v_ref[:, sls[h]], preferred_element_type=F32) * rinv[p][h]
         for h in range(X_HEADS)], axis=1).astype(BF16) for p in parts]
    x2 = [x1[p] + jnp.dot(o[p], wco_ref[...], preferred_element_type=F32) for p in parts]

    logits_p = []
    for p in parts:
        x2_ref[rows[p], :] = x2[p]
        xn = _rms(x2[p], gf_ref[...])
        xh = xn.astype(BF16)
        _store_slabs(xn_ref, xn, tm, stage_s, first=p * tm)
        xl = (xn - xh.astype(F32)).astype(BF16)
        lg2 = jnp.dot(xh, wr_ref[...], preferred_element_type=F32)
        logits_p.append(lg2[:, :LANES] + lg2[:, LANES:]
                        + jnp.dot(xl, wr_ref[:, :LANES], preferred_element_type=F32) + br_ref[...])

    for p in parts:
        _route_part(logits_p[p], rows[p], p * tm, route_ref, idx_ref, cnt_ref, carry_s)


def _route_part(logits, rows, first, route_ref, idx_ref, cnt_ref, carry_s):
    tm = logits.shape[0]
    lane = lax.broadcasted_iota(jnp.int32, (tm, LANES), 1)
    lanef = lane.astype(F32)
    big = float(4 * LANES)
    gl = jnp.where(lane < N_GROUPS, logits, NEG_INF)
    gmax = jnp.max(gl, axis=-1, keepdims=True)
    gsel = jnp.min(jnp.where(gl == gmax, lanef, big), axis=-1, keepdims=True)
    gw = 1.0 / jnp.sum(jnp.exp(gl - gmax), axis=-1, keepdims=True)
    lo_lane = _ROUTER_LANE0 + EXPERTS_PER_GROUP * gsel
    in_group = (lanef >= lo_lane) & (lanef < lo_lane + EXPERTS_PER_GROUP)
    el = jnp.where(in_group, logits, NEG_INF)
    v0 = jnp.max(el, axis=-1, keepdims=True)
    i0 = jnp.min(jnp.where(el == v0, lanef, big), axis=-1, keepdims=True)
    el2 = jnp.where(lanef == i0, NEG_INF, el)
    v1 = jnp.max(el2, axis=-1, keepdims=True)
    i1 = jnp.min(jnp.where(el2 == v1, lanef, big), axis=-1, keepdims=True)
    tt = jnp.exp(v1 - v0)
    p0 = gw / (1.0 + tt)
    p1 = gw * tt / (1.0 + tt)

    sel0 = lanef == i0
    sel1 = lanef == i1
    mb = jnp.where(sel0 | sel1, 1.0, 0.0).astype(BF16)
    ti = lax.broadcasted_iota(jnp.int32, (tm, tm), 0)
    si = lax.broadcasted_iota(jnp.int32, (tm, tm), 1)
    strict = jnp.where(si < ti, 1.0, 0.0).astype(BF16)
    carry = carry_s[...]
    pref = jnp.dot(strict, mb, preferred_element_type=F32) + carry
    r0 = jnp.sum(jnp.where(sel0, pref, 0.0), axis=-1, keepdims=True)
    r1 = jnp.sum(jnp.where(sel1, pref, 0.0), axis=-1, keepdims=True)
    carry = carry + jnp.sum(mb.astype(F32), axis=0, keepdims=True)
    carry_s[...] = carry
    cnt_ref[...] = carry

    route = jnp.zeros((tm, LANES), F32)
    for idx, col in ((_R_E0, i0 - _ROUTER_LANE0), (_R_E1, i1 - _ROUTER_LANE0), (_R_RANK0, r0),
                     (_R_RANK1, r1), (_R_P0, p0), (_R_P1, p1)):
        route = jnp.where(lane == idx, col, route)
    route_ref[rows, :] = route
    idx_ref[:, pl.ds(first, tm)] = jnp.transpose(route)[0:8, :].astype(jnp.int32)


def _mid(x2d, att, hm, wo, gc, wq, kmem, vmem_, wco, gf, wr, br, seq):
    t, d = x2d.shape
    tm = TM_MID
    per_b = seq // tm
    rowmap = lambda i: (i, 0)
    const = lambda i: (0, 0)
    bmap = lambda i: (i // per_b, 0)
    mem_len = kmem.shape[0] // (t // seq)
    return pl.pallas_call(
        _mid_kernel,
        grid=(t // tm,),
        in_specs=[pl.BlockSpec((tm, d), rowmap),
                  pl.BlockSpec((tm, ATT_Q), rowmap),
                  pl.BlockSpec((tm, ML_W), rowmap),
                  pl.BlockSpec(wo.shape, const),
                  pl.BlockSpec((1, d), const),
                  pl.BlockSpec(wq.shape, const),
                  pl.BlockSpec((mem_len, kmem.shape[1]), bmap),
                  pl.BlockSpec((mem_len, vmem_.shape[1]), bmap),
                  pl.BlockSpec(wco.shape, const),
                  pl.BlockSpec((1, d), const),
                  pl.BlockSpec(wr.shape, const),
                  pl.BlockSpec((1, LANES), const)],
        out_specs=[pl.BlockSpec((tm, d), rowmap),
                   pl.BlockSpec((tm * _PACK_ROWS, LANES), rowmap),
                   pl.BlockSpec((tm, LANES), rowmap),
                   pl.BlockSpec((8, tm), lambda i: (0, i)),
                   pl.BlockSpec((1, LANES), const)],
        out_shape=[jax.ShapeDtypeStruct((t, d), F32),
                   jax.ShapeDtypeStruct((t * _PACK_ROWS, LANES), BF16),
                   jax.ShapeDtypeStruct((t, LANES), F32),
                   jax.ShapeDtypeStruct((8, t), jnp.int32),
                   jax.ShapeDtypeStruct((1, LANES), F32)],
        scratch_shapes=[pltpu.VMEM((1, LANES), F32),
                        pltpu.VMEM((tm * _PACK_ROWS, LANES), F32)],
        compiler_params=_cparams(1),
        name="mid",
    )(x2d, att, hm, wo, gc, wq, kmem, vmem_, wco, gf, wr, br)


_M_BLOCK_E, _M_PAD_END, _M_NUSED = range(3)
_META_LANES = 2 * LANES


def _plan_kernel(cnt_ref, idx_ref, dest_ref, meta_ref):
    cnt = cnt_ref[...]
    lane = lax.broadcasted_iota(jnp.int32, (1, LANES), 1)
    is_expert = (lane >= _ROUTER_LANE0) & (lane < _ROUTER_LANE0 + N_EXPERTS)
    nblk = jnp.where(is_expert, jnp.floor((cnt + (BM - 1)) * (1.0 / BM)), 0.0)
    jj = lax.broadcasted_iota(jnp.int32, (LANES, LANES), 0)
    kk = lax.broadcasted_iota(jnp.int32, (LANES, LANES), 1)
    upper = jnp.where(jj <= kk, 1.0, 0.0).astype(BF16)
    pend_blk = jnp.dot(jnp.broadcast_to(nblk, (8, LANES)).astype(BF16), upper,
                       preferred_element_type=F32)[0:1, :]
    pstart_rows = (pend_blk - nblk) * BM
    pend_rows = pend_blk * BM

    idx = idx_ref[...]
    off = jnp.zeros(idx.shape, F32)
    for e in range(N_EXPERTS):
        lane_e = _ROUTER_LANE0 + e
        off = jnp.where(idx == e, pstart_rows[:, lane_e:lane_e + 1], off)
    ranks = pltpu.roll(idx, idx.shape[0] - 2, axis=0)
    dest_ref[...] = ranks + off.astype(jnp.int32)

    blk = lax.broadcasted_iota(jnp.int32, (1, _META_LANES), 1).astype(F32)
    block_e = jnp.zeros((1, _META_LANES), F32)
    for e in range(N_EXPERTS):
        lane_e = _ROUTER_LANE0 + e
        block_e = block_e + jnp.where(pend_blk[:, lane_e:lane_e + 1] <= blk, 1.0, 0.0)
    block_e = jnp.minimum(block_e, N_EXPERTS - 1.0)
    last = _ROUTER_LANE0 + N_EXPERTS - 1
    nused = pend_blk[:, last:last + 1]
    pend_wide = jnp.concatenate([pend_rows, jnp.zeros((1, _META_LANES - LANES), F32)], axis=1)
    sub = lax.broadcasted_iota(jnp.int32, (8, _META_LANES), 0)
    meta = jnp.where(sub == _M_BLOCK_E, block_e, jnp.where(sub == _M_PAD_END, pend_wide, nused))
    meta_ref[...] = meta.astype(jnp.int32)


def _plan(cnt, idx):
    t = idx.shape[1]
    return pl.pallas_call(
        _plan_kernel,
        grid=(1,),
        in_specs=[pl.BlockSpec(cnt.shape, lambda i: (0, 0)),
                  pl.BlockSpec(idx.shape, lambda i: (0, 0))],
        out_specs=[pl.BlockSpec((8, t), lambda i: (0, 0)),
                   pl.BlockSpec((8, _META_LANES), lambda i: (0, 0))],
        out_shape=[jax.ShapeDtypeStruct((8, t), jnp.int32),
                   jax.ShapeDtypeStruct((8, _META_LANES), jnp.int32)],
        compiler_params=_cparams(1),
        name="plan",
    )(cnt, idx)


_SLAB = BM * _PACK_ROWS


def _dispatch_kernel(meta_ref, d_ref, xn_ref, xpad_ref, zbuf, sem, zsem):
    i = pl.program_id(0)
    td = xn_ref.shape[0] // _PACK_ROWS

    def zero_copy(e):
        pend = meta_ref[_M_PAD_END, _ROUTER_LANE0 + e]
        pstart = meta_ref[_M_PAD_END, _ROUTER_LANE0 + e - 1]
        first = pl.multiple_of((pend - BM) * _PACK_ROWS, _SLAB)
        cp = pltpu.make_async_copy(zbuf, xpad_ref.at[pl.ds(first, _SLAB), :], zsem)
        return pend > pstart, cp

    @pl.when(i == 0)
    def _():
        zbuf[...] = jnp.zeros(zbuf.shape, zbuf.dtype)
        for e in range(N_EXPERTS):
            nonempty, cp = zero_copy(e)
            pl.when(nonempty)(cp.start)
        for e in range(N_EXPERTS):
            nonempty, cp = zero_copy(e)
            pl.when(nonempty)(cp.wait)

        def tail_copy(b):
            return pltpu.make_async_copy(
                zbuf, xpad_ref.at[pl.ds(pl.multiple_of(b * _SLAB, _SLAB), _SLAB), :], zsem)

        first_unused = meta_ref[_M_NUSED, 0]
        n_blocks = xpad_ref.shape[0] // _SLAB
        lax.fori_loop(first_unused, n_blocks, lambda b, c: (tail_copy(b).start(), c)[1], 0)
        lax.fori_loop(first_unused, n_blocks, lambda b, c: (tail_copy(b).wait(), c)[1], 0)

    def row_copy(t, dst):
        src = pl.ds(pl.multiple_of(t * _PACK_ROWS, _PACK_ROWS), _PACK_ROWS)
        dsl = pl.ds(pl.multiple_of(dst * _PACK_ROWS, _PACK_ROWS), _PACK_ROWS)
        return pltpu.make_async_copy(xn_ref.at[src, :], xpad_ref.at[dsl, :], sem)

    def start(t, carry):
        row_copy(t, d_ref[0, t]).start(priority=0)
        row_copy(t, d_ref[1, t]).start(priority=1)
        return carry

    def wait(t, carry):
        row_copy(0, 0).wait()
        row_copy(0, 0).wait()
        return carry

    lax.fori_loop(0, td, start, 0, unroll=8)
    lax.fori_loop(0, td, wait, 0, unroll=8)


def _dispatch(meta, dest, xn_packed, rows):
    t = dest.shape[1]
    td = TD
    return pl.pallas_call(
        _dispatch_kernel,
        grid_spec=pltpu.PrefetchScalarGridSpec(
            num_scalar_prefetch=1,
            grid=(t // td,),
            in_specs=[pl.BlockSpec((8, td), lambda i, m: (0, i), memory_space=pltpu.SMEM),
                      pl.BlockSpec((td * _PACK_ROWS, LANES), lambda i, m: (i, 0))],
            out_specs=pl.BlockSpec(memory_space=pl.ANY),
            scratch_shapes=[pltpu.VMEM((_SLAB, LANES), BF16),
                            pltpu.SemaphoreType.DMA(()),
                            pltpu.SemaphoreType.DMA(())]),
        out_shape=jax.ShapeDtypeStruct((rows * _PACK_ROWS, LANES), BF16),
        compiler_params=_cparams(1),
        name="dispatch",
    )(meta, dest, xn_packed)


def _expert_kernel(meta_ref, xpad_ref, wg_ref, wu_ref, wd_ref, ypad_ref, wg_s, wu_s, wd_s,
                   xbuf, ybuf, xstage_s, ystage_s, xsem, ysem):
    e = pl.program_id(0)
    nused = meta_ref[_M_NUSED, 0]
    lane_e = _ROUTER_LANE0 + e
    first_blk = lax.div(meta_ref[_M_PAD_END, lane_e - 1], BM)
    n_blk = lax.div(meta_ref[_M_PAD_END, lane_e], BM) - first_blk

    def block_rows(g):
        return pl.ds(pl.multiple_of(g * _SLAB, _SLAB), _SLAB)

    def x_copy(g, slot):
        return pltpu.make_async_copy(xpad_ref.at[block_rows(g), :], xbuf.at[slot], xsem.at[slot])

    def y_copy(g, slot):
        return pltpu.make_async_copy(ybuf.at[slot], ypad_ref.at[block_rows(g), :], ysem.at[slot])

    @pl.when(e == 0)
    def _():
        x_copy(0, 0).start(priority=1)

    @pl.when(n_blk > 0)
    def _():
        wg_s[...] = wg_ref[...].astype(BF16)
        wu_s[...] = wu_ref[...].astype(BF16)
        wd_s[...] = wd_ref[...].astype(BF16)

    def body(j, carry):
        g = first_blk + j
        slot = lax.rem(g, 2)
        x_copy(g, slot).wait()

        @pl.when(g + 1 < nused)
        def _():
            x_copy(g + 1, 1 - slot).start(priority=1)

        xb = _load_slabs(xbuf.at[slot], BM, xstage_s).astype(BF16)
        gate = jnp.dot(xb, wg_s[...], preferred_element_type=F32)
        up = jnp.dot(xb, wu_s[...], preferred_element_type=F32)
        hb = (gate * jax.nn.sigmoid(gate) * up).astype(BF16)
        y = jnp.dot(hb, wd_s[...], preferred_element_type=F32)

        @pl.when(g >= 2)
        def _():
            y_copy(g - 2, slot).wait()

        _store_slabs(ybuf.at[slot], y, BM, ystage_s)
        y_copy(g, slot).start()
        return carry

    lax.fori_loop(0, n_blk, body, 0)

    @pl.when(e == pl.num_programs(0) - 1)
    def _():
        for back in (2, 1):
            @pl.when(nused >= back)
            def _():
                y_copy(nused - back, lax.rem(nused - back, 2)).wait()

        ybuf[0] = jnp.zeros(ybuf.shape[1:], ybuf.dtype)
        n_blocks = ypad_ref.shape[0] // _SLAB
        lax.fori_loop(nused, n_blocks, lambda b, c: (y_copy(b, 0).start(), c)[1], 0)
        lax.fori_loop(nused, n_blocks, lambda b, c: (y_copy(b, 0).wait(), c)[1], 0)


def _experts(meta, xpad, wg, wu, wd):
    n_exp, d, de = wg.shape
    wmap = lambda e, m: (e, 0, 0)
    return pl.pallas_call(
        _expert_kernel,
        grid_spec=pltpu.PrefetchScalarGridSpec(
            num_scalar_prefetch=1,
            grid=(n_exp,),
            in_specs=[pl.BlockSpec(memory_space=pl.ANY),
                      pl.BlockSpec((None, d, de), wmap),
                      pl.BlockSpec((None, d, de), wmap),
                      pl.BlockSpec((None, de, d), wmap)],
            out_specs=pl.BlockSpec(memory_space=pl.ANY),
            scratch_shapes=[pltpu.VMEM((d, de), BF16),
                            pltpu.VMEM((d, de), BF16),
                            pltpu.VMEM((de, d), BF16),
                            pltpu.VMEM((2, _SLAB, LANES), BF16),
                            pltpu.VMEM((2, _SLAB, LANES), BF16),
                            pltpu.VMEM((_SLAB, LANES), F32),
                            pltpu.VMEM((_SLAB, LANES), F32),
                            pltpu.SemaphoreType.DMA((2,)),
                            pltpu.SemaphoreType.DMA((2,))]),
        out_shape=jax.ShapeDtypeStruct(xpad.shape, BF16),
        compiler_params=_cparams(1),
        name="experts",
    )(meta, xpad, wg, wu, wd)


def _combine_kernel(dc_ref, dn_ref, x2_ref, route_ref, g_ref, ypad_ref, out_ref, ybuf, sem,
                    stage0_s, stage1_s):
    i = pl.program_id(0)
    n = pl.num_programs(0)
    tc = x2_ref.shape[0]

    def row_copy(src, slot, which, t):
        ssl = pl.ds(pl.multiple_of(src * _PACK_ROWS, _PACK_ROWS), _PACK_ROWS)
        dsl = pl.ds(pl.multiple_of(t * _PACK_ROWS, _PACK_ROWS), _PACK_ROWS)
        return pltpu.make_async_copy(ypad_ref.at[ssl, :], ybuf.at[slot, which, dsl, :],
                                     sem.at[slot])

    def issue(d_ref, slot):
        def body(t, carry):
            row_copy(d_ref[0, t], slot, 0, t).start(priority=0)
            row_copy(d_ref[1, t], slot, 1, t).start(priority=1)
            return carry
        lax.fori_loop(0, tc, body, 0, unroll=8)

    slot = i % 2

    @pl.when(i == 0)
    def _():
        issue(dc_ref, 0)

    @pl.when(i + 1 < n)
    def _():
        issue(dn_ref, 1 - slot)

    def wait(t, carry):
        row_copy(0, slot, 0, 0).wait()
        row_copy(0, slot, 1, 0).wait()
        return carry
    lax.fori_loop(0, tc, wait, 0, unroll=8)

    route = route_ref[...]
    p0 = route[:, _R_P0:_R_P0 + 1]
    p1 = route[:, _R_P1:_R_P1 + 1]
    y0 = _load_slabs(ybuf.at[slot, 0], tc, stage0_s)
    y1 = _load_slabs(ybuf.at[slot, 1], tc, stage1_s)
    x3 = x2_ref[...] + p0 * y0 + p1 * y1
    out_ref[...] = _rms(x3, g_ref[...])


def _combine(dest, x2, route, g, ypad):
    t, d = x2.shape
    tc = TC
    nt = t // tc
    cur = pl.BlockSpec((8, tc), lambda i: (0, i), memory_space=pltpu.SMEM)
    nxt = pl.BlockSpec((8, tc), lambda i: (0, jnp.minimum(i + 1, nt - 1)),
                       memory_space=pltpu.SMEM)
    return pl.pallas_call(
        _combine_kernel,
        grid=(nt,),
        in_specs=[cur, nxt,
                  pl.BlockSpec((tc, d), lambda i: (i, 0)),
                  pl.BlockSpec((tc, LANES), lambda i: (i, 0)),
                  pl.BlockSpec((1, d), lambda i: (0, 0)),
                  pl.BlockSpec(memory_space=pl.ANY)],
        out_specs=pl.BlockSpec((tc, d), lambda i: (i, 0)),
        out_shape=jax.ShapeDtypeStruct((t, d), F32),
        scratch_shapes=[pltpu.VMEM((2, 2, tc * _PACK_ROWS, LANES), BF16),
                        pltpu.SemaphoreType.DMA((2,)),
                        pltpu.VMEM((tc * _PACK_ROWS, LANES), F32),
                        pltpu.VMEM((tc * _PACK_ROWS, LANES), F32)],
        compiler_params=_cparams(1),
        name="combine",
    )(dest, dest, x2, route, g, ypad)


def _pad_cols(a, width):
    return jnp.pad(a, ((0, 0), (0, width - a.shape[1])))


def kernel(x, mem, norm_mix, w_in, b_gates, conv_w, conv_b, att_sinks, norm_att_out, norm_ml_out,
           w_out, norm_cross, norm_mem, w_cq, w_ckv, w_co, norm_ffn, w_router_group,
           b_router_group, w_router_expert, b_router_expert, w_e_gate, w_e_up, w_e_down,
           norm_final):
    bsz, seq, d = x.shape
    t = bsz * seq
    depth = w_in.shape[0]
    assert depth == 1, "the final RMSNorm is fused into the last layer's combine kernel"
    xs = x.reshape(t, d)
    mem2d = mem.reshape(-1, d)

    for l in range(depth):
        n_main = _SEG["om"][1]
        wl = w_in[l]
        w_packed = jnp.concatenate(
            [wl[:, :n_main], _pad_cols(wl[:, n_main:n_main + ML_HEADS], LANES),
             _pad_cols(wl[:, n_main + ML_HEADS:], LANES)], axis=1).astype(BF16)
        bi = _pad_cols(b_gates[l][None, :ML_HEADS], LANES)
        bf = _pad_cols(b_gates[l][None, ML_HEADS:], LANES)
        w_r = _pad_cols(jnp.concatenate([w_router_group[l], w_router_expert[l]], axis=1), LANES)
        w_r_hi = w_r.astype(BF16)
        w_r_lo = (w_r - w_r_hi.astype(F32)).astype(BF16)
        w_r_packed = jnp.concatenate([w_r_hi, w_r_lo], axis=1)
        b_r = _pad_cols(jnp.concatenate([b_router_group[l], b_router_expert[l]])[None, :], LANES)

        qa, ka, va, qk, vm, og, gi, gf = _in_proj(xs, norm_mix[l][None, :], w_packed)
        att = _swa(qa, ka, va, att_sinks[l], norm_att_out[l][None, :], bsz, seq)
        hm = _mlstm(qk, vm, og, gi, gf, conv_w[l], conv_b[l][None, :], bi, bf,
                    norm_ml_out[l][None, :], bsz, seq)

        kmem, vmem_ = _kv_prep(mem2d, norm_mem[l][None, :], w_ckv[l].astype(BF16))
        x2, xn, route, idx, cnt = _mid(xs, att, hm, w_out[l].astype(BF16), norm_cross[l][None, :],
                                       w_cq[l].astype(BF16), kmem, vmem_, w_co[l].astype(BF16),
                                       norm_ffn[l][None, :], w_r_packed, b_r, seq)

        nbk = (t * TOP_K) // BM + N_EXPERTS
        assert nbk <= _META_LANES
        dest, meta = _plan(cnt, idx)
        xpad = _dispatch(meta, dest, xn, nbk * BM)
        ypad = _experts(meta, xpad, w_e_gate[l], w_e_up[l], w_e_down[l])
        xs = _combine(dest, x2, route, norm_final[None, :], ypad)
    return xs.reshape(bsz, seq, d)
```

```python
import functools
import math

import jax
import jax.numpy as jnp
from jax import lax
from jax.experimental import pallas as pl
from jax.experimental.pallas import tpu as pltpu

F32 = jnp.float32
BF16 = jnp.bfloat16
EPS = 1e-6
NEG_INF = float("-inf")

ATT_HEADS = 8
ATT_HEAD_DIM = 64
ATT_Q = ATT_HEADS * ATT_HEAD_DIM
ATT_KV = 128
WINDOW = 128
ML_HEADS = 4
ML_HEAD_DIM = 128
ML_W = ML_HEADS * ML_HEAD_DIM
CONV_WIDTH = 4
X_HEADS = 4
X_HEAD_DIM = 128
N_GROUPS = 4
EXPERTS_PER_GROUP = 8
N_EXPERTS = N_GROUPS * EXPERTS_PER_GROUP
TOP_K = 2

LANES = 128
TM_IN = 512
IN_PARTS = 2
SWA_QB = 2
ML_CHUNK = 256
ML_NB = 1
TM_MID = 1024
MID_PARTS = 4
TD = 512
TC = 256
BM = 256
EXPERT_AHEAD = 3
VMEM_LIMIT = 56 * 1024 * 1024

_NT = (((1,), (1,)), ((), ()))
_TN = (((0,), (0,)), ((), ()))


def _rms(x, g):
    return x * lax.rsqrt(jnp.mean(x * x, axis=-1, keepdims=True) + EPS) * g


def _cparams(n_axes):
    return pltpu.CompilerParams(dimension_semantics=("arbitrary",) * n_axes,
                                vmem_limit_bytes=VMEM_LIMIT)


_SEG = {"qa": (0, 512), "ka": (512, 640), "va": (640, 768), "qk": (768, 1792),
        "vm": (1792, 2304), "om": (2304, 2816), "gi": (2816, 2944), "gf": (2944, 3072)}


def _in_proj_kernel(x_ref, g_ref, w_ref, qa_ref, ka_ref, va_ref, qk_ref, vm_ref, og_ref,
                    gi_ref, gf_ref):
    tm = x_ref.shape[0] // IN_PARTS
    rows = [pl.ds(p * tm, tm) for p in range(IN_PARTS)]
    hb = [_rms(x_ref[r, :], g_ref[...]).astype(BF16) for r in rows]

    def seg(p, name):
        lo, hi = _SEG[name]
        return jnp.dot(hb[p], w_ref[:, lo:hi], preferred_element_type=F32)

    plain = {"qa": qa_ref, "ka": ka_ref, "va": va_ref, "qk": qk_ref, "vm": vm_ref, "gi": gi_ref,
             "gf": gf_ref}
    for p, r in enumerate(rows):
        for name, ref in plain.items():
            ref[r, :] = seg(p, name).astype(ref.dtype)
        og_ref[r, :] = jax.nn.sigmoid(seg(p, "om")).astype(og_ref.dtype)


def _in_proj(x2d, g, w_packed):
    t, d = x2d.shape
    tm = TM_IN
    widths = [(512, BF16), (128, BF16), (128, BF16), (1024, F32), (512, BF16), (512, BF16),
              (128, F32), (128, F32)]
    const = lambda i: (0, 0)
    return pl.pallas_call(
        _in_proj_kernel,
        grid=(t // tm,),
        in_specs=[pl.BlockSpec((tm, d), lambda i: (i, 0)),
                  pl.BlockSpec((1, d), const),
                  pl.BlockSpec(w_packed.shape, const)],
        out_specs=[pl.BlockSpec((tm, w), lambda i: (i, 0)) for w, _ in widths],
        out_shape=[jax.ShapeDtypeStruct((t, w), dt) for w, dt in widths],
        compiler_params=_cparams(1),
        name="in_proj",
    )(x2d, g, w_packed)


def _swa_kernel(sink_ref, q_ref, kp_ref, kc_ref, vp_ref, vc_ref, gn_ref, o_ref, bias_s):
    b = pl.program_id(0)
    n = pl.program_id(1)

    @pl.when((b == 0) & (n == 0))
    def _():
        ti = lax.broadcasted_iota(jnp.int32, (WINDOW, 2 * WINDOW), 0)
        si = lax.broadcasted_iota(jnp.int32, (WINDOW, 2 * WINDOW), 1)
        dist = ti + WINDOW - si
        band = (dist >= 0) & (dist < WINDOW)
        distf = dist.astype(F32)
        for hidx in range(ATT_HEADS):
            slope = 2.0 ** (-8.0 * (hidx + 1) / ATT_HEADS)
            full = jnp.where(band, -slope * distf, NEG_INF)
            bias_s[1, hidx] = full
            bias_s[0, hidx] = jnp.where(si >= WINDOW, full, NEG_INF)

    scale = 1.0 / math.sqrt(ATT_HEAD_DIM)
    k_all = jnp.concatenate([kp_ref[...], kc_ref[...]], axis=0).astype(F32) * scale
    v_all = jnp.concatenate([vp_ref[...], vc_ref[...]], axis=0).astype(F32)
    lane = lax.broadcasted_iota(jnp.int32, k_all.shape, 1)
    low = lane < ATT_HEAD_DIM

    def lane_variants(a):
        ar = pltpu.roll(a, ATT_HEAD_DIM, axis=1)
        z = jnp.zeros_like(a)
        return ((jnp.where(low, a, z).astype(BF16), jnp.where(low, z, ar).astype(BF16)),
                (jnp.where(low, ar, z).astype(BF16), jnp.where(low, z, a).astype(BF16)))

    k_ops = lane_variants(k_all)
    v_ops = lane_variants(v_all)
    n_tiles = ATT_Q // LANES
    kv_of = lambda j: (2 * j) // (ATT_HEADS // 2)
    blocks = range(SWA_QB)
    rows = [slice(WINDOW * i, WINDOW * (i + 1)) for i in blocks]
    keys = [slice(WINDOW * i, WINDOW * (i + 2)) for i in blocks]
    bias_slot = [jnp.where(n == 0, 0, 1) if i == 0 else 1 for i in blocks]

    scores = [[[lax.dot_general(q_ref[rows[i], LANES * j:LANES * (j + 1)],
                                k_ops[kv_of(j)][par][keys[i]], _NT, preferred_element_type=F32)
                for par in range(2)] for j in range(n_tiles)] for i in blocks]
    probs = [[[None, None] for _ in range(n_tiles)] for _ in blocks]
    rinv = [[[None, None] for _ in range(n_tiles)] for _ in blocks]
    for i in blocks:
        for j in range(n_tiles):
            for par in range(2):
                hidx = 2 * j + par
                logits = scores[i][j][par] + bias_s[bias_slot[i], hidx]
                sink = sink_ref[hidx]
                mx = jnp.maximum(jnp.max(logits, axis=-1, keepdims=True), sink)
                e = jnp.exp(logits - mx)
                den = jnp.sum(e, axis=-1, keepdims=True) + jnp.exp(sink - mx)
                probs[i][j][par] = e.astype(BF16)
                rinv[i][j][par] = 1.0 / den
    lane_o = lax.broadcasted_iota(jnp.int32, (WINDOW, LANES), 1)
    for i in blocks:
        tiles = []
        for j in range(n_tiles):
            acc = (jnp.dot(probs[i][j][0], v_ops[kv_of(j)][0][keys[i]], preferred_element_type=F32)
                   + jnp.dot(probs[i][j][1], v_ops[kv_of(j)][1][keys[i]],
                             preferred_element_type=F32))
            tiles.append(acc * jnp.where(lane_o < ATT_HEAD_DIM, rinv[i][j][0], rinv[i][j][1]))
        att = jnp.concatenate(tiles, axis=1)
        o_ref[rows[i], :] = _rms(att, gn_ref[...]).astype(BF16)


def _swa(qa, ka, va, sinks, gn, bsz, seq):
    nb = seq // WINDOW
    ns = nb // SWA_QB
    t = bsz * seq
    cur = lambda b, n: (b * ns + n, 0)
    prev = lambda b, n: (b * nb + jnp.maximum(n * SWA_QB - 1, 0), 0)
    return pl.pallas_call(
        _swa_kernel,
        grid=(bsz, ns),
        in_specs=[pl.BlockSpec(memory_space=pltpu.SMEM),
                  pl.BlockSpec((SWA_QB * WINDOW, ATT_Q), cur),
                  pl.BlockSpec((WINDOW, ATT_KV), prev),
                  pl.BlockSpec((SWA_QB * WINDOW, ATT_KV), cur),
                  pl.BlockSpec((WINDOW, ATT_KV), prev),
                  pl.BlockSpec((SWA_QB * WINDOW, ATT_KV), cur),
                  pl.BlockSpec((1, ATT_Q), lambda b, n: (0, 0))],
        out_specs=pl.BlockSpec((SWA_QB * WINDOW, ATT_Q), cur),
        out_shape=jax.ShapeDtypeStruct((t, ATT_Q), BF16),
        scratch_shapes=[pltpu.VMEM((2, ATT_HEADS, WINDOW, 2 * WINDOW), F32)],
        compiler_params=_cparams(2),
        name="swa",
    )(sinks, qa, ka, ka, va, va, gn)


def _split3(a):
    hi = a.astype(BF16)
    r1 = a - hi.astype(F32)
    mid = r1.astype(BF16)
    lo = (r1 - mid.astype(F32)).astype(BF16)
    return hi, mid, lo


def _mlstm_kernel(qk_ref, v_ref, og_ref, gi_ref, gf_ref, cw_ref, cb_ref, bi_ref, bf_ref, gn_ref,
                  out_ref, xbuf, cn_s, m_s):
    c = pl.program_id(1)
    nseq, L = qk_ref.shape[0], qk_ref.shape[1]
    HD = ML_HEAD_DIM
    seqs = range(nseq)
    heads = range(ML_HEADS)
    pairs = [(s, h) for s in seqs for h in heads]
    sls = [slice(HD * h, HD * (h + 1)) for h in heads]

    @pl.when(c == 0)
    def _():
        xbuf[:, 0:8, :] = jnp.zeros((nseq, 8, xbuf.shape[2]), F32)
        cn_s[...] = jnp.zeros(cn_s.shape, F32)
        m_s[...] = jnp.zeros(m_s.shape, F32)

    ti = lax.broadcasted_iota(jnp.int32, (L, L), 0)
    si = lax.broadcasted_iota(jnp.int32, (L, L), 1)
    causal = si <= ti
    tril = jnp.where(causal, 1.0, 0.0).astype(BF16)
    lane_l = lax.broadcasted_iota(jnp.int32, (L, HD), 1)
    ones_col = jnp.where(lane_l == 0, 1.0, 0.0).astype(BF16)

    gi, b = [], []
    for s in seqs:
        gi.append(gi_ref[s] + bi_ref[...])
        z = gf_ref[s] + bf_ref[...]
        lf = jnp.minimum(z, 0.0) - jnp.log1p(jnp.exp(-jnp.abs(z)))
        acc = jnp.zeros((L, LANES), F32)
        for part in _split3(lf):
            acc = acc + jnp.dot(tril, part, preferred_element_type=F32)
        b.append(acc)

    qb, kf = {}, {}
    for s in seqs:
        xbuf[s, 8:8 + L, :] = qk_ref[s]
        y = cb_ref[...] + cw_ref[CONV_WIDTH - 1:CONV_WIDTH, :] * xbuf[s, 8:8 + L, :]
        for j in range(CONV_WIDTH - 1):
            sh = CONV_WIDTH - 1 - j
            y = y + cw_ref[j:j + 1, :] * xbuf[s, 8 - sh:8 - sh + L, :]
        xbuf[s, 0:8, :] = xbuf[s, L:L + 8, :]
        qk = y * jax.nn.sigmoid(y)
        for h in heads:
            qb[s, h] = qk[:, sls[h]].astype(BF16)
            kf[s, h] = qk[:, ML_W + HD * h:ML_W + HD * (h + 1)] * (1.0 / math.sqrt(HD))
    kb = {p: kf[p].astype(BF16) for p in pairs}
    vext = {(s, h): jnp.concatenate([v_ref[s, :, sls[h]], ones_col], axis=1) for s, h in pairs}
    cn = {(s, h): cn_s[s * ML_HEADS + h] for s, h in pairs}
    qk_d = {p: lax.dot_general(qb[p], kb[p], _NT, preferred_element_type=F32) for p in pairs}
    qc = {p: jnp.dot(qb[p], cn[p].astype(BF16), preferred_element_type=F32) for p in pairs}

    m_inter, m_new, w, decay, rt = [], [], [], [], []
    for s in seqs:
        m_st = m_s[s]
        b_last = b[s][L - 1:L, :]
        m_inter.append(b[s] + m_st)
        log_w = b_last - b[s] + gi[s]
        m_new.append(jnp.maximum(b_last + m_st, jnp.max(log_w, axis=0, keepdims=True)))
        w.append(jnp.exp(log_w - m_new[s]))
        decay.append(jnp.exp(b_last + m_st - m_new[s]))
        rt.append(jnp.transpose(gi[s] - b[s]))

    s_b, kw_b, m_ts, a_inters = {}, {}, {}, {}
    for s, h in pairs:
        log_d = jnp.where(causal, b[s][:, h:h + 1] + rt[s][h:h + 1, :], NEG_INF)
        m_t = jnp.maximum(m_inter[s][:, h:h + 1], jnp.max(log_d, axis=1, keepdims=True))
        s_b[s, h] = (qk_d[s, h] * jnp.exp(log_d - m_t)).astype(BF16)
        kw_b[s, h] = (kf[s, h] * w[s][:, h:h + 1]).astype(BF16)
        m_ts[s, h] = m_t
        a_inters[s, h] = jnp.exp(m_inter[s][:, h:h + 1] - m_t)

    sv = {p: jnp.dot(s_b[p], vext[p], preferred_element_type=F32) for p in pairs}
    upd = {p: lax.dot_general(kw_b[p], vext[p], _TN, preferred_element_type=F32) for p in pairs}

    for s, h in pairs:
        nd = sv[s, h] + a_inters[s, h] * qc[s, h]
        num = nd[:, :HD]
        den = nd[:, HD:HD + 1]
        hh = num * (1.0 / jnp.maximum(jnp.abs(den), jnp.exp(-m_ts[s, h])))
        cn_s[s * ML_HEADS + h] = decay[s][:, h:h + 1] * cn[s, h] + upd[s, h]
        hm = og_ref[s, :, sls[h]].astype(F32) * hh
        out_ref[s, :, sls[h]] = _rms(hm, gn_ref[:, sls[h]]).astype(BF16)

    for s in seqs:
        m_s[s] = m_new[s]


def _mlstm(qk, vm, og, gi, gf, cw, cb, bi, bf, gn, bsz, seq):
    L = ML_CHUNK
    nc = seq // L
    nseq = ML_NB
    t = bsz * seq
    seq3 = lambda a: a.reshape(bsz, seq, a.shape[-1])
    row = lambda b, c: (b, c, 0)
    const = lambda b, c: (0, 0)
    out = pl.pallas_call(
        _mlstm_kernel,
        grid=(bsz // nseq, nc),
        in_specs=[pl.BlockSpec((nseq, L, 2 * ML_W), row),
                  pl.BlockSpec((nseq, L, ML_W), row),
                  pl.BlockSpec((nseq, L, ML_W), row),
                  pl.BlockSpec((nseq, L, LANES), row),
                  pl.BlockSpec((nseq, L, LANES), row),
                  pl.BlockSpec((CONV_WIDTH, 2 * ML_W), const),
                  pl.BlockSpec((1, 2 * ML_W), const),
                  pl.BlockSpec((1, LANES), const),
                  pl.BlockSpec((1, LANES), const),
                  pl.BlockSpec((1, ML_W), const)],
        out_specs=pl.BlockSpec((nseq, L, ML_W), row),
        out_shape=jax.ShapeDtypeStruct((bsz, seq, ML_W), BF16),
        scratch_shapes=[pltpu.VMEM((nseq, L + 8, 2 * ML_W), F32),
                        pltpu.VMEM((nseq * ML_HEADS, ML_HEAD_DIM, 2 * ML_HEAD_DIM), F32),
                        pltpu.VMEM((nseq, 1, LANES), F32)],
        compiler_params=_cparams(2),
        name="mlstm",
    )(seq3(qk), seq3(vm), seq3(og), seq3(gi), seq3(gf), cw, cb, bi, bf, gn)
    return out.reshape(t, ML_W)


def _kv_kernel(mem_ref, g_ref, w_ref, k_ref, v_ref):
    mn = _rms(mem_ref[...], g_ref[...]).astype(BF16)
    hw = k_ref.shape[1]
    k_ref[...] = jnp.dot(mn, w_ref[:, :hw], preferred_element_type=F32).astype(BF16)
    v_ref[...] = jnp.dot(mn, w_ref[:, hw:], preferred_element_type=F32).astype(BF16)


def _kv_prep(mem2d, g, w_ckv):
    r, d = mem2d.shape
    hw = w_ckv.shape[1] // 2
    tm = 256
    return pl.pallas_call(
        _kv_kernel,
        grid=(r // tm,),
        in_specs=[pl.BlockSpec((tm, d), lambda i: (i, 0)),
                  pl.BlockSpec((1, d), lambda i: (0, 0)),
                  pl.BlockSpec(w_ckv.shape, lambda i: (0, 0))],
        out_specs=[pl.BlockSpec((tm, hw), lambda i: (i, 0))] * 2,
        out_shape=[jax.ShapeDtypeStruct((r, hw), BF16)] * 2,
        compiler_params=_cparams(1),
        name="kv_prep",
    )(mem2d, g, w_ckv)


_R_E0, _R_E1, _R_RANK0, _R_RANK1, _R_P0, _R_P1 = range(6)
_ROUTER_LANE0 = N_GROUPS


_PACK_ROWS = 8


def _store_slabs(ref, v, rows, stage, first=0):
    base = first * _PACK_ROWS
    for r in range(_PACK_ROWS):
        stage[pl.ds(base + r, rows, stride=_PACK_ROWS), :] = v[:, LANES * r:LANES * (r + 1)]
    span = pl.ds(base, rows * _PACK_ROWS)
    ref[span, :] = stage[span, :].astype(ref.dtype)


def _load_slabs(ref, rows, stage):
    stage[...] = ref[...].astype(F32)
    return jnp.concatenate(
        [stage[pl.ds(r, rows, stride=_PACK_ROWS), :] for r in range(_PACK_ROWS)], axis=1)


def _mid_kernel(x_ref, att_ref, hm_ref, wo_ref, gc_ref, wq_ref, k_ref, v_ref, wco_ref, gf_ref,
                wr_ref, br_ref, x2_ref, xn_ref, route_ref, idx_ref, cnt_ref, carry_s, stage_s):
    i = pl.program_id(0)
    tm = x_ref.shape[0] // MID_PARTS
    parts = range(MID_PARTS)
    rows = [pl.ds(p * tm, tm) for p in parts]

    @pl.when(i == 0)
    def _():
        carry_s[...] = jnp.zeros(carry_s.shape, F32)

    x1 = [x_ref[rows[p], :]
          + jnp.dot(att_ref[rows[p], :], wo_ref[0:ATT_Q, :], preferred_element_type=F32)
          + jnp.dot(hm_ref[rows[p], :], wo_ref[ATT_Q:, :], preferred_element_type=F32)
          for p in parts]

    xc = [_rms(x1[p], gc_ref[...]).astype(BF16) for p in parts]
    qb = [jnp.dot(xc[p], wq_ref[...], preferred_element_type=F32).astype(BF16) for p in parts]
    sls = [slice(X_HEAD_DIM * h, X_HEAD_DIM * (h + 1)) for h in range(X_HEADS)]
    sc = [[lax.dot_general(qb[p][:, sl], k_ref[:, sl], _NT, preferred_element_type=F32)
           for sl in sls] for p in parts]
    es = [[None] * X_HEADS for _ in parts]
    rinv = [[None] * X_HEADS for _ in parts]
    for p in parts:
        for h in range(X_HEADS):
            s = sc[p][h] * (1.0 / math.sqrt(X_HEAD_DIM))
            e = jnp.exp(s - jnp.max(s, axis=-1, keepdims=True))
            es[p][h] = e.astype(BF16)
            rinv[p][h] = 1.0 / jnp.sum(e, axis=-1, keepdims=True)
    o = [jnp.concatenate(
        [jnp.dot(es[p][h], v_ref[:, sls[h]], preferred_element_type=F32) * rinv[p][h]
         for h in range(X_HEADS)], axis=1).astype(BF16) for p in parts]
    x2 = [x1[p] + jnp.dot(o[p], wco_ref[...], preferred_element_type=F32) for p in parts]

    logits_p = []
    for p in parts:
        x2_ref[rows[p], :] = x2[p]
        xn = _rms(x2[p], gf_ref[...])
        xh = xn.astype(BF16)
        _store_slabs(xn_ref, xn, tm, stage_s, first=p * tm)
        xl = (xn - xh.astype(F32)).astype(BF16)
        lg2 = jnp.dot(xh, wr_ref[...], preferred_element_type=F32)
        logits_p.append(lg2[:, :LANES] + lg2[:, LANES:]
                        + jnp.dot(xl, wr_ref[:, :LANES], preferred_element_type=F32) + br_ref[...])

    for p in parts:
        _route_part(logits_p[p], rows[p], p * tm, route_ref, idx_ref, cnt_ref, carry_s)


def _route_part(logits, rows, first, route_ref, idx_ref, cnt_ref, carry_s):
    tm = logits.shape[0]
    lane = lax.broadcasted_iota(jnp.int32, (tm, LANES), 1)
    lanef = lane.astype(F32)
    big = float(4 * LANES)
    gl = jnp.where(lane < N_GROUPS, logits, NEG_INF)
    gmax = jnp.max(gl, axis=-1, keepdims=True)
    gsel = jnp.min(jnp.where(gl == gmax, lanef, big), axis=-1, keepdims=True)
    gw = 1.0 / jnp.sum(jnp.exp(gl - gmax), axis=-1, keepdims=True)
    lo_lane = _ROUTER_LANE0 + EXPERTS_PER_GROUP * gsel
    in_group = (lanef >= lo_lane) & (lanef < lo_lane + EXPERTS_PER_GROUP)
    el = jnp.where(in_group, logits, NEG_INF)
    v0 = jnp.max(el, axis=-1, keepdims=True)
    i0 = jnp.min(jnp.where(el == v0, lanef, big), axis=-1, keepdims=True)
    el2 = jnp.where(lanef == i0, NEG_INF, el)
    v1 = jnp.max(el2, axis=-1, keepdims=True)
    i1 = jnp.min(jnp.where(el2 == v1, lanef, big), axis=-1, keepdims=True)
    tt = jnp.exp(v1 - v0)
    p0 = gw / (1.0 + tt)
    p1 = gw * tt / (1.0 + tt)

    sel0 = lanef == i0
    sel1 = lanef == i1
    mb = jnp.where(sel0 | sel1, 1.0, 0.0).astype(BF16)
    ti = lax.broadcasted_iota(jnp.int32, (tm, tm), 0)
    si = lax.broadcasted_iota(jnp.int32, (tm, tm), 1)
    strict = jnp.where(si < ti, 1.0, 0.0).astype(BF16)
    carry = carry_s[...]
    pref = jnp.dot(strict, mb, preferred_element_type=F32) + carry
    r0 = jnp.sum(jnp.where(sel0, pref, 0.0), axis=-1, keepdims=True)
    r1 = jnp.sum(jnp.where(sel1, pref, 0.0), axis=-1, keepdims=True)
    carry = carry + jnp.sum(mb.astype(F32), axis=0, keepdims=True)
    carry_s[...] = carry
    cnt_ref[...] = carry

    route = jnp.zeros((tm, LANES), F32)
    for idx, col in ((_R_E0, i0 - _ROUTER_LANE0), (_R_E1, i1 - _ROUTER_LANE0), (_R_RANK0, r0),
                     (_R_RANK1, r1), (_R_P0, p0), (_R_P1, p1)):
        route = jnp.where(lane == idx, col, route)
    route_ref[rows, :] = route
    idx_ref[:, pl.ds(first, tm)] = jnp.transpose(route)[0:8, :].astype(jnp.int32)


def _mid(x2d, att, hm, wo, gc, wq, kmem, vmem_, wco, gf, wr, br, seq):
    t, d = x2d.shape
    tm = TM_MID
    per_b = seq // tm
    rowmap = lambda i: (i, 0)
    const = lambda i: (0, 0)
    bmap = lambda i: (i // per_b, 0)
    mem_len = kmem.shape[0] // (t // seq)
    return pl.pallas_call(
        _mid_kernel,
        grid=(t // tm,),
        in_specs=[pl.BlockSpec((tm, d), rowmap),
                  pl.BlockSpec((tm, ATT_Q), rowmap),
                  pl.BlockSpec((tm, ML_W), rowmap),
                  pl.BlockSpec(wo.shape, const),
                  pl.BlockSpec((1, d), const),
                  pl.BlockSpec(wq.shape, const),
                  pl.BlockSpec((mem_len, kmem.shape[1]), bmap),
                  pl.BlockSpec((mem_len, vmem_.shape[1]), bmap),
                  pl.BlockSpec(wco.shape, const),
                  pl.BlockSpec((1, d), const),
                  pl.BlockSpec(wr.shape, const),
                  pl.BlockSpec((1, LANES), const)],
        out_specs=[pl.BlockSpec((tm, d), rowmap),
                   pl.BlockSpec((tm * _PACK_ROWS, LANES), rowmap),
                   pl.BlockSpec((tm, LANES), rowmap),
                   pl.BlockSpec((8, tm), lambda i: (0, i)),
                   pl.BlockSpec((1, LANES), const)],
        out_shape=[jax.ShapeDtypeStruct((t, d), F32),
                   jax.ShapeDtypeStruct((t * _PACK_ROWS, LANES), BF16),
                   jax.ShapeDtypeStruct((t, LANES), F32),
                   jax.ShapeDtypeStruct((8, t), jnp.int32),
                   jax.ShapeDtypeStruct((1, LANES), F32)],
        scratch_shapes=[pltpu.VMEM((1, LANES), F32),
                        pltpu.VMEM((tm * _PACK_ROWS, LANES), F32)],
        compiler_params=_cparams(1),
        name="mid",
    )(x2d, att, hm, wo, gc, wq, kmem, vmem_, wco, gf, wr, br)


_M_BLOCK_E, _M_PAD_END, _M_NUSED = range(3)
_META_LANES = 2 * LANES


def _plan_kernel(cnt_ref, idx_ref, dest_ref, meta_ref):
    cnt = cnt_ref[...]
    lane = lax.broadcasted_iota(jnp.int32, (1, LANES), 1)
    is_expert = (lane >= _ROUTER_LANE0) & (lane < _ROUTER_LANE0 + N_EXPERTS)
    nblk = jnp.where(is_expert, jnp.floor((cnt + (BM - 1)) * (1.0 / BM)), 0.0)
    jj = lax.broadcasted_iota(jnp.int32, (LANES, LANES), 0)
    kk = lax.broadcasted_iota(jnp.int32, (LANES, LANES), 1)
    upper = jnp.where(jj <= kk, 1.0, 0.0).astype(BF16)
    pend_blk = jnp.dot(jnp.broadcast_to(nblk, (8, LANES)).astype(BF16), upper,
                       preferred_element_type=F32)[0:1, :]
    pstart_rows = (pend_blk - nblk) * BM
    pend_rows = pend_blk * BM

    idx = idx_ref[...]
    off = jnp.zeros(idx.shape, F32)
    for e in range(N_EXPERTS):
        lane_e = _ROUTER_LANE0 + e
        off = jnp.where(idx == e, pstart_rows[:, lane_e:lane_e + 1], off)
    ranks = pltpu.roll(idx, idx.shape[0] - 2, axis=0)
    dest_ref[...] = ranks + off.astype(jnp.int32)

    blk = lax.broadcasted_iota(jnp.int32, (1, _META_LANES), 1).astype(F32)
    block_e = jnp.zeros((1, _META_LANES), F32)
    for e in range(N_EXPERTS):
        lane_e = _ROUTER_LANE0 + e
        block_e = block_e + jnp.where(pend_blk[:, lane_e:lane_e + 1] <= blk, 1.0, 0.0)
    block_e = jnp.minimum(block_e, N_EXPERTS - 1.0)
    last = _ROUTER_LANE0 + N_EXPERTS - 1
    nused = pend_blk[:, last:last + 1]
    pend_wide = jnp.concatenate([pend_rows, jnp.zeros((1, _META_LANES - LANES), F32)], axis=1)
    sub = lax.broadcasted_iota(jnp.int32, (8, _META_LANES), 0)
    meta = jnp.where(sub == _M_BLOCK_E, block_e, jnp.where(sub == _M_PAD_END, pend_wide, nused))
    meta_ref[...] = meta.astype(jnp.int32)


def _plan(cnt, idx):
    t = idx.shape[1]
    return pl.pallas_call(
        _plan_kernel,
        grid=(1,),
        in_specs=[pl.BlockSpec(cnt.shape, lambda i: (0, 0)),
                  pl.BlockSpec(idx.shape, lambda i: (0, 0))],
        out_specs=[pl.BlockSpec((8, t), lambda i: (0, 0)),
                   pl.BlockSpec((8, _META_LANES), lambda i: (0, 0))],
        out_shape=[jax.ShapeDtypeStruct((8, t), jnp.int32),
                   jax.ShapeDtypeStruct((8, _META_LANES), jnp.int32)],
        compiler_params=_cparams(1),
        name="plan",
    )(cnt, idx)


_SLAB = BM * _PACK_ROWS


def _dispatch_kernel(meta_ref, d_ref, xn_ref, xpad_ref, zbuf, sem, zsem):
    i = pl.program_id(0)
    td = xn_ref.shape[0] // _PACK_ROWS

    def zero_copy(e):
        pend = meta_ref[_M_PAD_END, _ROUTER_LANE0 + e]
        pstart = meta_ref[_M_PAD_END, _ROUTER_LANE0 + e - 1]
        first = pl.multiple_of((pend - BM) * _PACK_ROWS, _SLAB)
        cp = pltpu.make_async_copy(zbuf, xpad_ref.at[pl.ds(first, _SLAB), :], zsem)
        return pend > pstart, cp

    @pl.when(i == 0)
    def _():
        zbuf[...] = jnp.zeros(zbuf.shape, zbuf.dtype)
        for e in range(N_EXPERTS):
            nonempty, cp = zero_copy(e)
            pl.when(nonempty)(cp.start)
        for e in range(N_EXPERTS):
            nonempty, cp = zero_copy(e)
            pl.when(nonempty)(cp.wait)

        def tail_copy(b):
            return pltpu.make_async_copy(
                zbuf, xpad_ref.at[pl.ds(pl.multiple_of(b * _SLAB, _SLAB), _SLAB), :], zsem)

        first_unused = meta_ref[_M_NUSED, 0]
        n_blocks = xpad_ref.shape[0] // _SLAB
        lax.fori_loop(first_unused, n_blocks, lambda b, c: (tail_copy(b).start(), c)[1], 0)
        lax.fori_loop(first_unused, n_blocks, lambda b, c: (tail_copy(b).wait(), c)[1], 0)

    def row_copy(t, dst):
        src = pl.ds(pl.multiple_of(t * _PACK_ROWS, _PACK_ROWS), _PACK_ROWS)
        dsl = pl.ds(pl.multiple_of(dst * _PACK_ROWS, _PACK_ROWS), _PACK_ROWS)
        return pltpu.make_async_copy(xn_ref.at[src, :], xpad_ref.at[dsl, :], sem)

    def start(t, carry):
        row_copy(t, d_ref[0, t]).start(priority=0)
        row_copy(t, d_ref[1, t]).start(priority=1)
        return carry

    def wait(t, carry):
        row_copy(0, 0).wait()
        row_copy(0, 0).wait()
        return carry

    lax.fori_loop(0, td, start, 0, unroll=8)
    lax.fori_loop(0, td, wait, 0, unroll=8)


def _dispatch(meta, dest, xn_packed, rows):
    t = dest.shape[1]
    td = TD
    return pl.pallas_call(
        _dispatch_kernel,
        grid_spec=pltpu.PrefetchScalarGridSpec(
            num_scalar_prefetch=1,
            grid=(t // td,),
            in_specs=[pl.BlockSpec((8, td), lambda i, m: (0, i), memory_space=pltpu.SMEM),
                      pl.BlockSpec((td * _PACK_ROWS, LANES), lambda i, m: (i, 0))],
            out_specs=pl.BlockSpec(memory_space=pl.ANY),
            scratch_shapes=[pltpu.VMEM((_SLAB, LANES), BF16),
                            pltpu.SemaphoreType.DMA(()),
                            pltpu.SemaphoreType.DMA(())]),
        out_shape=jax.ShapeDtypeStruct((rows * _PACK_ROWS, LANES), BF16),
        compiler_params=_cparams(1),
        name="dispatch",
    )(meta, dest, xn_packed)


def _expert_kernel(meta_ref, xpad_ref, wg_ref, wu_ref, wd_ref, ypad_ref, wg_s, wu_s, wd_s,
                   xbuf, ybuf, xstage_s, ystage_s, xsem, ysem):
    e = pl.program_id(0)
    nused = meta_ref[_M_NUSED, 0]
    lane_e = _ROUTER_LANE0 + e
    first_blk = lax.div(meta_ref[_M_PAD_END, lane_e - 1], BM)
    n_blk = lax.div(meta_ref[_M_PAD_END, lane_e], BM) - first_blk

    def block_rows(g):
        return pl.ds(pl.multiple_of(g * _SLAB, _SLAB), _SLAB)

    def x_copy(g, slot):
        return pltpu.make_async_copy(xpad_ref.at[block_rows(g), :], xbuf.at[slot], xsem.at[slot])

    def y_copy(g, slot):
        return pltpu.make_async_copy(ybuf.at[slot], ypad_ref.at[block_rows(g), :], ysem.at[slot])

    nx = xbuf.shape[0]
    ahead = nx - 1

    @pl.when(e == 0)
    def _():
        for g0 in range(ahead):
            @pl.when(g0 < nused)
            def _():
                x_copy(g0, g0).start(priority=1)

    @pl.when(n_blk > 0)
    def _():
        wg_s[...] = wg_ref[...].astype(BF16)
        wu_s[...] = wu_ref[...].astype(BF16)
        wd_s[...] = wd_ref[...].astype(BF16)

    def body(j, carry):
        g = first_blk + j
        xslot = lax.rem(g, nx)
        yslot = lax.rem(g, 2)
        x_copy(g, xslot).wait()

        @pl.when(g + ahead < nused)
        def _():
            x_copy(g + ahead, lax.rem(g + ahead, nx)).start(priority=1)

        xb = _load_slabs(xbuf.at[xslot], BM, xstage_s).astype(BF16)
        gate = jnp.dot(xb, wg_s[...], preferred_element_type=F32)
        up = jnp.dot(xb, wu_s[...], preferred_element_type=F32)
        hb = (gate * jax.nn.sigmoid(gate) * up).astype(BF16)
        y = jnp.dot(hb, wd_s[...], preferred_element_type=F32)

        @pl.when(g >= 2)
        def _():
            y_copy(g - 2, yslot).wait()

        _store_slabs(ybuf.at[yslot], y, BM, ystage_s)
        y_copy(g, yslot).start()
        return carry

    lax.fori_loop(0, n_blk, body, 0)

    @pl.when(e == pl.num_programs(0) - 1)
    def _():
        for back in (2, 1):
            @pl.when(nused >= back)
            def _():
                y_copy(nused - back, lax.rem(nused - back, 2)).wait()

        ybuf[0] = jnp.zeros(ybuf.shape[1:], ybuf.dtype)
        n_blocks = ypad_ref.shape[0] // _SLAB
        lax.fori_loop(nused, n_blocks, lambda b, c: (y_copy(b, 0).start(), c)[1], 0)
        lax.fori_loop(nused, n_blocks, lambda b, c: (y_copy(b, 0).wait(), c)[1], 0)


def _experts(meta, xpad, wg, wu, wd):
    n_exp, d, de = wg.shape
    wmap = lambda e, m: (e, 0, 0)
    return pl.pallas_call(
        _expert_kernel,
        grid_spec=pltpu.PrefetchScalarGridSpec(
            num_scalar_prefetch=1,
            grid=(n_exp,),
            in_specs=[pl.BlockSpec(memory_space=pl.ANY),
                      pl.BlockSpec((None, d, de), wmap),
                      pl.BlockSpec((None, d, de), wmap),
                      pl.BlockSpec((None, de, d), wmap)],
            out_specs=pl.BlockSpec(memory_space=pl.ANY),
            scratch_shapes=[pltpu.VMEM((d, de), BF16),
                            pltpu.VMEM((d, de), BF16),
                            pltpu.VMEM((de, d), BF16),
                            pltpu.VMEM((EXPERT_AHEAD + 1, _SLAB, LANES), BF16),
                            pltpu.VMEM((2, _SLAB, LANES), BF16),
                            pltpu.VMEM((_SLAB, LANES), F32),
                            pltpu.VMEM((_SLAB, LANES), F32),
                            pltpu.SemaphoreType.DMA((EXPERT_AHEAD + 1,)),
                            pltpu.SemaphoreType.DMA((2,))]),
        out_shape=jax.ShapeDtypeStruct(xpad.shape, BF16),
        compiler_params=_cparams(1),
        name="experts",
    )(meta, xpad, wg, wu, wd)


def _combine_kernel(dc_ref, dn_ref, x2_ref, route_ref, g_ref, ypad_ref, out_ref, ybuf, sem,
                    stage0_s, stage1_s):
    i = pl.program_id(0)
    n = pl.num_programs(0)
    tc = x2_ref.shape[0]

    def row_copy(src, slot, which, t):
        ssl = pl.ds(pl.multiple_of(src * _PACK_ROWS, _PACK_ROWS), _PACK_ROWS)
        dsl = pl.ds(pl.multiple_of(t * _PACK_ROWS, _PACK_ROWS), _PACK_ROWS)
        return pltpu.make_async_copy(ypad_ref.at[ssl, :], ybuf.at[slot, which, dsl, :],
                                     sem.at[slot])

    def issue(d_ref, slot):
        def body(t, carry):
            row_copy(d_ref[0, t], slot, 0, t).start(priority=0)
            row_copy(d_ref[1, t], slot, 1, t).start(priority=1)
            return carry
        lax.fori_loop(0, tc, body, 0, unroll=8)

    slot = i % 2

    @pl.when(i == 0)
    def _():
        issue(dc_ref, 0)

    @pl.when(i + 1 < n)
    def _():
        issue(dn_ref, 1 - slot)

    def wait(t, carry):
        row_copy(0, slot, 0, 0).wait()
        row_copy(0, slot, 1, 0).wait()
        return carry
    lax.fori_loop(0, tc, wait, 0, unroll=8)

    route = route_ref[...]
    p0 = route[:, _R_P0:_R_P0 + 1]
    p1 = route[:, _R_P1:_R_P1 + 1]
    y0 = _load_slabs(ybuf.at[slot, 0], tc, stage0_s)
    y1 = _load_slabs(ybuf.at[slot, 1], tc, stage1_s)
    x3 = x2_ref[...] + p0 * y0 + p1 * y1
    out_ref[...] = _rms(x3, g_ref[...])


def _combine(dest, x2, route, g, ypad):
    t, d = x2.shape
    tc = TC
    nt = t // tc
    cur = pl.BlockSpec((8, tc), lambda i: (0, i), memory_space=pltpu.SMEM)
    nxt = pl.BlockSpec((8, tc), lambda i: (0, jnp.minimum(i + 1, nt - 1)),
                       memory_space=pltpu.SMEM)
    return pl.pallas_call(
        _combine_kernel,
        grid=(nt,),
        in_specs=[cur, nxt,
                  pl.BlockSpec((tc, d), lambda i: (i, 0)),
                  pl.BlockSpec((tc, LANES), lambda i: (i, 0)),
                  pl.BlockSpec((1, d), lambda i: (0, 0)),
                  pl.BlockSpec(memory_space=pl.ANY)],
        out_specs=pl.BlockSpec((tc, d), lambda i: (i, 0)),
        out_shape=jax.ShapeDtypeStruct((t, d), F32),
        scratch_shapes=[pltpu.VMEM((2, 2, tc * _PACK_ROWS, LANES), BF16),
                        pltpu.SemaphoreType.DMA((2,)),
                        pltpu.VMEM((tc * _PACK_ROWS, LANES), F32),
                        pltpu.VMEM((tc * _PACK_ROWS, LANES), F32)],
        compiler_params=_cparams(1),
        name="combine",
    )(dest, dest, x2, route, g, ypad)


def _pad_cols(a, width):
    return jnp.pad(a, ((0, 0), (0, width - a.shape[1])))


def kernel(x, mem, norm_mix, w_in, b_gates, conv_w, conv_b, att_sinks, norm_att_out, norm_ml_out,
           w_out, norm_cross, norm_mem, w_cq, w_ckv, w_co, norm_ffn, w_router_group,
           b_router_group, w_router_expert, b_router_expert, w_e_gate, w_e_up, w_e_down,
           norm_final):
    bsz, seq, d = x.shape
    t = bsz * seq
    depth = w_in.shape[0]
    assert depth == 1, "the final RMSNorm is fused into the last layer's combine kernel"
    xs = x.reshape(t, d)
    mem2d = mem.reshape(-1, d)

    for l in range(depth):
        n_main = _SEG["om"][1]
        wl = w_in[l]
        w_packed = jnp.concatenate(
            [wl[:, :n_main], _pad_cols(wl[:, n_main:n_main + ML_HEADS], LANES),
             _pad_cols(wl[:, n_main + ML_HEADS:], LANES)], axis=1).astype(BF16)
        bi = _pad_cols(b_gates[l][None, :ML_HEADS], LANES)
        bf = _pad_cols(b_gates[l][None, ML_HEADS:], LANES)
        w_r = _pad_cols(jnp.concatenate([w_router_group[l], w_router_expert[l]], axis=1), LANES)
        w_r_hi = w_r.astype(BF16)
        w_r_lo = (w_r - w_r_hi.astype(F32)).astype(BF16)
        w_r_packed = jnp.concatenate([w_r_hi, w_r_lo], axis=1)
        b_r = _pad_cols(jnp.concatenate([b_router_group[l], b_router_expert[l]])[None, :], LANES)

        qa, ka, va, qk, vm, og, gi, gf = _in_proj(xs, norm_mix[l][None, :], w_packed)
        att = _swa(qa, ka, va, att_sinks[l], norm_att_out[l][None, :], bsz, seq)
        hm = _mlstm(qk, vm, og, gi, gf, conv_w[l], conv_b[l][None, :], bi, bf,
                    norm_ml_out[l][None, :], bsz, seq)

        kmem, vmem_ = _kv_prep(mem2d, norm_mem[l][None, :], w_ckv[l].astype(BF16))
        x2, xn, route, idx, cnt = _mid(xs, att, hm, w_out[l].astype(BF16), norm_cross[l][None, :],
                                       w_cq[l].astype(BF16), kmem, vmem_, w_co[l].astype(BF16),
                                       norm_ffn[l][None, :], w_r_packed, b_r, seq)

        nbk = (t * TOP_K) // BM + N_EXPERTS
        assert nbk <= _META_LANES
        dest, meta = _plan(cnt, idx)
        xpad = _dispatch(meta, dest, xn, nbk * BM)
        ypad = _experts(meta, xpad, w_e_gate[l], w_e_up[l], w_e_down[l])
        xs = _combine(dest, x2, route, norm_final[None, :], ypad)
    return xs.reshape(bsz, seq, d)
```

```python
import functools
import math

import jax
import jax.numpy as jnp
from jax import lax
from jax.experimental import pallas as pl
from jax.experimental.pallas import tpu as pltpu

F32 = jnp.float32
BF16 = jnp.bfloat16
EPS = 1e-6
NEG_INF = float("-inf")

ATT_HEADS = 8
ATT_HEAD_DIM = 64
ATT_Q = ATT_HEADS * ATT_HEAD_DIM
ATT_KV = 128
WINDOW = 128
ML_HEADS = 4
ML_HEAD_DIM = 128
ML_W = ML_HEADS * ML_HEAD_DIM
CONV_WIDTH = 4
X_HEADS = 4
X_HEAD_DIM = 128
N_GROUPS = 4
EXPERTS_PER_GROUP = 8
N_EXPERTS = N_GROUPS * EXPERTS_PER_GROUP
TOP_K = 2

LANES = 128
TM_IN = 512
IN_PARTS = 2
SWA_QB = 2
ML_CHUNK = 256
ML_NB = 1
TM_MID = 1024
MID_PARTS = 4
TD = 512
TC = 256
BM = 256
EXPERT_AHEAD = 3
VMEM_LIMIT = 56 * 1024 * 1024

_NT = (((1,), (1,)), ((), ()))
_TN = (((0,), (0,)), ((), ()))


def _rms(x, g):
    return x * lax.rsqrt(jnp.mean(x * x, axis=-1, keepdims=True) + EPS) * g


def _cparams(n_axes):
    return pltpu.CompilerParams(dimension_semantics=("arbitrary",) * n_axes,
                                vmem_limit_bytes=VMEM_LIMIT)


_SEG = {"qa": (0, 512), "ka": (512, 640), "va": (640, 768), "qk": (768, 1792),
        "vm": (1792, 2304), "om": (2304, 2816), "gi": (2816, 2944), "gf": (2944, 3072)}


def _in_proj_kernel(x_ref, g_ref, w_ref, qa_ref, ka_ref, va_ref, qk_ref, vm_ref, og_ref,
                    gi_ref, gf_ref):
    tm = x_ref.shape[0] // IN_PARTS
    rows = [pl.ds(p * tm, tm) for p in range(IN_PARTS)]
    hb = [_rms(x_ref[r, :], g_ref[...]).astype(BF16) for r in rows]

    def seg(p, name):
        lo, hi = _SEG[name]
        return jnp.dot(hb[p], w_ref[:, lo:hi], preferred_element_type=F32)

    plain = {"qa": qa_ref, "ka": ka_ref, "va": va_ref, "qk": qk_ref, "vm": vm_ref, "gi": gi_ref,
             "gf": gf_ref}
    for p, r in enumerate(rows):
        for name, ref in plain.items():
            ref[r, :] = seg(p, name).astype(ref.dtype)
        og_ref[r, :] = jax.nn.sigmoid(seg(p, "om")).astype(og_ref.dtype)


def _in_proj(x2d, g, w_packed):
    t, d = x2d.shape
    tm = TM_IN
    widths = [(512, BF16), (128, BF16), (128, BF16), (1024, F32), (512, BF16), (512, BF16),
              (128, F32), (128, F32)]
    const = lambda i: (0, 0)
    return pl.pallas_call(
        _in_proj_kernel,
        grid=(t // tm,),
        in_specs=[pl.BlockSpec((tm, d), lambda i: (i, 0)),
                  pl.BlockSpec((1, d), const),
                  pl.BlockSpec(w_packed.shape, const)],
        out_specs=[pl.BlockSpec((tm, w), lambda i: (i, 0)) for w, _ in widths],
        out_shape=[jax.ShapeDtypeStruct((t, w), dt) for w, dt in widths],
        compiler_params=_cparams(1),
        name="in_proj",
    )(x2d, g, w_packed)


def _swa_kernel(sink_ref, q_ref, kp_ref, kc_ref, vp_ref, vc_ref, gn_ref, o_ref, bias_s):
    b = pl.program_id(0)
    n = pl.program_id(1)

    @pl.when((b == 0) & (n == 0))
    def _():
        ti = lax.broadcasted_iota(jnp.int32, (WINDOW, 2 * WINDOW), 0)
        si = lax.broadcasted_iota(jnp.int32, (WINDOW, 2 * WINDOW), 1)
        dist = ti + WINDOW - si
        band = (dist >= 0) & (dist < WINDOW)
        distf = dist.astype(F32)
        for hidx in range(ATT_HEADS):
            slope = 2.0 ** (-8.0 * (hidx + 1) / ATT_HEADS)
            full = jnp.where(band, -slope * distf, NEG_INF)
            bias_s[1, hidx] = full
            bias_s[0, hidx] = jnp.where(si >= WINDOW, full, NEG_INF)

    scale = 1.0 / math.sqrt(ATT_HEAD_DIM)
    k_all = jnp.concatenate([kp_ref[...], kc_ref[...]], axis=0).astype(F32) * scale
    v_all = jnp.concatenate([vp_ref[...], vc_ref[...]], axis=0).astype(F32)
    lane = lax.broadcasted_iota(jnp.int32, k_all.shape, 1)
    low = lane < ATT_HEAD_DIM

    def lane_variants(a):
        ar = pltpu.roll(a, ATT_HEAD_DIM, axis=1)
        z = jnp.zeros_like(a)
        return ((jnp.where(low, a, z).astype(BF16), jnp.where(low, z, ar).astype(BF16)),
                (jnp.where(low, ar, z).astype(BF16), jnp.where(low, z, a).astype(BF16)))

    k_ops = lane_variants(k_all)
    v_ops = lane_variants(v_all)
    n_tiles = ATT_Q // LANES
    kv_of = lambda j: (2 * j) // (ATT_HEADS // 2)
    blocks = range(SWA_QB)
    rows = [slice(WINDOW * i, WINDOW * (i + 1)) for i in blocks]
    keys = [slice(WINDOW * i, WINDOW * (i + 2)) for i in blocks]
    bias_slot = [jnp.where(n == 0, 0, 1) if i == 0 else 1 for i in blocks]

    scores = [[[lax.dot_general(q_ref[rows[i], LANES * j:LANES * (j + 1)],
                                k_ops[kv_of(j)][par][keys[i]], _NT, preferred_element_type=F32)
                for par in range(2)] for j in range(n_tiles)] for i in blocks]
    probs = [[[None, None] for _ in range(n_tiles)] for _ in blocks]
    rinv = [[[None, None] for _ in range(n_tiles)] for _ in blocks]
    for i in blocks:
        for j in range(n_tiles):
            for par in range(2):
                hidx = 2 * j + par
                logits = scores[i][j][par] + bias_s[bias_slot[i], hidx]
                sink = sink_ref[hidx]
                mx = jnp.maximum(jnp.max(logits, axis=-1, keepdims=True), sink)
                e = jnp.exp(logits - mx)
                den = jnp.sum(e, axis=-1, keepdims=True) + jnp.exp(sink - mx)
                probs[i][j][par] = e.astype(BF16)
                rinv[i][j][par] = 1.0 / den
    lane_o = lax.broadcasted_iota(jnp.int32, (WINDOW, LANES), 1)
    for i in blocks:
        tiles = []
        for j in range(n_tiles):
            acc = (jnp.dot(probs[i][j][0], v_ops[kv_of(j)][0][keys[i]], preferred_element_type=F32)
                   + jnp.dot(probs[i][j][1], v_ops[kv_of(j)][1][keys[i]],
                             preferred_element_type=F32))
            tiles.append(acc * jnp.where(lane_o < ATT_HEAD_DIM, rinv[i][j][0], rinv[i][j][1]))
        att = jnp.concatenate(tiles, axis=1)
        o_ref[rows[i], :] = _rms(att, gn_ref[...]).astype(BF16)


def _swa(qa, ka, va, sinks, gn, bsz, seq):
    nb = seq // WINDOW
    ns = nb // SWA_QB
    t = bsz * seq
    cur = lambda b, n: (b * ns + n, 0)
    prev = lambda b, n: (b * nb + jnp.maximum(n * SWA_QB - 1, 0), 0)
    return pl.pallas_call(
        _swa_kernel,
        grid=(bsz, ns),
        in_specs=[pl.BlockSpec(memory_space=pltpu.SMEM),
                  pl.BlockSpec((SWA_QB * WINDOW, ATT_Q), cur),
                  pl.BlockSpec((WINDOW, ATT_KV), prev),
                  pl.BlockSpec((SWA_QB * WINDOW, ATT_KV), cur),
                  pl.BlockSpec((WINDOW, ATT_KV), prev),
                  pl.BlockSpec((SWA_QB * WINDOW, ATT_KV), cur),
                  pl.BlockSpec((1, ATT_Q), lambda b, n: (0, 0))],
        out_specs=pl.BlockSpec((SWA_QB * WINDOW, ATT_Q), cur),
        out_shape=jax.ShapeDtypeStruct((t, ATT_Q), BF16),
        scratch_shapes=[pltpu.VMEM((2, ATT_HEADS, WINDOW, 2 * WINDOW), F32)],
        compiler_params=_cparams(2),
        name="swa",
    )(sinks, qa, ka, ka, va, va, gn)


def _split3(a):
    hi = a.astype(BF16)
    r1 = a - hi.astype(F32)
    mid = r1.astype(BF16)
    lo = (r1 - mid.astype(F32)).astype(BF16)
    return hi, mid, lo


def _mlstm_kernel(qk_ref, v_ref, og_ref, gi_ref, gf_ref, cw_ref, cb_ref, bi_ref, bf_ref, gn_ref,
                  out_ref, xbuf, cn_s, m_s):
    c = pl.program_id(1)
    nseq, L = qk_ref.shape[0], qk_ref.shape[1]
    HD = ML_HEAD_DIM
    seqs = range(nseq)
    heads = range(ML_HEADS)
    pairs = [(s, h) for s in seqs for h in heads]
    sls = [slice(HD * h, HD * (h + 1)) for h in heads]

    @pl.when(c == 0)
    def _():
        xbuf[:, 0:8, :] = jnp.zeros((nseq, 8, xbuf.shape[2]), F32)
        cn_s[...] = jnp.zeros(cn_s.shape, F32)
        m_s[...] = jnp.zeros(m_s.shape, F32)

    ti = lax.broadcasted_iota(jnp.int32, (L, L), 0)
    si = lax.broadcasted_iota(jnp.int32, (L, L), 1)
    causal = si <= ti
    tril = jnp.where(causal, 1.0, 0.0).astype(BF16)
    lane_l = lax.broadcasted_iota(jnp.int32, (L, HD), 1)
    ones_col = jnp.where(lane_l == 0, 1.0, 0.0).astype(BF16)

    gi, b = [], []
    for s in seqs:
        gi.append(gi_ref[s] + bi_ref[...])
        z = gf_ref[s] + bf_ref[...]
        lf = jnp.minimum(z, 0.0) - jnp.log1p(jnp.exp(-jnp.abs(z)))
        acc = jnp.zeros((L, LANES), F32)
        for part in _split3(lf):
            acc = acc + jnp.dot(tril, part, preferred_element_type=F32)
        b.append(acc)

    qb, kf = {}, {}
    for s in seqs:
        xbuf[s, 8:8 + L, :] = qk_ref[s]
        y = cb_ref[...] + cw_ref[CONV_WIDTH - 1:CONV_WIDTH, :] * xbuf[s, 8:8 + L, :]
        for j in range(CONV_WIDTH - 1):
            sh = CONV_WIDTH - 1 - j
            y = y + cw_ref[j:j + 1, :] * xbuf[s, 8 - sh:8 - sh + L, :]
        xbuf[s, 0:8, :] = xbuf[s, L:L + 8, :]
        qk = y * jax.nn.sigmoid(y)
        for h in heads:
            qb[s, h] = qk[:, sls[h]].astype(BF16)
            kf[s, h] = qk[:, ML_W + HD * h:ML_W + HD * (h + 1)] * (1.0 / math.sqrt(HD))
    kb = {p: kf[p].astype(BF16) for p in pairs}
    vext = {(s, h): jnp.concatenate([v_ref[s, :, sls[h]], ones_col], axis=1) for s, h in pairs}
    cn = {(s, h): cn_s[s * ML_HEADS + h] for s, h in pairs}
    qk_d = {p: lax.dot_general(qb[p], kb[p], _NT, preferred_element_type=F32) for p in pairs}
    qc = {p: jnp.dot(qb[p], cn[p].astype(BF16), preferred_element_type=F32) for p in pairs}

    m_inter, m_new, w, decay, rt = [], [], [], [], []
    for s in seqs:
        m_st = m_s[s]
        b_last = b[s][L - 1:L, :]
        m_inter.append(b[s] + m_st)
        log_w = b_last - b[s] + gi[s]
        m_new.append(jnp.maximum(b_last + m_st, jnp.max(log_w, axis=0, keepdims=True)))
        w.append(jnp.exp(log_w - m_new[s]))
        decay.append(jnp.exp(b_last + m_st - m_new[s]))
        rt.append(jnp.transpose(gi[s] - b[s]))

    s_b, kw_b, m_ts, a_inters = {}, {}, {}, {}
    for s, h in pairs:
        log_d = jnp.where(causal, b[s][:, h:h + 1] + rt[s][h:h + 1, :], NEG_INF)
        m_t = jnp.maximum(m_inter[s][:, h:h + 1], jnp.max(log_d, axis=1, keepdims=True))
        s_b[s, h] = (qk_d[s, h] * jnp.exp(log_d - m_t)).astype(BF16)
        kw_b[s, h] = (kf[s, h] * w[s][:, h:h + 1]).astype(BF16)
        m_ts[s, h] = m_t
        a_inters[s, h] = jnp.exp(m_inter[s][:, h:h + 1] - m_t)

    sv = {p: jnp.dot(s_b[p], vext[p], preferred_element_type=F32) for p in pairs}
    upd = {p: lax.dot_general(kw_b[p], vext[p], _TN, preferred_element_type=F32) for p in pairs}

    for s, h in pairs:
        nd = sv[s, h] + a_inters[s, h] * qc[s, h]
        num = nd[:, :HD]
        den = nd[:, HD:HD + 1]
        hh = num * (1.0 / jnp.maximum(jnp.abs(den), jnp.exp(-m_ts[s, h])))
        cn_s[s * ML_HEADS + h] = decay[s][:, h:h + 1] * cn[s, h] + upd[s, h]
        hm = og_ref[s, :, sls[h]].astype(F32) * hh
        out_ref[s, :, sls[h]] = _rms(hm, gn_ref[:, sls[h]]).astype(BF16)

    for s in seqs:
        m_s[s] = m_new[s]


def _mlstm(qk, vm, og, gi, gf, cw, cb, bi, bf, gn, bsz, seq):
    L = ML_CHUNK
    nc = seq // L
    nseq = ML_NB
    t = bsz * seq
    seq3 = lambda a: a.reshape(bsz, seq, a.shape[-1])
    row = lambda b, c: (b, c, 0)
    const = lambda b, c: (0, 0)
    out = pl.pallas_call(
        _mlstm_kernel,
        grid=(bsz // nseq, nc),
        in_specs=[pl.BlockSpec((nseq, L, 2 * ML_W), row),
                  pl.BlockSpec((nseq, L, ML_W), row),
                  pl.BlockSpec((nseq, L, ML_W), row),
                  pl.BlockSpec((nseq, L, LANES), row),
                  pl.BlockSpec((nseq, L, LANES), row),
                  pl.BlockSpec((CONV_WIDTH, 2 * ML_W), const),
                  pl.BlockSpec((1, 2 * ML_W), const),
                  pl.BlockSpec((1, LANES), const),
                  pl.BlockSpec((1, LANES), const),
                  pl.BlockSpec((1, ML_W), const)],
        out_specs=pl.BlockSpec((nseq, L, ML_W), row),
        out_shape=jax.ShapeDtypeStruct((bsz, seq, ML_W), BF16),
        scratch_shapes=[pltpu.VMEM((nseq, L + 8, 2 * ML_W), F32),
                        pltpu.VMEM((nseq * ML_HEADS, ML_HEAD_DIM, 2 * ML_HEAD_DIM), F32),
                        pltpu.VMEM((nseq, 1, LANES), F32)],
        compiler_params=_cparams(2),
        name="mlstm",
    )(seq3(qk), seq3(vm), seq3(og), seq3(gi), seq3(gf), cw, cb, bi, bf, gn)
    return out.reshape(t, ML_W)


def _kv_kernel(mem_ref, g_ref, w_ref, k_ref, v_ref):
    mn = _rms(mem_ref[...], g_ref[...]).astype(BF16)
    hw = k_ref.shape[1]
    k_ref[...] = jnp.dot(mn, w_ref[:, :hw], preferred_element_type=F32).astype(BF16)
    v_ref[...] = jnp.dot(mn, w_ref[:, hw:], preferred_element_type=F32).astype(BF16)


def _kv_prep(mem2d, g, w_ckv):
    r, d = mem2d.shape
    hw = w_ckv.shape[1] // 2
    tm = 256
    return pl.pallas_call(
        _kv_kernel,
        grid=(r // tm,),
        in_specs=[pl.BlockSpec((tm, d), lambda i: (i, 0)),
                  pl.BlockSpec((1, d), lambda i: (0, 0)),
                  pl.BlockSpec(w_ckv.shape, lambda i: (0, 0))],
        out_specs=[pl.BlockSpec((tm, hw), lambda i: (i, 0))] * 2,
        out_shape=[jax.ShapeDtypeStruct((r, hw), BF16)] * 2,
        compiler_params=_cparams(1),
        name="kv_prep",
    )(mem2d, g, w_ckv)


_R_E0, _R_E1, _R_RANK0, _R_RANK1, _R_P0, _R_P1 = range(6)
_ROUTER_LANE0 = N_GROUPS


_PACK_ROWS = 8


def _store_slabs(ref, v, rows, first=0):
    span = pl.ds(first * _PACK_ROWS, rows * _PACK_ROWS)
    ref[span, :] = v.reshape(rows * _PACK_ROWS, LANES).astype(ref.dtype)


def _load_slabs(ref, rows):
    return ref[...].astype(F32).reshape(rows, _PACK_ROWS * LANES)


def _store_slabs_staged(ref, v, rows, stage, first=0):
    base = first * _PACK_ROWS
    for r in range(_PACK_ROWS):
        stage[pl.ds(base + r, rows, stride=_PACK_ROWS), :] = v[:, LANES * r:LANES * (r + 1)]
    span = pl.ds(base, rows * _PACK_ROWS)
    ref[span, :] = stage[span, :].astype(ref.dtype)


def _load_slabs_staged(ref, rows, stage):
    stage[...] = ref[...].astype(F32)
    return jnp.concatenate(
        [stage[pl.ds(r, rows, stride=_PACK_ROWS), :] for r in range(_PACK_ROWS)], axis=1)


def _mid_kernel(x_ref, att_ref, hm_ref, wo_ref, gc_ref, wq_ref, k_ref, v_ref, wco_ref, gf_ref,
                wr_ref, br_ref, x2_ref, xn_ref, route_ref, idx_ref, cnt_ref, carry_s, stage_s):
    i = pl.program_id(0)
    tm = x_ref.shape[0] // MID_PARTS
    parts = range(MID_PARTS)
    rows = [pl.ds(p * tm, tm) for p in parts]

    @pl.when(i == 0)
    def _():
        carry_s[...] = jnp.zeros(carry_s.shape, F32)

    x1 = [x_ref[rows[p], :]
          + jnp.dot(att_ref[rows[p], :], wo_ref[0:ATT_Q, :], preferred_element_type=F32)
          + jnp.dot(hm_ref[rows[p], :], wo_ref[ATT_Q:, :], preferred_element_type=F32)
          for p in parts]

    xc = [_rms(x1[p], gc_ref[...]).astype(BF16) for p in parts]
    qb = [jnp.dot(xc[p], wq_ref[...], preferred_element_type=F32).astype(BF16) for p in parts]
    sls = [slice(X_HEAD_DIM * h, X_HEAD_DIM * (h + 1)) for h in range(X_HEADS)]
    sc = [[lax.dot_general(qb[p][:, sl], k_ref[:, sl], _NT, preferred_element_type=F32)
           for sl in sls] for p in parts]
    es = [[None] * X_HEADS for _ in parts]
    rinv = [[None] * X_HEADS for _ in parts]
    for p in parts:
        for h in range(X_HEADS):
            s = sc[p][h] * (1.0 / math.sqrt(X_HEAD_DIM))
            e = jnp.exp(s - jnp.max(s, axis=-1, keepdims=True))
            es[p][h] = e.astype(BF16)
            rinv[p][h] = 1.0 / jnp.sum(e, axis=-1, keepdims=True)
    o = [jnp.concatenate(
        [jnp.dot(es[p][h], v_ref[:, sls[h]], preferred_element_type=F32) * rinv[p][h]
         for h in range(X_HEADS)], axis=1).astype(BF16) for p in parts]
    x2 = [x1[p] + jnp.dot(o[p], wco_ref[...], preferred_element_type=F32) for p in parts]

    logits_p = []
    for p in parts:
        x2_ref[rows[p], :] = x2[p]
        xn = _rms(x2[p], gf_ref[...])
        xh = xn.astype(BF16)
        _store_slabs_staged(xn_ref, xn, tm, stage_s, first=p * tm)
        xl = (xn - xh.astype(F32)).astype(BF16)
        lg2 = jnp.dot(xh, wr_ref[...], preferred_element_type=F32)
        logits_p.append(lg2[:, :LANES] + lg2[:, LANES:]
                        + jnp.dot(xl, wr_ref[:, :LANES], preferred_element_type=F32) + br_ref[...])

    for p in parts:
        _route_part(logits_p[p], rows[p], p * tm, route_ref, idx_ref, cnt_ref, carry_s)


def _route_part(logits, rows, first, route_ref, idx_ref, cnt_ref, carry_s):
    tm = logits.shape[0]
    lane = lax.broadcasted_iota(jnp.int32, (tm, LANES), 1)
    lanef = lane.astype(F32)
    big = float(4 * LANES)
    gl = jnp.where(lane < N_GROUPS, logits, NEG_INF)
    gmax = jnp.max(gl, axis=-1, keepdims=True)
    gsel = jnp.min(jnp.where(gl == gmax, lanef, big), axis=-1, keepdims=True)
    gw = 1.0 / jnp.sum(jnp.exp(gl - gmax), axis=-1, keepdims=True)
    lo_lane = _ROUTER_LANE0 + EXPERTS_PER_GROUP * gsel
    in_group = (lanef >= lo_lane) & (lanef < lo_lane + EXPERTS_PER_GROUP)
    el = jnp.where(in_group, logits, NEG_INF)
    v0 = jnp.max(el, axis=-1, keepdims=True)
    i0 = jnp.min(jnp.where(el == v0, lanef, big), axis=-1, keepdims=True)
    el2 = jnp.where(lanef == i0, NEG_INF, el)
    v1 = jnp.max(el2, axis=-1, keepdims=True)
    i1 = jnp.min(jnp.where(el2 == v1, lanef, big), axis=-1, keepdims=True)
    tt = jnp.exp(v1 - v0)
    p0 = gw / (1.0 + tt)
    p1 = gw * tt / (1.0 + tt)

    sel0 = lanef == i0
    sel1 = lanef == i1
    mb = jnp.where(sel0 | sel1, 1.0, 0.0).astype(BF16)
    ti = lax.broadcasted_iota(jnp.int32, (tm, tm), 0)
    si = lax.broadcasted_iota(jnp.int32, (tm, tm), 1)
    strict = jnp.where(si < ti, 1.0, 0.0).astype(BF16)
    carry = carry_s[...]
    pref = jnp.dot(strict, mb, preferred_element_type=F32) + carry
    r0 = jnp.sum(jnp.where(sel0, pref, 0.0), axis=-1, keepdims=True)
    r1 = jnp.sum(jnp.where(sel1, pref, 0.0), axis=-1, keepdims=True)
    carry = carry + jnp.sum(mb.astype(F32), axis=0, keepdims=True)
    carry_s[...] = carry
    cnt_ref[...] = carry

    route = jnp.zeros((tm, LANES), F32)
    for idx, col in ((_R_E0, i0 - _ROUTER_LANE0), (_R_E1, i1 - _ROUTER_LANE0), (_R_RANK0, r0),
                     (_R_RANK1, r1), (_R_P0, p0), (_R_P1, p1)):
        route = jnp.where(lane == idx, col, route)
    route_ref[rows, :] = route
    idx_ref[:, pl.ds(first, tm)] = jnp.transpose(route)[0:8, :].astype(jnp.int32)


def _mid(x2d, att, hm, wo, gc, wq, kmem, vmem_, wco, gf, wr, br, seq):
    t, d = x2d.shape
    tm = TM_MID
    per_b = seq // tm
    rowmap = lambda i: (i, 0)
    const = lambda i: (0, 0)
    bmap = lambda i: (i // per_b, 0)
    mem_len = kmem.shape[0] // (t // seq)
    return pl.pallas_call(
        _mid_kernel,
        grid=(t // tm,),
        in_specs=[pl.BlockSpec((tm, d), rowmap),
                  pl.BlockSpec((tm, ATT_Q), rowmap),
                  pl.BlockSpec((tm, ML_W), rowmap),
                  pl.BlockSpec(wo.shape, const),
                  pl.BlockSpec((1, d), const),
                  pl.BlockSpec(wq.shape, const),
                  pl.BlockSpec((mem_len, kmem.shape[1]), bmap),
                  pl.BlockSpec((mem_len, vmem_.shape[1]), bmap),
                  pl.BlockSpec(wco.shape, const),
                  pl.BlockSpec((1, d), const),
                  pl.BlockSpec(wr.shape, const),
                  pl.BlockSpec((1, LANES), const)],
        out_specs=[pl.BlockSpec((tm, d), rowmap),
                   pl.BlockSpec((tm * _PACK_ROWS, LANES), rowmap),
                   pl.BlockSpec((tm, LANES), rowmap),
                   pl.BlockSpec((8, tm), lambda i: (0, i)),
                   pl.BlockSpec((1, LANES), const)],
        out_shape=[jax.ShapeDtypeStruct((t, d), F32),
                   jax.ShapeDtypeStruct((t * _PACK_ROWS, LANES), BF16),
                   jax.ShapeDtypeStruct((t, LANES), F32),
                   jax.ShapeDtypeStruct((8, t), jnp.int32),
                   jax.ShapeDtypeStruct((1, LANES), F32)],
        scratch_shapes=[pltpu.VMEM((1, LANES), F32),
                        pltpu.VMEM((tm * _PACK_ROWS, LANES), F32)],
        compiler_params=_cparams(1),
        name="mid",
    )(x2d, att, hm, wo, gc, wq, kmem, vmem_, wco, gf, wr, br)


_M_BLOCK_E, _M_PAD_END, _M_NUSED = range(3)
_META_LANES = 2 * LANES


def _plan_kernel(cnt_ref, idx_ref, dest_ref, meta_ref):
    cnt = cnt_ref[...]
    lane = lax.broadcasted_iota(jnp.int32, (1, LANES), 1)
    is_expert = (lane >= _ROUTER_LANE0) & (lane < _ROUTER_LANE0 + N_EXPERTS)
    nblk = jnp.where(is_expert, jnp.floor((cnt + (BM - 1)) * (1.0 / BM)), 0.0)
    jj = lax.broadcasted_iota(jnp.int32, (LANES, LANES), 0)
    kk = lax.broadcasted_iota(jnp.int32, (LANES, LANES), 1)
    upper = jnp.where(jj <= kk, 1.0, 0.0).astype(BF16)
    pend_blk = jnp.dot(jnp.broadcast_to(nblk, (8, LANES)).astype(BF16), upper,
                       preferred_element_type=F32)[0:1, :]
    pstart_rows = (pend_blk - nblk) * BM
    pend_rows = pend_blk * BM

    idx = idx_ref[...]
    off = jnp.zeros(idx.shape, F32)
    for e in range(N_EXPERTS):
        lane_e = _ROUTER_LANE0 + e
        off = jnp.where(idx == e, pstart_rows[:, lane_e:lane_e + 1], off)
    ranks = pltpu.roll(idx, idx.shape[0] - 2, axis=0)
    dest_ref[...] = ranks + off.astype(jnp.int32)

    blk = lax.broadcasted_iota(jnp.int32, (1, _META_LANES), 1).astype(F32)
    block_e = jnp.zeros((1, _META_LANES), F32)
    for e in range(N_EXPERTS):
        lane_e = _ROUTER_LANE0 + e
        block_e = block_e + jnp.where(pend_blk[:, lane_e:lane_e + 1] <= blk, 1.0, 0.0)
    block_e = jnp.minimum(block_e, N_EXPERTS - 1.0)
    last = _ROUTER_LANE0 + N_EXPERTS - 1
    nused = pend_blk[:, last:last + 1]
    pend_wide = jnp.concatenate([pend_rows, jnp.zeros((1, _META_LANES - LANES), F32)], axis=1)
    sub = lax.broadcasted_iota(jnp.int32, (8, _META_LANES), 0)
    meta = jnp.where(sub == _M_BLOCK_E, block_e, jnp.where(sub == _M_PAD_END, pend_wide, nused))
    meta_ref[...] = meta.astype(jnp.int32)


def _plan(cnt, idx):
    t = idx.shape[1]
    return pl.pallas_call(
        _plan_kernel,
        grid=(1,),
        in_specs=[pl.BlockSpec(cnt.shape, lambda i: (0, 0)),
                  pl.BlockSpec(idx.shape, lambda i: (0, 0))],
        out_specs=[pl.BlockSpec((8, t), lambda i: (0, 0)),
                   pl.BlockSpec((8, _META_LANES), lambda i: (0, 0))],
        out_shape=[jax.ShapeDtypeStruct((8, t), jnp.int32),
                   jax.ShapeDtypeStruct((8, _META_LANES), jnp.int32)],
        compiler_params=_cparams(1),
        name="plan",
    )(cnt, idx)


_SLAB = BM * _PACK_ROWS


def _dispatch_kernel(meta_ref, d_ref, xn_ref, xpad_ref, zbuf, sem, zsem):
    i = pl.program_id(0)
    td = xn_ref.shape[0] // _PACK_ROWS

    def zero_copy(e):
        pend = meta_ref[_M_PAD_END, _ROUTER_LANE0 + e]
        pstart = meta_ref[_M_PAD_END, _ROUTER_LANE0 + e - 1]
        first = pl.multiple_of((pend - BM) * _PACK_ROWS, _SLAB)
        cp = pltpu.make_async_copy(zbuf, xpad_ref.at[pl.ds(first, _SLAB), :], zsem)
        return pend > pstart, cp

    @pl.when(i == 0)
    def _():
        zbuf[...] = jnp.zeros(zbuf.shape, zbuf.dtype)
        for e in range(N_EXPERTS):
            nonempty, cp = zero_copy(e)
            pl.when(nonempty)(cp.start)
        for e in range(N_EXPERTS):
            nonempty, cp = zero_copy(e)
            pl.when(nonempty)(cp.wait)

        def tail_copy(b):
            return pltpu.make_async_copy(
                zbuf, xpad_ref.at[pl.ds(pl.multiple_of(b * _SLAB, _SLAB), _SLAB), :], zsem)

        first_unused = meta_ref[_M_NUSED, 0]
        n_blocks = xpad_ref.shape[0] // _SLAB
        lax.fori_loop(first_unused, n_blocks, lambda b, c: (tail_copy(b).start(), c)[1], 0)
        lax.fori_loop(first_unused, n_blocks, lambda b, c: (tail_copy(b).wait(), c)[1], 0)

    def row_copy(t, dst):
        src = pl.ds(pl.multiple_of(t * _PACK_ROWS, _PACK_ROWS), _PACK_ROWS)
        dsl = pl.ds(pl.multiple_of(dst * _PACK_ROWS, _PACK_ROWS), _PACK_ROWS)
        return pltpu.make_async_copy(xn_ref.at[src, :], xpad_ref.at[dsl, :], sem)

    def start(t, carry):
        row_copy(t, d_ref[0, t]).start(priority=0)
        row_copy(t, d_ref[1, t]).start(priority=1)
        return carry

    def wait(t, carry):
        row_copy(0, 0).wait()
        row_copy(0, 0).wait()
        return carry

    lax.fori_loop(0, td, start, 0, unroll=8)
    lax.fori_loop(0, td, wait, 0, unroll=8)


def _dispatch(meta, dest, xn_packed, rows):
    t = dest.shape[1]
    td = TD
    return pl.pallas_call(
        _dispatch_kernel,
        grid_spec=pltpu.PrefetchScalarGridSpec(
            num_scalar_prefetch=1,
            grid=(t // td,),
            in_specs=[pl.BlockSpec((8, td), lambda i, m: (0, i), memory_space=pltpu.SMEM),
                      pl.BlockSpec((td * _PACK_ROWS, LANES), lambda i, m: (i, 0))],
            out_specs=pl.BlockSpec(memory_space=pl.ANY),
            scratch_shapes=[pltpu.VMEM((_SLAB, LANES), BF16),
                            pltpu.SemaphoreType.DMA(()),
                            pltpu.SemaphoreType.DMA(())]),
        out_shape=jax.ShapeDtypeStruct((rows * _PACK_ROWS, LANES), BF16),
        compiler_params=_cparams(1),
        name="dispatch",
    )(meta, dest, xn_packed)


def _expert_kernel(meta_ref, xpad_ref, wg_ref, wu_ref, wd_ref, ypad_ref, wg_s, wu_s, wd_s,
                   xbuf, ybuf, xsem, ysem):
    e = pl.program_id(0)
    nused = meta_ref[_M_NUSED, 0]
    lane_e = _ROUTER_LANE0 + e
    first_blk = lax.div(meta_ref[_M_PAD_END, lane_e - 1], BM)
    n_blk = lax.div(meta_ref[_M_PAD_END, lane_e], BM) - first_blk

    def block_rows(g):
        return pl.ds(pl.multiple_of(g * _SLAB, _SLAB), _SLAB)

    def x_copy(g, slot):
        return pltpu.make_async_copy(xpad_ref.at[block_rows(g), :], xbuf.at[slot], xsem.at[slot])

    def y_copy(g, slot):
        return pltpu.make_async_copy(ybuf.at[slot], ypad_ref.at[block_rows(g), :], ysem.at[slot])

    nx = xbuf.shape[0]
    ahead = nx - 1

    @pl.when(e == 0)
    def _():
        for g0 in range(ahead):
            @pl.when(g0 < nused)
            def _():
                x_copy(g0, g0).start(priority=1)

    @pl.when(n_blk > 0)
    def _():
        wg_s[...] = wg_ref[...].astype(BF16)
        wu_s[...] = wu_ref[...].astype(BF16)
        wd_s[...] = wd_ref[...].astype(BF16)

    def body(j, carry):
        g = first_blk + j
        xslot = lax.rem(g, nx)
        yslot = lax.rem(g, 2)
        x_copy(g, xslot).wait()

        @pl.when(g + ahead < nused)
        def _():
            x_copy(g + ahead, lax.rem(g + ahead, nx)).start(priority=1)

        xb = _load_slabs(xbuf.at[xslot], BM).astype(BF16)
        gate = jnp.dot(xb, wg_s[...], preferred_element_type=F32)
        up = jnp.dot(xb, wu_s[...], preferred_element_type=F32)
        hb = (gate * jax.nn.sigmoid(gate) * up).astype(BF16)
        y = jnp.dot(hb, wd_s[...], preferred_element_type=F32)

        @pl.when(g >= 2)
        def _():
            y_copy(g - 2, yslot).wait()

        _store_slabs(ybuf.at[yslot], y, BM)
        y_copy(g, yslot).start()
        return carry

    lax.fori_loop(0, n_blk, body, 0)

    @pl.when(e == pl.num_programs(0) - 1)
    def _():
        for back in (2, 1):
            @pl.when(nused >= back)
            def _():
                y_copy(nused - back, lax.rem(nused - back, 2)).wait()

        ybuf[0] = jnp.zeros(ybuf.shape[1:], ybuf.dtype)
        n_blocks = ypad_ref.shape[0] // _SLAB
        lax.fori_loop(nused, n_blocks, lambda b, c: (y_copy(b, 0).start(), c)[1], 0)
        lax.fori_loop(nused, n_blocks, lambda b, c: (y_copy(b, 0).wait(), c)[1], 0)


def _experts(meta, xpad, wg, wu, wd):
    n_exp, d, de = wg.shape
    wmap = lambda e, m: (e, 0, 0)
    return pl.pallas_call(
        _expert_kernel,
        grid_spec=pltpu.PrefetchScalarGridSpec(
            num_scalar_prefetch=1,
            grid=(n_exp,),
            in_specs=[pl.BlockSpec(memory_space=pl.ANY),
                      pl.BlockSpec((None, d, de), wmap),
                      pl.BlockSpec((None, d, de), wmap),
                      pl.BlockSpec((None, de, d), wmap)],
            out_specs=pl.BlockSpec(memory_space=pl.ANY),
            scratch_shapes=[pltpu.VMEM((d, de), BF16),
                            pltpu.VMEM((d, de), BF16),
                            pltpu.VMEM((de, d), BF16),
                            pltpu.VMEM((EXPERT_AHEAD + 1, _SLAB, LANES), BF16),
                            pltpu.VMEM((2, _SLAB, LANES), BF16),
                            pltpu.SemaphoreType.DMA((EXPERT_AHEAD + 1,)),
                            pltpu.SemaphoreType.DMA((2,))]),
        out_shape=jax.ShapeDtypeStruct(xpad.shape, BF16),
        compiler_params=_cparams(1),
        name="experts",
    )(meta, xpad, wg, wu, wd)


def _combine_kernel(dc_ref, dn_ref, x2_ref, route_ref, g_ref, ypad_ref, out_ref, ybuf, sem,
                    stage0_s, stage1_s):
    i = pl.program_id(0)
    n = pl.num_programs(0)
    tc = x2_ref.shape[0]

    def row_copy(src, slot, which, t):
        ssl = pl.ds(pl.multiple_of(src * _PACK_ROWS, _PACK_ROWS), _PACK_ROWS)
        dsl = pl.ds(pl.multiple_of(t * _PACK_ROWS, _PACK_ROWS), _PACK_ROWS)
        return pltpu.make_async_copy(ypad_ref.at[ssl, :], ybuf.at[slot, which, dsl, :],
                                     sem.at[slot])

    def issue(d_ref, slot):
        def body(t, carry):
            row_copy(d_ref[0, t], slot, 0, t).start(priority=0)
            row_copy(d_ref[1, t], slot, 1, t).start(priority=1)
            return carry
        lax.fori_loop(0, tc, body, 0, unroll=8)

    slot = i % 2

    @pl.when(i == 0)
    def _():
        issue(dc_ref, 0)

    @pl.when(i + 1 < n)
    def _():
        issue(dn_ref, 1 - slot)

    def wait(t, carry):
        row_copy(0, slot, 0, 0).wait()
        row_copy(0, slot, 1, 0).wait()
        return carry
    lax.fori_loop(0, tc, wait, 0, unroll=8)

    route = route_ref[...]
    p0 = route[:, _R_P0:_R_P0 + 1]
    p1 = route[:, _R_P1:_R_P1 + 1]
    y0 = _load_slabs_staged(ybuf.at[slot, 0], tc, stage0_s)
    y1 = _load_slabs_staged(ybuf.at[slot, 1], tc, stage1_s)
    x3 = x2_ref[...] + p0 * y0 + p1 * y1
    out_ref[...] = _rms(x3, g_ref[...])


def _combine(dest, x2, route, g, ypad):
    t, d = x2.shape
    tc = TC
    nt = t // tc
    cur = pl.BlockSpec((8, tc), lambda i: (0, i), memory_space=pltpu.SMEM)
    nxt = pl.BlockSpec((8, tc), lambda i: (0, jnp.minimum(i + 1, nt - 1)),
                       memory_space=pltpu.SMEM)
    return pl.pallas_call(
        _combine_kernel,
        grid=(nt,),
        in_specs=[cur, nxt,
                  pl.BlockSpec((tc, d), lambda i: (i, 0)),
                  pl.BlockSpec((tc, LANES), lambda i: (i, 0)),
                  pl.BlockSpec((1, d), lambda i: (0, 0)),
                  pl.BlockSpec(memory_space=pl.ANY)],
        out_specs=pl.BlockSpec((tc, d), lambda i: (i, 0)),
        out_shape=jax.ShapeDtypeStruct((t, d), F32),
        scratch_shapes=[pltpu.VMEM((2, 2, tc * _PACK_ROWS, LANES), BF16),
                        pltpu.SemaphoreType.DMA((2,)),
                        pltpu.VMEM((tc * _PACK_ROWS, LANES), F32),
                        pltpu.VMEM((tc * _PACK_ROWS, LANES), F32)],
        compiler_params=_cparams(1),
        name="combine",
    )(dest, dest, x2, route, g, ypad)


def _pad_cols(a, width):
    return jnp.pad(a, ((0, 0), (0, width - a.shape[1])))


def kernel(x, mem, norm_mix, w_in, b_gates, conv_w, conv_b, att_sinks, norm_att_out, norm_ml_out,
           w_out, norm_cross, norm_mem, w_cq, w_ckv, w_co, norm_ffn, w_router_group,
           b_router_group, w_router_expert, b_router_expert, w_e_gate, w_e_up, w_e_down,
           norm_final):
    bsz, seq, d = x.shape
    t = bsz * seq
    depth = w_in.shape[0]
    assert depth == 1, "the final RMSNorm is fused into the last layer's combine kernel"
    xs = x.reshape(t, d)
    mem2d = mem.reshape(-1, d)

    for l in range(depth):
        n_main = _SEG["om"][1]
        wl = w_in[l]
        w_packed = jnp.concatenate(
            [wl[:, :n_main], _pad_cols(wl[:, n_main:n_main + ML_HEADS], LANES),
             _pad_cols(wl[:, n_main + ML_HEADS:], LANES)], axis=1).astype(BF16)
        bi = _pad_cols(b_gates[l][None, :ML_HEADS], LANES)
        bf = _pad_cols(b_gates[l][None, ML_HEADS:], LANES)
        w_r = _pad_cols(jnp.concatenate([w_router_group[l], w_router_expert[l]], axis=1), LANES)
        w_r_hi = w_r.astype(BF16)
        w_r_lo = (w_r - w_r_hi.astype(F32)).astype(BF16)
        w_r_packed = jnp.concatenate([w_r_hi, w_r_lo], axis=1)
        b_r = _pad_cols(jnp.concatenate([b_router_group[l], b_router_expert[l]])[None, :], LANES)

        qa, ka, va, qk, vm, og, gi, gf = _in_proj(xs, norm_mix[l][None, :], w_packed)
        att = _swa(qa, ka, va, att_sinks[l], norm_att_out[l][None, :], bsz, seq)
        hm = _mlstm(qk, vm, og, gi, gf, conv_w[l], conv_b[l][None, :], bi, bf,
                    norm_ml_out[l][None, :], bsz, seq)

        kmem, vmem_ = _kv_prep(mem2d, norm_mem[l][None, :], w_ckv[l].astype(BF16))
        x2, xn, route, idx, cnt = _mid(xs, att, hm, w_out[l].astype(BF16), norm_cross[l][None, :],
                                       w_cq[l].astype(BF16), kmem, vmem_, w_co[l].astype(BF16),
                                       norm_ffn[l][None, :], w_r_packed, b_r, seq)

        nbk = (t * TOP_K) // BM + N_EXPERTS
        assert nbk <= _META_LANES
        dest, meta = _plan(cnt, idx)
        xpad = _dispatch(meta, dest, xn, nbk * BM)
        ypad = _experts(meta, xpad, w_e_gate[l], w_e_up[l], w_e_down[l])
        xs = _combine(dest, x2, route, norm_final[None, :], ypad)
    return xs.reshape(bsz, seq, d)
```

```python
import functools
import math

import jax
import jax.numpy as jnp
from jax import lax
from jax.experimental import pallas as pl
from jax.experimental.pallas import tpu as pltpu

F32 = jnp.float32
BF16 = jnp.bfloat16
EPS = 1e-6
NEG_INF = float("-inf")

ATT_HEADS = 8
ATT_HEAD_DIM = 64
ATT_Q = ATT_HEADS * ATT_HEAD_DIM
ATT_KV = 128
WINDOW = 128
ML_HEADS = 4
ML_HEAD_DIM = 128
ML_W = ML_HEADS * ML_HEAD_DIM
CONV_WIDTH = 4
X_HEADS = 4
X_HEAD_DIM = 128
N_GROUPS = 4
EXPERTS_PER_GROUP = 8
N_EXPERTS = N_GROUPS * EXPERTS_PER_GROUP
TOP_K = 2

LANES = 128
TM_IN = 512
IN_PARTS = 2
SWA_QB = 2
ML_CHUNK = 256
ML_NB = 1
TM_MID = 1024
MID_PARTS = 4
TD = 512
TC = 256
BM = 256
EXPERT_AHEAD = 3
VMEM_LIMIT = 56 * 1024 * 1024

_NT = (((1,), (1,)), ((), ()))
_TN = (((0,), (0,)), ((), ()))


def _rms(x, g):
    return x * lax.rsqrt(jnp.mean(x * x, axis=-1, keepdims=True) + EPS) * g


def _cparams(n_axes):
    return pltpu.CompilerParams(dimension_semantics=("arbitrary",) * n_axes,
                                vmem_limit_bytes=VMEM_LIMIT)


_SEG = {"qa": (0, 512), "ka": (512, 640), "va": (640, 768), "qk": (768, 1792),
        "vm": (1792, 2304), "om": (2304, 2816), "gi": (2816, 2944), "gf": (2944, 3072)}


_W_CHUNK = 256


def _in_proj_kernel(x_ref, g_ref, win_ref, qa_ref, ka_ref, va_ref, qk_ref, vm_ref, og_ref,
                    gi_ref, gf_ref, w_ref):
    @pl.when(pl.program_id(0) == 0)
    def _():
        n_main = _SEG["om"][1]
        for lo in range(0, n_main, _W_CHUNK):
            w_ref[:, lo:lo + _W_CHUNK] = win_ref[:, lo:lo + _W_CHUNK].astype(BF16)
        gates = win_ref[:, n_main:n_main + 2 * ML_HEADS]
        pad = jnp.zeros((gates.shape[0], LANES - ML_HEADS), F32)
        for name, first in (("gi", 0), ("gf", ML_HEADS)):
            lo, hi = _SEG[name]
            w_ref[:, lo:hi] = jnp.concatenate([gates[:, first:first + ML_HEADS], pad],
                                              axis=1).astype(BF16)

    tm = x_ref.shape[0] // IN_PARTS
    rows = [pl.ds(p * tm, tm) for p in range(IN_PARTS)]
    hb = [_rms(x_ref[r, :], g_ref[...]).astype(BF16) for r in rows]

    def seg(p, name):
        lo, hi = _SEG[name]
        return jnp.dot(hb[p], w_ref[:, lo:hi], preferred_element_type=F32)

    plain = {"qa": qa_ref, "ka": ka_ref, "va": va_ref, "qk": qk_ref, "vm": vm_ref, "gi": gi_ref,
             "gf": gf_ref}
    for p, r in enumerate(rows):
        for name, ref in plain.items():
            ref[r, :] = seg(p, name).astype(ref.dtype)
        og_ref[r, :] = jax.nn.sigmoid(seg(p, "om")).astype(og_ref.dtype)


def _in_proj(x2d, g, w_in, layer):
    t, d = x2d.shape
    tm = TM_IN
    widths = [(512, BF16), (128, BF16), (128, BF16), (1024, F32), (512, BF16), (512, BF16),
              (128, F32), (128, F32)]
    const = lambda i: (0, 0)
    return pl.pallas_call(
        _in_proj_kernel,
        grid=(t // tm,),
        in_specs=[pl.BlockSpec((tm, d), lambda i: (i, 0)),
                  pl.BlockSpec((1, d), const),
                  pl.BlockSpec((None,) + w_in.shape[1:], lambda i: (layer, 0, 0),
                               pipeline_mode=pl.Buffered(1))],
        out_specs=[pl.BlockSpec((tm, w), lambda i: (i, 0)) for w, _ in widths],
        out_shape=[jax.ShapeDtypeStruct((t, w), dt) for w, dt in widths],
        scratch_shapes=[pltpu.VMEM((d, _SEG["gf"][1]), BF16)],
        compiler_params=_cparams(1),
        name="in_proj",
    )(x2d, g, w_in)


def _swa_kernel(sink_ref, q_ref, kp_ref, kc_ref, vp_ref, vc_ref, gn_ref, o_ref, bias_s):
    b = pl.program_id(0)
    n = pl.program_id(1)

    @pl.when((b == 0) & (n == 0))
    def _():
        ti = lax.broadcasted_iota(jnp.int32, (WINDOW, 2 * WINDOW), 0)
        si = lax.broadcasted_iota(jnp.int32, (WINDOW, 2 * WINDOW), 1)
        dist = ti + WINDOW - si
        band = (dist >= 0) & (dist < WINDOW)
        distf = dist.astype(F32)
        for hidx in range(ATT_HEADS):
            slope = 2.0 ** (-8.0 * (hidx + 1) / ATT_HEADS)
            full = jnp.where(band, -slope * distf, NEG_INF)
            bias_s[1, hidx] = full
            bias_s[0, hidx] = jnp.where(si >= WINDOW, full, NEG_INF)

    scale = 1.0 / math.sqrt(ATT_HEAD_DIM)
    k_all = jnp.concatenate([kp_ref[...], kc_ref[...]], axis=0).astype(F32) * scale
    v_all = jnp.concatenate([vp_ref[...], vc_ref[...]], axis=0).astype(F32)
    lane = lax.broadcasted_iota(jnp.int32, k_all.shape, 1)
    low = lane < ATT_HEAD_DIM

    def lane_variants(a):
        ar = pltpu.roll(a, ATT_HEAD_DIM, axis=1)
        z = jnp.zeros_like(a)
        return ((jnp.where(low, a, z).astype(BF16), jnp.where(low, z, ar).astype(BF16)),
                (jnp.where(low, ar, z).astype(BF16), jnp.where(low, z, a).astype(BF16)))

    k_ops = lane_variants(k_all)
    v_ops = lane_variants(v_all)
    n_tiles = ATT_Q // LANES
    kv_of = lambda j: (2 * j) // (ATT_HEADS // 2)
    blocks = range(SWA_QB)
    rows = [slice(WINDOW * i, WINDOW * (i + 1)) for i in blocks]
    keys = [slice(WINDOW * i, WINDOW * (i + 2)) for i in blocks]
    bias_slot = [jnp.where(n == 0, 0, 1) if i == 0 else 1 for i in blocks]

    scores = [[[lax.dot_general(q_ref[rows[i], LANES * j:LANES * (j + 1)],
                                k_ops[kv_of(j)][par][keys[i]], _NT, preferred_element_type=F32)
                for par in range(2)] for j in range(n_tiles)] for i in blocks]
    probs = [[[None, None] for _ in range(n_tiles)] for _ in blocks]
    rinv = [[[None, None] for _ in range(n_tiles)] for _ in blocks]
    for i in blocks:
        for j in range(n_tiles):
            for par in range(2):
                hidx = 2 * j + par
                logits = scores[i][j][par] + bias_s[bias_slot[i], hidx]
                sink = sink_ref[hidx]
                mx = jnp.maximum(jnp.max(logits, axis=-1, keepdims=True), sink)
                e = jnp.exp(logits - mx)
                den = jnp.sum(e, axis=-1, keepdims=True) + jnp.exp(sink - mx)
                probs[i][j][par] = e.astype(BF16)
                rinv[i][j][par] = 1.0 / den
    lane_o = lax.broadcasted_iota(jnp.int32, (WINDOW, LANES), 1)
    for i in blocks:
        tiles = []
        for j in range(n_tiles):
            acc = (jnp.dot(probs[i][j][0], v_ops[kv_of(j)][0][keys[i]], preferred_element_type=F32)
                   + jnp.dot(probs[i][j][1], v_ops[kv_of(j)][1][keys[i]],
                             preferred_element_type=F32))
            tiles.append(acc * jnp.where(lane_o < ATT_HEAD_DIM, rinv[i][j][0], rinv[i][j][1]))
        att = jnp.concatenate(tiles, axis=1)
        o_ref[rows[i], :] = _rms(att, gn_ref[...]).astype(BF16)


def _swa(qa, ka, va, sinks, gn, bsz, seq):
    nb = seq // WINDOW
    ns = nb // SWA_QB
    t = bsz * seq
    cur = lambda b, n: (b * ns + n, 0)
    prev = lambda b, n: (b * nb + jnp.maximum(n * SWA_QB - 1, 0), 0)
    return pl.pallas_call(
        _swa_kernel,
        grid=(bsz, ns),
        in_specs=[pl.BlockSpec(memory_space=pltpu.SMEM),
                  pl.BlockSpec((SWA_QB * WINDOW, ATT_Q), cur),
                  pl.BlockSpec((WINDOW, ATT_KV), prev),
                  pl.BlockSpec((SWA_QB * WINDOW, ATT_KV), cur),
                  pl.BlockSpec((WINDOW, ATT_KV), prev),
                  pl.BlockSpec((SWA_QB * WINDOW, ATT_KV), cur),
                  pl.BlockSpec((1, ATT_Q), lambda b, n: (0, 0))],
        out_specs=pl.BlockSpec((SWA_QB * WINDOW, ATT_Q), cur),
        out_shape=jax.ShapeDtypeStruct((t, ATT_Q), BF16),
        scratch_shapes=[pltpu.VMEM((2, ATT_HEADS, WINDOW, 2 * WINDOW), F32)],
        compiler_params=_cparams(2),
        name="swa",
    )(sinks, qa, ka, ka, va, va, gn)


def _split3(a):
    hi = a.astype(BF16)
    r1 = a - hi.astype(F32)
    mid = r1.astype(BF16)
    lo = (r1 - mid.astype(F32)).astype(BF16)
    return hi, mid, lo


def _mlstm_kernel(qk_ref, v_ref, og_ref, gi_ref, gf_ref, cw_ref, cb_ref, bi_ref, bf_ref, gn_ref,
                  out_ref, xbuf, cn_s, m_s):
    c = pl.program_id(1)
    nseq, L = qk_ref.shape[0], qk_ref.shape[1]
    HD = ML_HEAD_DIM
    seqs = range(nseq)
    heads = range(ML_HEADS)
    pairs = [(s, h) for s in seqs for h in heads]
    sls = [slice(HD * h, HD * (h + 1)) for h in heads]

    @pl.when(c == 0)
    def _():
        xbuf[:, 0:8, :] = jnp.zeros((nseq, 8, xbuf.shape[2]), F32)
        cn_s[...] = jnp.zeros(cn_s.shape, F32)
        m_s[...] = jnp.zeros(m_s.shape, F32)

    ti = lax.broadcasted_iota(jnp.int32, (L, L), 0)
    si = lax.broadcasted_iota(jnp.int32, (L, L), 1)
    causal = si <= ti
    tril = jnp.where(causal, 1.0, 0.0).astype(BF16)
    ones_col = jnp.ones((L, HD), BF16)
    mean_mat = jnp.full((HD, HD), 1.0 / HD, BF16)

    gi, b = [], []
    for s in seqs:
        gi.append(gi_ref[s] + bi_ref[...])
        z = gf_ref[s] + bf_ref[...]
        lf = jnp.minimum(z, 0.0) - jnp.log1p(jnp.exp(-jnp.abs(z)))
        acc = jnp.zeros((L, LANES), F32)
        for part in _split3(lf):
            acc = acc + jnp.dot(tril, part, preferred_element_type=F32)
        b.append(acc)

    qb, kf = {}, {}
    for s in seqs:
        xbuf[s, 8:8 + L, :] = qk_ref[s]
        y = cb_ref[...] + cw_ref[CONV_WIDTH - 1:CONV_WIDTH, :] * xbuf[s, 8:8 + L, :]
        for j in range(CONV_WIDTH - 1):
            sh = CONV_WIDTH - 1 - j
            y = y + cw_ref[j:j + 1, :] * xbuf[s, 8 - sh:8 - sh + L, :]
        xbuf[s, 0:8, :] = xbuf[s, L:L + 8, :]
        qk = y * jax.nn.sigmoid(y)
        for h in heads:
            qb[s, h] = qk[:, sls[h]].astype(BF16)
            kf[s, h] = qk[:, ML_W + HD * h:ML_W + HD * (h + 1)] * (1.0 / math.sqrt(HD))
    kb = {p: kf[p].astype(BF16) for p in pairs}
    vext = {(s, h): jnp.concatenate([v_ref[s, :, sls[h]], ones_col], axis=1) for s, h in pairs}
    cn = {(s, h): cn_s[s * ML_HEADS + h] for s, h in pairs}
    qk_d = {p: lax.dot_general(qb[p], kb[p], _NT, preferred_element_type=F32) for p in pairs}
    qc = {p: jnp.dot(qb[p], cn[p].astype(BF16), preferred_element_type=F32) for p in pairs}

    m_inter, m_new, w, decay, rt = [], [], [], [], []
    for s in seqs:
        m_st = m_s[s]
        b_last = b[s][L - 1:L, :]
        m_inter.append(b[s] + m_st)
        log_w = b_last - b[s] + gi[s]
        m_new.append(jnp.maximum(b_last + m_st, jnp.max(log_w, axis=0, keepdims=True)))
        w.append(jnp.exp(log_w - m_new[s]))
        decay.append(jnp.exp(b_last + m_st - m_new[s]))
        rt.append(jnp.transpose(gi[s] - b[s]))

    s_b, kw_b, m_ts, a_inters = {}, {}, {}, {}
    for s, h in pairs:
        log_d = jnp.where(causal, b[s][:, h:h + 1] + rt[s][h:h + 1, :], NEG_INF)
        m_t = jnp.maximum(m_inter[s][:, h:h + 1], jnp.max(log_d, axis=1, keepdims=True))
        s_b[s, h] = (qk_d[s, h] * jnp.exp(log_d - m_t)).astype(BF16)
        kw_b[s, h] = (kf[s, h] * w[s][:, h:h + 1]).astype(BF16)
        m_ts[s, h] = m_t
        a_inters[s, h] = jnp.exp(m_inter[s][:, h:h + 1] - m_t)

    sv = {p: jnp.dot(s_b[p], vext[p], preferred_element_type=F32) for p in pairs}
    upd = {p: lax.dot_general(kw_b[p], vext[p], _TN, preferred_element_type=F32) for p in pairs}

    hm, sq = {}, {}
    for s, h in pairs:
        nd = sv[s, h] + a_inters[s, h] * qc[s, h]
        num = nd[:, :HD]
        den = nd[:, HD:]
        hh = num * (1.0 / jnp.maximum(jnp.abs(den), jnp.exp(-m_ts[s, h])))
        cn_s[s * ML_HEADS + h] = decay[s][:, h:h + 1] * cn[s, h] + upd[s, h]
        hm[s, h] = og_ref[s, :, sls[h]].astype(F32) * hh
        sq[s, h] = (hm[s, h] * hm[s, h]).astype(BF16)
    ms = {p: jnp.dot(sq[p], mean_mat, preferred_element_type=F32) for p in pairs}
    for s, h in pairs:
        y = hm[s, h] * lax.rsqrt(ms[s, h] + EPS) * gn_ref[:, sls[h]]
        out_ref[s, :, sls[h]] = y.astype(BF16)

    for s in seqs:
        m_s[s] = m_new[s]


def _mlstm(qk, vm, og, gi, gf, cw, cb, bi, bf, gn, bsz, seq):
    L = ML_CHUNK
    nc = seq // L
    nseq = ML_NB
    t = bsz * seq
    seq3 = lambda a: a.reshape(bsz, seq, a.shape[-1])
    row = lambda b, c: (b, c, 0)
    const = lambda b, c: (0, 0)
    out = pl.pallas_call(
        _mlstm_kernel,
        grid=(bsz // nseq, nc),
        in_specs=[pl.BlockSpec((nseq, L, 2 * ML_W), row),
                  pl.BlockSpec((nseq, L, ML_W), row),
                  pl.BlockSpec((nseq, L, ML_W), row),
                  pl.BlockSpec((nseq, L, LANES), row),
                  pl.BlockSpec((nseq, L, LANES), row),
                  pl.BlockSpec((CONV_WIDTH, 2 * ML_W), const),
                  pl.BlockSpec((1, 2 * ML_W), const),
                  pl.BlockSpec((1, LANES), const),
                  pl.BlockSpec((1, LANES), const),
                  pl.BlockSpec((1, ML_W), const)],
        out_specs=pl.BlockSpec((nseq, L, ML_W), row),
        out_shape=jax.ShapeDtypeStruct((bsz, seq, ML_W), BF16),
        scratch_shapes=[pltpu.VMEM((nseq, L + 8, 2 * ML_W), F32),
                        pltpu.VMEM((nseq * ML_HEADS, ML_HEAD_DIM, 2 * ML_HEAD_DIM), F32),
                        pltpu.VMEM((nseq, 1, LANES), F32)],
        compiler_params=_cparams(2),
        name="mlstm",
    )(seq3(qk), seq3(vm), seq3(og), seq3(gi), seq3(gf), cw, cb, bi, bf, gn)
    return out.reshape(t, ML_W)


def _kv_kernel(mem_ref, g_ref, w_ref, k_ref, v_ref):
    mn = _rms(mem_ref[...], g_ref[...]).astype(BF16)
    hw = k_ref.shape[1]
    k_ref[...] = jnp.dot(mn, w_ref[:, :hw], preferred_element_type=F32).astype(BF16)
    v_ref[...] = jnp.dot(mn, w_ref[:, hw:], preferred_element_type=F32).astype(BF16)


def _kv_prep(mem2d, g, w_ckv):
    r, d = mem2d.shape
    hw = w_ckv.shape[1] // 2
    tm = 256
    return pl.pallas_call(
        _kv_kernel,
        grid=(r // tm,),
        in_specs=[pl.BlockSpec((tm, d), lambda i: (i, 0)),
                  pl.BlockSpec((1, d), lambda i: (0, 0)),
                  pl.BlockSpec(w_ckv.shape, lambda i: (0, 0))],
        out_specs=[pl.BlockSpec((tm, hw), lambda i: (i, 0))] * 2,
        out_shape=[jax.ShapeDtypeStruct((r, hw), BF16)] * 2,
        compiler_params=_cparams(1),
        name="kv_prep",
    )(mem2d, g, w_ckv)


_R_E0, _R_E1, _R_RANK0, _R_RANK1, _R_P0, _R_P1 = range(6)
_ROUTER_LANE0 = N_GROUPS


_PACK_ROWS = 8


def _store_slabs(ref, v, rows, first=0):
    span = pl.ds(first * _PACK_ROWS, rows * _PACK_ROWS)
    ref[span, :] = v.reshape(rows * _PACK_ROWS, LANES).astype(ref.dtype)


def _load_slabs(ref, rows):
    return ref[...].astype(F32).reshape(rows, _PACK_ROWS * LANES)


def _store_slabs_staged(ref, v, rows, stage, first=0):
    base = first * _PACK_ROWS
    for r in range(_PACK_ROWS):
        stage[pl.ds(base + r, rows, stride=_PACK_ROWS), :] = v[:, LANES * r:LANES * (r + 1)]
    span = pl.ds(base, rows * _PACK_ROWS)
    ref[span, :] = stage[span, :].astype(ref.dtype)


def _load_slabs_staged(ref, rows, stage):
    stage[...] = ref[...].astype(F32)
    return jnp.concatenate(
        [stage[pl.ds(r, rows, stride=_PACK_ROWS), :] for r in range(_PACK_ROWS)], axis=1)


def _mid_kernel(x_ref, att_ref, hm_ref, wo_ref, gc_ref, wq_ref, k_ref, v_ref, wco_ref, gf_ref,
                wr_ref, br_ref, x2_ref, xn_ref, route_ref, idx_ref, cnt_ref, carry_s, stage_s):
    i = pl.program_id(0)
    tm = x_ref.shape[0] // MID_PARTS
    parts = range(MID_PARTS)
    rows = [pl.ds(p * tm, tm) for p in parts]

    @pl.when(i == 0)
    def _():
        carry_s[...] = jnp.zeros(carry_s.shape, F32)

    x1 = [x_ref[rows[p], :]
          + jnp.dot(att_ref[rows[p], :], wo_ref[0:ATT_Q, :], preferred_element_type=F32)
          + jnp.dot(hm_ref[rows[p], :], wo_ref[ATT_Q:, :], preferred_element_type=F32)
          for p in parts]

    xc = [_rms(x1[p], gc_ref[...]).astype(BF16) for p in parts]
    qb = [jnp.dot(xc[p], wq_ref[...], preferred_element_type=F32).astype(BF16) for p in parts]
    sls = [slice(X_HEAD_DIM * h, X_HEAD_DIM * (h + 1)) for h in range(X_HEADS)]
    sc = [[lax.dot_general(qb[p][:, sl], k_ref[:, sl], _NT, preferred_element_type=F32)
           for sl in sls] for p in parts]
    es = [[None] * X_HEADS for _ in parts]
    rinv = [[None] * X_HEADS for _ in parts]
    for p in parts:
        for h in range(X_HEADS):
            s = sc[p][h] * (1.0 / math.sqrt(X_HEAD_DIM))
            e = jnp.exp(s - jnp.max(s, axis=-1, keepdims=True))
            es[p][h] = e.astype(BF16)
            rinv[p][h] = 1.0 / jnp.sum(e, axis=-1, keepdims=True)
    o = [jnp.concatenate(
        [jnp.dot(es[p][h], v_ref[:, sls[h]], preferred_element_type=F32) * rinv[p][h]
         for h in range(X_HEADS)], axis=1).astype(BF16) for p in parts]
    x2 = [x1[p] + jnp.dot(o[p], wco_ref[...], preferred_element_type=F32) for p in parts]

    logits_p = []
    for p in parts:
        x2_ref[rows[p], :] = x2[p]
        xn = _rms(x2[p], gf_ref[...])
        xh = xn.astype(BF16)
        _store_slabs_staged(xn_ref, xn, tm, stage_s, first=p * tm)
        xl = (xn - xh.astype(F32)).astype(BF16)
        lg2 = jnp.dot(xh, wr_ref[...], preferred_element_type=F32)
        logits_p.append(lg2[:, :LANES] + lg2[:, LANES:]
                        + jnp.dot(xl, wr_ref[:, :LANES], preferred_element_type=F32) + br_ref[...])

    for p in parts:
        _route_part(logits_p[p], rows[p], p * tm, route_ref, idx_ref, cnt_ref, carry_s)


def _route_part(logits, rows, first, route_ref, idx_ref, cnt_ref, carry_s):
    tm = logits.shape[0]
    lane = lax.broadcasted_iota(jnp.int32, (tm, LANES), 1)
    lanef = lane.astype(F32)
    big = float(4 * LANES)
    gl = jnp.where(lane < N_GROUPS, logits, NEG_INF)
    gmax = jnp.max(gl, axis=-1, keepdims=True)
    gsel = jnp.min(jnp.where(gl == gmax, lanef, big), axis=-1, keepdims=True)
    gw = 1.0 / jnp.sum(jnp.exp(gl - gmax), axis=-1, keepdims=True)
    lo_lane = _ROUTER_LANE0 + EXPERTS_PER_GROUP * gsel
    in_group = (lanef >= lo_lane) & (lanef < lo_lane + EXPERTS_PER_GROUP)
    el = jnp.where(in_group, logits, NEG_INF)
    v0 = jnp.max(el, axis=-1, keepdims=True)
    i0 = jnp.min(jnp.where(el == v0, lanef, big), axis=-1, keepdims=True)
    el2 = jnp.where(lanef == i0, NEG_INF, el)
    v1 = jnp.max(el2, axis=-1, keepdims=True)
    i1 = jnp.min(jnp.where(el2 == v1, lanef, big), axis=-1, keepdims=True)
    tt = jnp.exp(v1 - v0)
    p0 = gw / (1.0 + tt)
    p1 = gw * tt / (1.0 + tt)

    sel0 = lanef == i0
    sel1 = lanef == i1
    mb = jnp.where(sel0 | sel1, 1.0, 0.0).astype(BF16)
    ti = lax.broadcasted_iota(jnp.int32, (tm, tm), 0)
    si = lax.broadcasted_iota(jnp.int32, (tm, tm), 1)
    strict = jnp.where(si < ti, 1.0, 0.0).astype(BF16)
    carry = carry_s[...]
    pref = jnp.dot(strict, mb, preferred_element_type=F32) + carry
    r0 = jnp.sum(jnp.where(sel0, pref, 0.0), axis=-1, keepdims=True)
    r1 = jnp.sum(jnp.where(sel1, pref, 0.0), axis=-1, keepdims=True)
    carry = carry + jnp.sum(mb.astype(F32), axis=0, keepdims=True)
    carry_s[...] = carry
    cnt_ref[...] = carry

    route = jnp.zeros((tm, LANES), F32)
    for idx, col in ((_R_E0, i0 - _ROUTER_LANE0), (_R_E1, i1 - _ROUTER_LANE0), (_R_RANK0, r0),
                     (_R_RANK1, r1), (_R_P0, p0), (_R_P1, p1)):
        route = jnp.where(lane == idx, col, route)
    route_ref[rows, :] = route
    idx_ref[:, pl.ds(first, tm)] = jnp.transpose(route)[0:8, :].astype(jnp.int32)


def _mid(x2d, att, hm, wo, gc, wq, kmem, vmem_, wco, gf, wr, br, seq):
    t, d = x2d.shape
    tm = TM_MID
    per_b = seq // tm
    rowmap = lambda i: (i, 0)
    const = lambda i: (0, 0)
    bmap = lambda i: (i // per_b, 0)
    mem_len = kmem.shape[0] // (t // seq)
    return pl.pallas_call(
        _mid_kernel,
        grid=(t // tm,),
        in_specs=[pl.BlockSpec((tm, d), rowmap),
                  pl.BlockSpec((tm, ATT_Q), rowmap),
                  pl.BlockSpec((tm, ML_W), rowmap),
                  pl.BlockSpec(wo.shape, const),
                  pl.BlockSpec((1, d), const),
                  pl.BlockSpec(wq.shape, const),
                  pl.BlockSpec((mem_len, kmem.shape[1]), bmap),
                  pl.BlockSpec((mem_len, vmem_.shape[1]), bmap),
                  pl.BlockSpec(wco.shape, const),
                  pl.BlockSpec((1, d), const),
                  pl.BlockSpec(wr.shape, const),
                  pl.BlockSpec((1, LANES), const)],
        out_specs=[pl.BlockSpec((tm, d), rowmap),
                   pl.BlockSpec((tm * _PACK_ROWS, LANES), rowmap),
                   pl.BlockSpec((tm, LANES), rowmap),
                   pl.BlockSpec((8, tm), lambda i: (0, i)),
                   pl.BlockSpec((1, LANES), const)],
        out_shape=[jax.ShapeDtypeStruct((t, d), F32),
                   jax.ShapeDtypeStruct((t * _PACK_ROWS, LANES), BF16),
                   jax.ShapeDtypeStruct((t, LANES), F32),
                   jax.ShapeDtypeStruct((8, t), jnp.int32),
                   jax.ShapeDtypeStruct((1, LANES), F32)],
        scratch_shapes=[pltpu.VMEM((1, LANES), F32),
                        pltpu.VMEM((tm * _PACK_ROWS, LANES), F32)],
        compiler_params=_cparams(1),
        name="mid",
    )(x2d, att, hm, wo, gc, wq, kmem, vmem_, wco, gf, wr, br)


_M_BLOCK_E, _M_PAD_END, _M_NUSED = range(3)
_META_LANES = 2 * LANES


def _plan_kernel(cnt_ref, idx_ref, dest_ref, meta_ref):
    cnt = cnt_ref[...]
    lane = lax.broadcasted_iota(jnp.int32, (1, LANES), 1)
    is_expert = (lane >= _ROUTER_LANE0) & (lane < _ROUTER_LANE0 + N_EXPERTS)
    nblk = jnp.where(is_expert, jnp.floor((cnt + (BM - 1)) * (1.0 / BM)), 0.0)
    jj = lax.broadcasted_iota(jnp.int32, (LANES, LANES), 0)
    kk = lax.broadcasted_iota(jnp.int32, (LANES, LANES), 1)
    upper = jnp.where(jj <= kk, 1.0, 0.0).astype(BF16)
    pend_blk = jnp.dot(jnp.broadcast_to(nblk, (8, LANES)).astype(BF16), upper,
                       preferred_element_type=F32)[0:1, :]
    pstart_rows = (pend_blk - nblk) * BM
    pend_rows = pend_blk * BM

    idx = idx_ref[...]
    off = jnp.zeros(idx.shape, F32)
    for e in range(N_EXPERTS):
        lane_e = _ROUTER_LANE0 + e
        off = jnp.where(idx == e, pstart_rows[:, lane_e:lane_e + 1], off)
    ranks = pltpu.roll(idx, idx.shape[0] - 2, axis=0)
    dest_ref[...] = ranks + off.astype(jnp.int32)

    blk = lax.broadcasted_iota(jnp.int32, (1, _META_LANES), 1).astype(F32)
    block_e = jnp.zeros((1, _META_LANES), F32)
    for e in range(N_EXPERTS):
        lane_e = _ROUTER_LANE0 + e
        block_e = block_e + jnp.where(pend_blk[:, lane_e:lane_e + 1] <= blk, 1.0, 0.0)
    block_e = jnp.minimum(block_e, N_EXPERTS - 1.0)
    last = _ROUTER_LANE0 + N_EXPERTS - 1
    nused = pend_blk[:, last:last + 1]
    pend_wide = jnp.concatenate([pend_rows, jnp.zeros((1, _META_LANES - LANES), F32)], axis=1)
    sub = lax.broadcasted_iota(jnp.int32, (8, _META_LANES), 0)
    meta = jnp.where(sub == _M_BLOCK_E, block_e, jnp.where(sub == _M_PAD_END, pend_wide, nused))
    meta_ref[...] = meta.astype(jnp.int32)


def _plan(cnt, idx):
    t = idx.shape[1]
    return pl.pallas_call(
        _plan_kernel,
        grid=(1,),
        in_specs=[pl.BlockSpec(cnt.shape, lambda i: (0, 0)),
                  pl.BlockSpec(idx.shape, lambda i: (0, 0))],
        out_specs=[pl.BlockSpec((8, t), lambda i: (0, 0)),
                   pl.BlockSpec((8, _META_LANES), lambda i: (0, 0))],
        out_shape=[jax.ShapeDtypeStruct((8, t), jnp.int32),
                   jax.ShapeDtypeStruct((8, _META_LANES), jnp.int32)],
        compiler_params=_cparams(1),
        name="plan",
    )(cnt, idx)


_SLAB = BM * _PACK_ROWS


def _dispatch_kernel(meta_ref, d_ref, xn_ref, xpad_ref, zbuf, sem, zsem):
    i = pl.program_id(0)
    td = xn_ref.shape[0] // _PACK_ROWS

    def zero_copy(e):
        pend = meta_ref[_M_PAD_END, _ROUTER_LANE0 + e]
        pstart = meta_ref[_M_PAD_END, _ROUTER_LANE0 + e - 1]
        first = pl.multiple_of((pend - BM) * _PACK_ROWS, _SLAB)
        cp = pltpu.make_async_copy(zbuf, xpad_ref.at[pl.ds(first, _SLAB), :], zsem)
        return pend > pstart, cp

    @pl.when(i == 0)
    def _():
        zbuf[...] = jnp.zeros(zbuf.shape, zbuf.dtype)
        for e in range(N_EXPERTS):
            nonempty, cp = zero_copy(e)
            pl.when(nonempty)(cp.start)
        for e in range(N_EXPERTS):
            nonempty, cp = zero_copy(e)
            pl.when(nonempty)(cp.wait)

        def tail_copy(b):
            return pltpu.make_async_copy(
                zbuf, xpad_ref.at[pl.ds(pl.multiple_of(b * _SLAB, _SLAB), _SLAB), :], zsem)

        first_unused = meta_ref[_M_NUSED, 0]
        n_blocks = xpad_ref.shape[0] // _SLAB
        lax.fori_loop(first_unused, n_blocks, lambda b, c: (tail_copy(b).start(), c)[1], 0)
        lax.fori_loop(first_unused, n_blocks, lambda b, c: (tail_copy(b).wait(), c)[1], 0)

    def row_copy(t, dst):
        src = pl.ds(pl.multiple_of(t * _PACK_ROWS, _PACK_ROWS), _PACK_ROWS)
        dsl = pl.ds(pl.multiple_of(dst * _PACK_ROWS, _PACK_ROWS), _PACK_ROWS)
        return pltpu.make_async_copy(xn_ref.at[src, :], xpad_ref.at[dsl, :], sem)

    def start(t, carry):
        row_copy(t, d_ref[0, t]).start(priority=0)
        row_copy(t, d_ref[1, t]).start(priority=1)
        return carry

    def wait(t, carry):
        row_copy(0, 0).wait()
        row_copy(0, 0).wait()
        return carry

    lax.fori_loop(0, td, start, 0, unroll=8)
    lax.fori_loop(0, td, wait, 0, unroll=8)


def _dispatch(meta, dest, xn_packed, rows):
    t = dest.shape[1]
    td = TD
    return pl.pallas_call(
        _dispatch_kernel,
        grid_spec=pltpu.PrefetchScalarGridSpec(
            num_scalar_prefetch=1,
            grid=(t // td,),
            in_specs=[pl.BlockSpec((8, td), lambda i, m: (0, i), memory_space=pltpu.SMEM),
                      pl.BlockSpec((td * _PACK_ROWS, LANES), lambda i, m: (i, 0))],
            out_specs=pl.BlockSpec(memory_space=pl.ANY),
            scratch_shapes=[pltpu.VMEM((_SLAB, LANES), BF16),
                            pltpu.SemaphoreType.DMA(()),
                            pltpu.SemaphoreType.DMA(())]),
        out_shape=jax.ShapeDtypeStruct((rows * _PACK_ROWS, LANES), BF16),
        compiler_params=_cparams(1),
        name="dispatch",
    )(meta, dest, xn_packed)


def _expert_kernel(meta_ref, xpad_ref, wg_ref, wu_ref, wd_ref, ypad_ref, wg_s, wu_s, wd_s,
                   xbuf, ybuf, xsem, ysem):
    e = pl.program_id(0)
    nused = meta_ref[_M_NUSED, 0]
    lane_e = _ROUTER_LANE0 + e
    first_blk = lax.div(meta_ref[_M_PAD_END, lane_e - 1], BM)
    n_blk = lax.div(meta_ref[_M_PAD_END, lane_e], BM) - first_blk

    def block_rows(g):
        return pl.ds(pl.multiple_of(g * _SLAB, _SLAB), _SLAB)

    def x_copy(g, slot):
        return pltpu.make_async_copy(xpad_ref.at[block_rows(g), :], xbuf.at[slot], xsem.at[slot])

    def y_copy(g, slot):
        return pltpu.make_async_copy(ybuf.at[slot], ypad_ref.at[block_rows(g), :], ysem.at[slot])

    nx = xbuf.shape[0]
    ahead = nx - 1

    @pl.when(e == 0)
    def _():
        for g0 in range(ahead):
            @pl.when(g0 < nused)
            def _():
                x_copy(g0, g0).start(priority=1)

    @pl.when(n_blk > 0)
    def _():
        wg_s[...] = wg_ref[...].astype(BF16)
        wu_s[...] = wu_ref[...].astype(BF16)
        wd_s[...] = wd_ref[...].astype(BF16)

    def body(j, carry):
        g = first_blk + j
        xslot = lax.rem(g, nx)
        yslot = lax.rem(g, 2)
        x_copy(g, xslot).wait()

        @pl.when(g + ahead < nused)
        def _():
            x_copy(g + ahead, lax.rem(g + ahead, nx)).start(priority=1)

        xb = _load_slabs(xbuf.at[xslot], BM).astype(BF16)
        gate = jnp.dot(xb, wg_s[...], preferred_element_type=F32)
        up = jnp.dot(xb, wu_s[...], preferred_element_type=F32)
        hb = (gate * jax.nn.sigmoid(gate) * up).astype(BF16)
        y = jnp.dot(hb, wd_s[...], preferred_element_type=F32)

        @pl.when(g >= 2)
        def _():
            y_copy(g - 2, yslot).wait()

        _store_slabs(ybuf.at[yslot], y, BM)
        y_copy(g, yslot).start()
        return carry

    lax.fori_loop(0, n_blk, body, 0)

    @pl.when(e == pl.num_programs(0) - 1)
    def _():
        for back in (2, 1):
            @pl.when(nused >= back)
            def _():
                y_copy(nused - back, lax.rem(nused - back, 2)).wait()

        ybuf[0] = jnp.zeros(ybuf.shape[1:], ybuf.dtype)
        n_blocks = ypad_ref.shape[0] // _SLAB
        lax.fori_loop(nused, n_blocks, lambda b, c: (y_copy(b, 0).start(), c)[1], 0)
        lax.fori_loop(nused, n_blocks, lambda b, c: (y_copy(b, 0).wait(), c)[1], 0)


def _experts(meta, xpad, wg, wu, wd):
    n_exp, d, de = wg.shape
    wmap = lambda e, m: (e, 0, 0)
    return pl.pallas_call(
        _expert_kernel,
        grid_spec=pltpu.PrefetchScalarGridSpec(
            num_scalar_prefetch=1,
            grid=(n_exp,),
            in_specs=[pl.BlockSpec(memory_space=pl.ANY),
                      pl.BlockSpec((None, d, de), wmap),
                      pl.BlockSpec((None, d, de), wmap),
                      pl.BlockSpec((None, de, d), wmap)],
            out_specs=pl.BlockSpec(memory_space=pl.ANY),
            scratch_shapes=[pltpu.VMEM((d, de), BF16),
                            pltpu.VMEM((d, de), BF16),
                            pltpu.VMEM((de, d), BF16),
                            pltpu.VMEM((EXPERT_AHEAD + 1, _SLAB, LANES), BF16),
                            pltpu.VMEM((2, _SLAB, LANES), BF16),
                            pltpu.SemaphoreType.DMA((EXPERT_AHEAD + 1,)),
                            pltpu.SemaphoreType.DMA((2,))]),
        out_shape=jax.ShapeDtypeStruct(xpad.shape, BF16),
        compiler_params=_cparams(1),
        name="experts",
    )(meta, xpad, wg, wu, wd)


def _combine_kernel(dc_ref, dn_ref, x2_ref, route_ref, g_ref, ypad_ref, out_ref, ybuf, sem,
                    stage0_s, stage1_s):
    i = pl.program_id(0)
    n = pl.num_programs(0)
    tc = x2_ref.shape[0]

    def row_copy(src, slot, which, t):
        ssl = pl.ds(pl.multiple_of(src * _PACK_ROWS, _PACK_ROWS), _PACK_ROWS)
        dsl = pl.ds(pl.multiple_of(t * _PACK_ROWS, _PACK_ROWS), _PACK_ROWS)
        return pltpu.make_async_copy(ypad_ref.at[ssl, :], ybuf.at[slot, which, dsl, :],
                                     sem.at[slot])

    def issue(d_ref, slot):
        def body(t, carry):
            row_copy(d_ref[0, t], slot, 0, t).start(priority=0)
            row_copy(d_ref[1, t], slot, 1, t).start(priority=1)
            return carry
        lax.fori_loop(0, tc, body, 0, unroll=8)

    slot = i % 2

    @pl.when(i == 0)
    def _():
        issue(dc_ref, 0)

    @pl.when(i + 1 < n)
    def _():
        issue(dn_ref, 1 - slot)

    def wait(t, carry):
        row_copy(0, slot, 0, 0).wait()
        row_copy(0, slot, 1, 0).wait()
        return carry
    lax.fori_loop(0, tc, wait, 0, unroll=8)

    route = route_ref[...]
    p0 = route[:, _R_P0:_R_P0 + 1]
    p1 = route[:, _R_P1:_R_P1 + 1]
    y0 = _load_slabs_staged(ybuf.at[slot, 0], tc, stage0_s)
    y1 = _load_slabs_staged(ybuf.at[slot, 1], tc, stage1_s)
    x3 = x2_ref[...] + p0 * y0 + p1 * y1
    out_ref[...] = _rms(x3, g_ref[...])


def _combine(dest, x2, route, g, ypad):
    t, d = x2.shape
    tc = TC
    nt = t // tc
    cur = pl.BlockSpec((8, tc), lambda i: (0, i), memory_space=pltpu.SMEM)
    nxt = pl.BlockSpec((8, tc), lambda i: (0, jnp.minimum(i + 1, nt - 1)),
                       memory_space=pltpu.SMEM)
    return pl.pallas_call(
        _combine_kernel,
        grid=(nt,),
        in_specs=[cur, nxt,
                  pl.BlockSpec((tc, d), lambda i: (i, 0)),
                  pl.BlockSpec((tc, LANES), lambda i: (i, 0)),
                  pl.BlockSpec((1, d), lambda i: (0, 0)),
                  pl.BlockSpec(memory_space=pl.ANY)],
        out_specs=pl.BlockSpec((tc, d), lambda i: (i, 0)),
        out_shape=jax.ShapeDtypeStruct((t, d), F32),
        scratch_shapes=[pltpu.VMEM((2, 2, tc * _PACK_ROWS, LANES), BF16),
                        pltpu.SemaphoreType.DMA((2,)),
                        pltpu.VMEM((tc * _PACK_ROWS, LANES), F32),
                        pltpu.VMEM((tc * _PACK_ROWS, LANES), F32)],
        compiler_params=_cparams(1),
        name="combine",
    )(dest, dest, x2, route, g, ypad)


def _pad_cols(a, width):
    return jnp.pad(a, ((0, 0), (0, width - a.shape[1])))


def kernel(x, mem, norm_mix, w_in, b_gates, conv_w, conv_b, att_sinks, norm_att_out, norm_ml_out,
           w_out, norm_cross, norm_mem, w_cq, w_ckv, w_co, norm_ffn, w_router_group,
           b_router_group, w_router_expert, b_router_expert, w_e_gate, w_e_up, w_e_down,
           norm_final):
    bsz, seq, d = x.shape
    t = bsz * seq
    depth = w_in.shape[0]
    assert depth == 1, "the final RMSNorm is fused into the last layer's combine kernel"
    xs = x.reshape(t, d)
    mem2d = mem.reshape(-1, d)

    for l in range(depth):
        bi = _pad_cols(b_gates[l][None, :ML_HEADS], LANES)
        bf = _pad_cols(b_gates[l][None, ML_HEADS:], LANES)
        w_r = _pad_cols(jnp.concatenate([w_router_group[l], w_router_expert[l]], axis=1), LANES)
        w_r_hi = w_r.astype(BF16)
        w_r_lo = (w_r - w_r_hi.astype(F32)).astype(BF16)
        w_r_packed = jnp.concatenate([w_r_hi, w_r_lo], axis=1)
        b_r = _pad_cols(jnp.concatenate([b_router_group[l], b_router_expert[l]])[None, :], LANES)

        qa, ka, va, qk, vm, og, gi, gf = _in_proj(xs, norm_mix[l][None, :], w_in, l)
        att = _swa(qa, ka, va, att_sinks[l], norm_att_out[l][None, :], bsz, seq)
        hm = _mlstm(qk, vm, og, gi, gf, conv_w[l], conv_b[l][None, :], bi, bf,
                    norm_ml_out[l][None, :], bsz, seq)

        kmem, vmem_ = _kv_prep(mem2d, norm_mem[l][None, :], w_ckv[l].astype(BF16))
        x2, xn, route, idx, cnt = _mid(xs, att, hm, w_out[l].astype(BF16), norm_cross[l][None, :],
                                       w_cq[l].astype(BF16), kmem, vmem_, w_co[l].astype(BF16),
                                       norm_ffn[l][None, :], w_r_packed, b_r, seq)

        nbk = (t * TOP_K) // BM + N_EXPERTS
        assert nbk <= _META_LANES
        dest, meta = _plan(cnt, idx)
        xpad = _dispatch(meta, dest, xn, nbk * BM)
        ypad = _experts(meta, xpad, w_e_gate[l], w_e_up[l], w_e_down[l])
        xs = _combine(dest, x2, route, norm_final[None, :], ypad)
    return xs.reshape(bsz, seq, d)
```

```python
import functools
import math

import jax
import jax.numpy as jnp
from jax import lax
from jax.experimental import pallas as pl
from jax.experimental.pallas import tpu as pltpu

F32 = jnp.float32
BF16 = jnp.bfloat16
EPS = 1e-6
NEG_INF = float("-inf")

ATT_HEADS = 8
ATT_HEAD_DIM = 64
ATT_Q = ATT_HEADS * ATT_HEAD_DIM
ATT_KV = 128
WINDOW = 128
ML_HEADS = 4
ML_HEAD_DIM = 128
ML_W = ML_HEADS * ML_HEAD_DIM
CONV_WIDTH = 4
X_HEADS = 4
X_HEAD_DIM = 128
N_GROUPS = 4
EXPERTS_PER_GROUP = 8
N_EXPERTS = N_GROUPS * EXPERTS_PER_GROUP
TOP_K = 2

LANES = 128
TM_IN = 512
IN_PARTS = 2
SWA_QB = 2
ML_CHUNK = 256
ML_NB = 1
TM_MID = 1024
MID_PARTS = 4
TD = 512
TC = 256
BM = 256
EXPERT_AHEAD = 3
VMEM_LIMIT = 56 * 1024 * 1024

_NT = (((1,), (1,)), ((), ()))
_TN = (((0,), (0,)), ((), ()))


def _rms(x, g):
    return x * lax.rsqrt(jnp.mean(x * x, axis=-1, keepdims=True) + EPS) * g


def _cparams(n_axes):
    return pltpu.CompilerParams(dimension_semantics=("arbitrary",) * n_axes,
                                vmem_limit_bytes=VMEM_LIMIT)


_SEG = {"qa": (0, 512), "ka": (512, 640), "va": (640, 768), "qk": (768, 1792),
        "vm": (1792, 2304), "om": (2304, 2816), "gi": (2816, 2944), "gf": (2944, 3072)}


_W_CHUNK = 256


def _in_proj_kernel(x_ref, g_ref, win_ref, qa_ref, ka_ref, va_ref, qk_ref, vm_ref, og_ref,
                    gi_ref, gf_ref, w_ref):
    @pl.when(pl.program_id(0) == 0)
    def _():
        n_main = _SEG["om"][1]
        for lo in range(0, n_main, _W_CHUNK):
            w_ref[:, lo:lo + _W_CHUNK] = jnp.transpose(win_ref[lo:lo + _W_CHUNK, :]).astype(BF16)
        gates = win_ref[n_main:n_main + 2 * ML_HEADS, :]
        row = lax.broadcasted_iota(jnp.int32, gates.shape, 0)
        pad = jnp.zeros((LANES - 2 * ML_HEADS, gates.shape[1]), F32)
        gi_rows = jnp.where(row < ML_HEADS, gates, 0.0)
        gf_rows = jnp.where(row < ML_HEADS, pltpu.roll(gates, ML_HEADS, axis=0), 0.0)
        for name, blk in (("gi", gi_rows), ("gf", gf_rows)):
            lo, hi = _SEG[name]
            w_ref[:, lo:hi] = jnp.transpose(jnp.concatenate([blk, pad], axis=0)).astype(BF16)

    tm = x_ref.shape[0] // IN_PARTS
    rows = [pl.ds(p * tm, tm) for p in range(IN_PARTS)]
    hb = [_rms(x_ref[r, :], g_ref[...]).astype(BF16) for r in rows]

    def seg(p, name):
        lo, hi = _SEG[name]
        return jnp.dot(hb[p], w_ref[:, lo:hi], preferred_element_type=F32)

    plain = {"qa": qa_ref, "ka": ka_ref, "va": va_ref, "qk": qk_ref, "vm": vm_ref, "gi": gi_ref,
             "gf": gf_ref}
    for p, r in enumerate(rows):
        for name, ref in plain.items():
            ref[r, :] = seg(p, name).astype(ref.dtype)
        og_ref[r, :] = jax.nn.sigmoid(seg(p, "om")).astype(og_ref.dtype)


def _in_proj(x2d, g, w_in_t, layer):
    t, d = x2d.shape
    tm = TM_IN
    widths = [(512, BF16), (128, BF16), (128, BF16), (1024, F32), (512, BF16), (512, BF16),
              (128, F32), (128, F32)]
    const = lambda i: (0, 0)
    return pl.pallas_call(
        _in_proj_kernel,
        grid=(t // tm,),
        in_specs=[pl.BlockSpec((tm, d), lambda i: (i, 0)),
                  pl.BlockSpec((1, d), const),
                  pl.BlockSpec((None,) + w_in_t.shape[1:], lambda i: (layer, 0, 0),
                               pipeline_mode=pl.Buffered(1))],
        out_specs=[pl.BlockSpec((tm, w), lambda i: (i, 0)) for w, _ in widths],
        out_shape=[jax.ShapeDtypeStruct((t, w), dt) for w, dt in widths],
        scratch_shapes=[pltpu.VMEM((d, _SEG["gf"][1]), BF16)],
        compiler_params=_cparams(1),
        name="in_proj",
    )(x2d, g, w_in_t)


def _swa_kernel(sink_ref, q_ref, kp_ref, kc_ref, vp_ref, vc_ref, gn_ref, o_ref, bias_s):
    b = pl.program_id(0)
    n = pl.program_id(1)

    @pl.when((b == 0) & (n == 0))
    def _():
        ti = lax.broadcasted_iota(jnp.int32, (WINDOW, 2 * WINDOW), 0)
        si = lax.broadcasted_iota(jnp.int32, (WINDOW, 2 * WINDOW), 1)
        dist = ti + WINDOW - si
        band = (dist >= 0) & (dist < WINDOW)
        distf = dist.astype(F32)
        for hidx in range(ATT_HEADS):
            slope = 2.0 ** (-8.0 * (hidx + 1) / ATT_HEADS)
            full = jnp.where(band, -slope * distf, NEG_INF)
            bias_s[1, hidx] = full
            bias_s[0, hidx] = jnp.where(si >= WINDOW, full, NEG_INF)

    scale = 1.0 / math.sqrt(ATT_HEAD_DIM)
    k_all = jnp.concatenate([kp_ref[...], kc_ref[...]], axis=0).astype(F32) * scale
    v_all = jnp.concatenate([vp_ref[...], vc_ref[...]], axis=0).astype(F32)
    lane = lax.broadcasted_iota(jnp.int32, k_all.shape, 1)
    low = lane < ATT_HEAD_DIM

    def lane_variants(a):
        ar = pltpu.roll(a, ATT_HEAD_DIM, axis=1)
        z = jnp.zeros_like(a)
        return ((jnp.where(low, a, z).astype(BF16), jnp.where(low, z, ar).astype(BF16)),
                (jnp.where(low, ar, z).astype(BF16), jnp.where(low, z, a).astype(BF16)))

    k_ops = lane_variants(k_all)
    v_ops = lane_variants(v_all)
    n_tiles = ATT_Q // LANES
    kv_of = lambda j: (2 * j) // (ATT_HEADS // 2)
    blocks = range(SWA_QB)
    rows = [slice(WINDOW * i, WINDOW * (i + 1)) for i in blocks]
    keys = [slice(WINDOW * i, WINDOW * (i + 2)) for i in blocks]
    bias_slot = [jnp.where(n == 0, 0, 1) if i == 0 else 1 for i in blocks]

    scores = [[[lax.dot_general(q_ref[rows[i], LANES * j:LANES * (j + 1)],
                                k_ops[kv_of(j)][par][keys[i]], _NT, preferred_element_type=F32)
                for par in range(2)] for j in range(n_tiles)] for i in blocks]
    probs = [[[None, None] for _ in range(n_tiles)] for _ in blocks]
    rinv = [[[None, None] for _ in range(n_tiles)] for _ in blocks]
    for i in blocks:
        for j in range(n_tiles):
            for par in range(2):
                hidx = 2 * j + par
                logits = scores[i][j][par] + bias_s[bias_slot[i], hidx]
                sink = sink_ref[hidx]
                mx = jnp.maximum(jnp.max(logits, axis=-1, keepdims=True), sink)
                e = jnp.exp(logits - mx)
                den = jnp.sum(e, axis=-1, keepdims=True) + jnp.exp(sink - mx)
                probs[i][j][par] = e.astype(BF16)
                rinv[i][j][par] = 1.0 / den
    lane_o = lax.broadcasted_iota(jnp.int32, (WINDOW, LANES), 1)
    for i in blocks:
        tiles = []
        for j in range(n_tiles):
            acc = (jnp.dot(probs[i][j][0], v_ops[kv_of(j)][0][keys[i]], preferred_element_type=F32)
                   + jnp.dot(probs[i][j][1], v_ops[kv_of(j)][1][keys[i]],
                             preferred_element_type=F32))
            tiles.append(acc * jnp.where(lane_o < ATT_HEAD_DIM, rinv[i][j][0], rinv[i][j][1]))
        att = jnp.concatenate(tiles, axis=1)
        o_ref[rows[i], :] = _rms(att, gn_ref[...]).astype(BF16)


def _swa(qa, ka, va, sinks, gn, bsz, seq):
    nb = seq // WINDOW
    ns = nb // SWA_QB
    t = bsz * seq
    cur = lambda b, n: (b * ns + n, 0)
    prev = lambda b, n: (b * nb + jnp.maximum(n * SWA_QB - 1, 0), 0)
    return pl.pallas_call(
        _swa_kernel,
        grid=(bsz, ns),
        in_specs=[pl.BlockSpec(memory_space=pltpu.SMEM),
                  pl.BlockSpec((SWA_QB * WINDOW, ATT_Q), cur),
                  pl.BlockSpec((WINDOW, ATT_KV), prev),
                  pl.BlockSpec((SWA_QB * WINDOW, ATT_KV), cur),
                  pl.BlockSpec((WINDOW, ATT_KV), prev),
                  pl.BlockSpec((SWA_QB * WINDOW, ATT_KV), cur),
                  pl.BlockSpec((1, ATT_Q), lambda b, n: (0, 0))],
        out_specs=pl.BlockSpec((SWA_QB * WINDOW, ATT_Q), cur),
        out_shape=jax.ShapeDtypeStruct((t, ATT_Q), BF16),
        scratch_shapes=[pltpu.VMEM((2, ATT_HEADS, WINDOW, 2 * WINDOW), F32)],
        compiler_params=_cparams(2),
        name="swa",
    )(sinks, qa, ka, ka, va, va, gn)


def _split3(a):
    hi = a.astype(BF16)
    r1 = a - hi.astype(F32)
    mid = r1.astype(BF16)
    lo = (r1 - mid.astype(F32)).astype(BF16)
    return hi, mid, lo


def _mlstm_kernel(qk_ref, v_ref, og_ref, gi_ref, gf_ref, cw_ref, cb_ref, bi_ref, bf_ref, gn_ref,
                  out_ref, xbuf, cn_s, m_s):
    c = pl.program_id(1)
    nseq, L = qk_ref.shape[0], qk_ref.shape[1]
    HD = ML_HEAD_DIM
    seqs = range(nseq)
    heads = range(ML_HEADS)
    pairs = [(s, h) for s in seqs for h in heads]
    sls = [slice(HD * h, HD * (h + 1)) for h in heads]

    @pl.when(c == 0)
    def _():
        xbuf[:, 0:8, :] = jnp.zeros((nseq, 8, xbuf.shape[2]), F32)
        cn_s[...] = jnp.zeros(cn_s.shape, F32)
        m_s[...] = jnp.zeros(m_s.shape, F32)

    ti = lax.broadcasted_iota(jnp.int32, (L, L), 0)
    si = lax.broadcasted_iota(jnp.int32, (L, L), 1)
    causal = si <= ti
    tril = jnp.where(causal, 1.0, 0.0).astype(BF16)
    ones_col = jnp.ones((L, HD), BF16)
    mean_mat = jnp.full((HD, HD), 1.0 / HD, BF16)

    gi, b = [], []
    for s in seqs:
        gi.append(gi_ref[s] + bi_ref[...])
        z = gf_ref[s] + bf_ref[...]
        lf = jnp.minimum(z, 0.0) - jnp.log1p(jnp.exp(-jnp.abs(z)))
        acc = jnp.zeros((L, LANES), F32)
        for part in _split3(lf):
            acc = acc + jnp.dot(tril, part, preferred_element_type=F32)
        b.append(acc)

    qb, kf = {}, {}
    for s in seqs:
        xbuf[s, 8:8 + L, :] = qk_ref[s]
        y = cb_ref[...] + cw_ref[CONV_WIDTH - 1:CONV_WIDTH, :] * xbuf[s, 8:8 + L, :]
        for j in range(CONV_WIDTH - 1):
            sh = CONV_WIDTH - 1 - j
            y = y + cw_ref[j:j + 1, :] * xbuf[s, 8 - sh:8 - sh + L, :]
        xbuf[s, 0:8, :] = xbuf[s, L:L + 8, :]
        qk = y * jax.nn.sigmoid(y)
        for h in heads:
            qb[s, h] = qk[:, sls[h]].astype(BF16)
            kf[s, h] = qk[:, ML_W + HD * h:ML_W + HD * (h + 1)] * (1.0 / math.sqrt(HD))
    kb = {p: kf[p].astype(BF16) for p in pairs}
    vext = {(s, h): jnp.concatenate([v_ref[s, :, sls[h]], ones_col], axis=1) for s, h in pairs}
    cn = {(s, h): cn_s[s * ML_HEADS + h] for s, h in pairs}
    qk_d = {p: lax.dot_general(qb[p], kb[p], _NT, preferred_element_type=F32) for p in pairs}
    qc = {p: jnp.dot(qb[p], cn[p].astype(BF16), preferred_element_type=F32) for p in pairs}

    m_inter, m_new, w, decay, rt = [], [], [], [], []
    for s in seqs:
        m_st = m_s[s]
        b_last = b[s][L - 1:L, :]
        m_inter.append(b[s] + m_st)
        log_w = b_last - b[s] + gi[s]
        m_new.append(jnp.maximum(b_last + m_st, jnp.max(log_w, axis=0, keepdims=True)))
        w.append(jnp.exp(log_w - m_new[s]))
        decay.append(jnp.exp(b_last + m_st - m_new[s]))
        rt.append(jnp.transpose(gi[s] - b[s]))

    s_b, kw_b, m_ts, a_inters = {}, {}, {}, {}
    for s, h in pairs:
        log_d = jnp.where(causal, b[s][:, h:h + 1] + rt[s][h:h + 1, :], NEG_INF)
        m_t = jnp.maximum(m_inter[s][:, h:h + 1], jnp.max(log_d, axis=1, keepdims=True))
        s_b[s, h] = (qk_d[s, h] * jnp.exp(log_d - m_t)).astype(BF16)
        kw_b[s, h] = (kf[s, h] * w[s][:, h:h + 1]).astype(BF16)
        m_ts[s, h] = m_t
        a_inters[s, h] = jnp.exp(m_inter[s][:, h:h + 1] - m_t)

    sv = {p: jnp.dot(s_b[p], vext[p], preferred_element_type=F32) for p in pairs}
    upd = {p: lax.dot_general(kw_b[p], vext[p], _TN, preferred_element_type=F32) for p in pairs}

    hm, sq = {}, {}
    for s, h in pairs:
        nd = sv[s, h] + a_inters[s, h] * qc[s, h]
        num = nd[:, :HD]
        den = nd[:, HD:]
        hh = num * (1.0 / jnp.maximum(jnp.abs(den), jnp.exp(-m_ts[s, h])))
        cn_s[s * ML_HEADS + h] = decay[s][:, h:h + 1] * cn[s, h] + upd[s, h]
        hm[s, h] = og_ref[s, :, sls[h]].astype(F32) * hh
        sq[s, h] = (hm[s, h] * hm[s, h]).astype(BF16)
    ms = {p: jnp.dot(sq[p], mean_mat, preferred_element_type=F32) for p in pairs}
    for s, h in pairs:
        y = hm[s, h] * lax.rsqrt(ms[s, h] + EPS) * gn_ref[:, sls[h]]
        out_ref[s, :, sls[h]] = y.astype(BF16)

    for s in seqs:
        m_s[s] = m_new[s]


def _mlstm(qk, vm, og, gi, gf, cw, cb, bi, bf, gn, bsz, seq):
    L = ML_CHUNK
    nc = seq // L
    nseq = ML_NB
    t = bsz * seq
    seq3 = lambda a: a.reshape(bsz, seq, a.shape[-1])
    row = lambda b, c: (b, c, 0)
    const = lambda b, c: (0, 0)
    out = pl.pallas_call(
        _mlstm_kernel,
        grid=(bsz // nseq, nc),
        in_specs=[pl.BlockSpec((nseq, L, 2 * ML_W), row),
                  pl.BlockSpec((nseq, L, ML_W), row),
                  pl.BlockSpec((nseq, L, ML_W), row),
                  pl.BlockSpec((nseq, L, LANES), row),
                  pl.BlockSpec((nseq, L, LANES), row),
                  pl.BlockSpec((CONV_WIDTH, 2 * ML_W), const),
                  pl.BlockSpec((1, 2 * ML_W), const),
                  pl.BlockSpec((1, LANES), const),
                  pl.BlockSpec((1, LANES), const),
                  pl.BlockSpec((1, ML_W), const)],
        out_specs=pl.BlockSpec((nseq, L, ML_W), row),
        out_shape=jax.ShapeDtypeStruct((bsz, seq, ML_W), BF16),
        scratch_shapes=[pltpu.VMEM((nseq, L + 8, 2 * ML_W), F32),
                        pltpu.VMEM((nseq * ML_HEADS, ML_HEAD_DIM, 2 * ML_HEAD_DIM), F32),
                        pltpu.VMEM((nseq, 1, LANES), F32)],
        compiler_params=_cparams(2),
        name="mlstm",
    )(seq3(qk), seq3(vm), seq3(og), seq3(gi), seq3(gf), cw, cb, bi, bf, gn)
    return out.reshape(t, ML_W)


def _kv_kernel(mem_ref, g_ref, w_ref, k_ref, v_ref):
    mn = _rms(mem_ref[...], g_ref[...]).astype(BF16)
    hw = k_ref.shape[1]
    k_ref[...] = jnp.dot(mn, w_ref[:, :hw], preferred_element_type=F32).astype(BF16)
    v_ref[...] = jnp.dot(mn, w_ref[:, hw:], preferred_element_type=F32).astype(BF16)


def _kv_prep(mem2d, g, w_ckv):
    r, d = mem2d.shape
    hw = w_ckv.shape[1] // 2
    tm = 256
    return pl.pallas_call(
        _kv_kernel,
        grid=(r // tm,),
        in_specs=[pl.BlockSpec((tm, d), lambda i: (i, 0)),
                  pl.BlockSpec((1, d), lambda i: (0, 0)),
                  pl.BlockSpec(w_ckv.shape, lambda i: (0, 0))],
        out_specs=[pl.BlockSpec((tm, hw), lambda i: (i, 0))] * 2,
        out_shape=[jax.ShapeDtypeStruct((r, hw), BF16)] * 2,
        compiler_params=_cparams(1),
        name="kv_prep",
    )(mem2d, g, w_ckv)


_R_E0, _R_E1, _R_RANK0, _R_RANK1, _R_P0, _R_P1 = range(6)
_ROUTER_LANE0 = N_GROUPS


_PACK_ROWS = 8


def _store_slabs(ref, v, rows, first=0):
    span = pl.ds(first * _PACK_ROWS, rows * _PACK_ROWS)
    ref[span, :] = v.reshape(rows * _PACK_ROWS, LANES).astype(ref.dtype)


def _load_slabs(ref, rows):
    return ref[...].astype(F32).reshape(rows, _PACK_ROWS * LANES)


def _store_slabs_staged(ref, v, rows, stage, first=0):
    base = first * _PACK_ROWS
    for r in range(_PACK_ROWS):
        stage[pl.ds(base + r, rows, stride=_PACK_ROWS), :] = v[:, LANES * r:LANES * (r + 1)]
    span = pl.ds(base, rows * _PACK_ROWS)
    ref[span, :] = stage[span, :].astype(ref.dtype)


def _load_slabs_staged(ref, rows, stage):
    stage[...] = ref[...].astype(F32)
    return jnp.concatenate(
        [stage[pl.ds(r, rows, stride=_PACK_ROWS), :] for r in range(_PACK_ROWS)], axis=1)


def _mid_kernel(x_ref, att_ref, hm_ref, wo_ref, gc_ref, wq_ref, k_ref, v_ref, wco_ref, gf_ref,
                wr_ref, br_ref, x2_ref, xn_ref, route_ref, idx_ref, cnt_ref, carry_s, stage_s):
    i = pl.program_id(0)
    tm = x_ref.shape[0] // MID_PARTS
    parts = range(MID_PARTS)
    rows = [pl.ds(p * tm, tm) for p in parts]

    @pl.when(i == 0)
    def _():
        carry_s[...] = jnp.zeros(carry_s.shape, F32)

    x1 = [x_ref[rows[p], :]
          + jnp.dot(att_ref[rows[p], :], wo_ref[0:ATT_Q, :], preferred_element_type=F32)
          + jnp.dot(hm_ref[rows[p], :], wo_ref[ATT_Q:, :], preferred_element_type=F32)
          for p in parts]

    xc = [_rms(x1[p], gc_ref[...]).astype(BF16) for p in parts]
    qb = [jnp.dot(xc[p], wq_ref[...], preferred_element_type=F32).astype(BF16) for p in parts]
    sls = [slice(X_HEAD_DIM * h, X_HEAD_DIM * (h + 1)) for h in range(X_HEADS)]
    sc = [[lax.dot_general(qb[p][:, sl], k_ref[:, sl], _NT, preferred_element_type=F32)
           for sl in sls] for p in parts]
    es = [[None] * X_HEADS for _ in parts]
    rinv = [[None] * X_HEADS for _ in parts]
    for p in parts:
        for h in range(X_HEADS):
            s = sc[p][h] * (1.0 / math.sqrt(X_HEAD_DIM))
            e = jnp.exp(s - jnp.max(s, axis=-1, keepdims=True))
            es[p][h] = e.astype(BF16)
            rinv[p][h] = 1.0 / jnp.sum(e, axis=-1, keepdims=True)
    o = [jnp.concatenate(
        [jnp.dot(es[p][h], v_ref[:, sls[h]], preferred_element_type=F32) * rinv[p][h]
         for h in range(X_HEADS)], axis=1).astype(BF16) for p in parts]
    x2 = [x1[p] + jnp.dot(o[p], wco_ref[...], preferred_element_type=F32) for p in parts]

    logits_p = []
    for p in parts:
        x2_ref[rows[p], :] = x2[p]
        xn = _rms(x2[p], gf_ref[...])
        xh = xn.astype(BF16)
        _store_slabs_staged(xn_ref, xn, tm, stage_s, first=p * tm)
        xl = (xn - xh.astype(F32)).astype(BF16)
        lg2 = jnp.dot(xh, wr_ref[...], preferred_element_type=F32)
        logits_p.append(lg2[:, :LANES] + lg2[:, LANES:]
                        + jnp.dot(xl, wr_ref[:, :LANES], preferred_element_type=F32) + br_ref[...])

    for p in parts:
        _route_part(logits_p[p], rows[p], p * tm, route_ref, idx_ref, cnt_ref, carry_s)


def _route_part(logits, rows, first, route_ref, idx_ref, cnt_ref, carry_s):
    tm = logits.shape[0]
    lane = lax.broadcasted_iota(jnp.int32, (tm, LANES), 1)
    lanef = lane.astype(F32)
    big = float(4 * LANES)
    gl = jnp.where(lane < N_GROUPS, logits, NEG_INF)
    gmax = jnp.max(gl, axis=-1, keepdims=True)
    gsel = jnp.min(jnp.where(gl == gmax, lanef, big), axis=-1, keepdims=True)
    gw = 1.0 / jnp.sum(jnp.exp(gl - gmax), axis=-1, keepdims=True)
    lo_lane = _ROUTER_LANE0 + EXPERTS_PER_GROUP * gsel
    in_group = (lanef >= lo_lane) & (lanef < lo_lane + EXPERTS_PER_GROUP)
    el = jnp.where(in_group, logits, NEG_INF)
    v0 = jnp.max(el, axis=-1, keepdims=True)
    i0 = jnp.min(jnp.where(el == v0, lanef, big), axis=-1, keepdims=True)
    el2 = jnp.where(lanef == i0, NEG_INF, el)
    v1 = jnp.max(el2, axis=-1, keepdims=True)
    i1 = jnp.min(jnp.where(el2 == v1, lanef, big), axis=-1, keepdims=True)
    tt = jnp.exp(v1 - v0)
    p0 = gw / (1.0 + tt)
    p1 = gw * tt / (1.0 + tt)

    sel0 = lanef == i0
    sel1 = lanef == i1
    mb = jnp.where(sel0 | sel1, 1.0, 0.0).astype(BF16)
    ti = lax.broadcasted_iota(jnp.int32, (tm, tm), 0)
    si = lax.broadcasted_iota(jnp.int32, (tm, tm), 1)
    strict = jnp.where(si < ti, 1.0, 0.0).astype(BF16)
    carry = carry_s[...]
    pref = jnp.dot(strict, mb, preferred_element_type=F32) + carry
    r0 = jnp.sum(jnp.where(sel0, pref, 0.0), axis=-1, keepdims=True)
    r1 = jnp.sum(jnp.where(sel1, pref, 0.0), axis=-1, keepdims=True)
    carry = carry + jnp.sum(mb.astype(F32), axis=0, keepdims=True)
    carry_s[...] = carry
    cnt_ref[...] = carry

    route = jnp.zeros((tm, LANES), F32)
    for idx, col in ((_R_E0, i0 - _ROUTER_LANE0), (_R_E1, i1 - _ROUTER_LANE0), (_R_RANK0, r0),
                     (_R_RANK1, r1), (_R_P0, p0), (_R_P1, p1)):
        route = jnp.where(lane == idx, col, route)
    route_ref[rows, :] = route
    idx_ref[:, pl.ds(first, tm)] = jnp.transpose(route)[0:8, :].astype(jnp.int32)


def _mid(x2d, att, hm, wo, gc, wq, kmem, vmem_, wco, gf, wr, br, seq):
    t, d = x2d.shape
    tm = TM_MID
    per_b = seq // tm
    rowmap = lambda i: (i, 0)
    const = lambda i: (0, 0)
    bmap = lambda i: (i // per_b, 0)
    mem_len = kmem.shape[0] // (t // seq)
    return pl.pallas_call(
        _mid_kernel,
        grid=(t // tm,),
        in_specs=[pl.BlockSpec((tm, d), rowmap),
                  pl.BlockSpec((tm, ATT_Q), rowmap),
                  pl.BlockSpec((tm, ML_W), rowmap),
                  pl.BlockSpec(wo.shape, const),
                  pl.BlockSpec((1, d), const),
                  pl.BlockSpec(wq.shape, const),
                  pl.BlockSpec((mem_len, kmem.shape[1]), bmap),
                  pl.BlockSpec((mem_len, vmem_.shape[1]), bmap),
                  pl.BlockSpec(wco.shape, const),
                  pl.BlockSpec((1, d), const),
                  pl.BlockSpec(wr.shape, const),
                  pl.BlockSpec((1, LANES), const)],
        out_specs=[pl.BlockSpec((tm, d), rowmap),
                   pl.BlockSpec((tm * _PACK_ROWS, LANES), rowmap),
                   pl.BlockSpec((tm, LANES), rowmap),
                   pl.BlockSpec((8, tm), lambda i: (0, i)),
                   pl.BlockSpec((1, LANES), const)],
        out_shape=[jax.ShapeDtypeStruct((t, d), F32),
                   jax.ShapeDtypeStruct((t * _PACK_ROWS, LANES), BF16),
                   jax.ShapeDtypeStruct((t, LANES), F32),
                   jax.ShapeDtypeStruct((8, t), jnp.int32),
                   jax.ShapeDtypeStruct((1, LANES), F32)],
        scratch_shapes=[pltpu.VMEM((1, LANES), F32),
                        pltpu.VMEM((tm * _PACK_ROWS, LANES), F32)],
        compiler_params=_cparams(1),
        name="mid",
    )(x2d, att, hm, wo, gc, wq, kmem, vmem_, wco, gf, wr, br)


_M_BLOCK_E, _M_PAD_END, _M_NUSED = range(3)
_META_LANES = 2 * LANES


def _plan_kernel(cnt_ref, idx_ref, dest_ref, meta_ref):
    cnt = cnt_ref[...]
    lane = lax.broadcasted_iota(jnp.int32, (1, LANES), 1)
    is_expert = (lane >= _ROUTER_LANE0) & (lane < _ROUTER_LANE0 + N_EXPERTS)
    nblk = jnp.where(is_expert, jnp.floor((cnt + (BM - 1)) * (1.0 / BM)), 0.0)
    jj = lax.broadcasted_iota(jnp.int32, (LANES, LANES), 0)
    kk = lax.broadcasted_iota(jnp.int32, (LANES, LANES), 1)
    upper = jnp.where(jj <= kk, 1.0, 0.0).astype(BF16)
    pend_blk = jnp.dot(jnp.broadcast_to(nblk, (8, LANES)).astype(BF16), upper,
                       preferred_element_type=F32)[0:1, :]
    pstart_rows = (pend_blk - nblk) * BM
    pend_rows = pend_blk * BM

    idx = idx_ref[...]
    off = jnp.zeros(idx.shape, F32)
    for e in range(N_EXPERTS):
        lane_e = _ROUTER_LANE0 + e
        off = jnp.where(idx == e, pstart_rows[:, lane_e:lane_e + 1], off)
    ranks = pltpu.roll(idx, idx.shape[0] - 2, axis=0)
    dest_ref[...] = ranks + off.astype(jnp.int32)

    blk = lax.broadcasted_iota(jnp.int32, (1, _META_LANES), 1).astype(F32)
    block_e = jnp.zeros((1, _META_LANES), F32)
    for e in range(N_EXPERTS):
        lane_e = _ROUTER_LANE0 + e
        block_e = block_e + jnp.where(pend_blk[:, lane_e:lane_e + 1] <= blk, 1.0, 0.0)
    block_e = jnp.minimum(block_e, N_EXPERTS - 1.0)
    last = _ROUTER_LANE0 + N_EXPERTS - 1
    nused = pend_blk[:, last:last + 1]
    pend_wide = jnp.concatenate([pend_rows, jnp.zeros((1, _META_LANES - LANES), F32)], axis=1)
    sub = lax.broadcasted_iota(jnp.int32, (8, _META_LANES), 0)
    meta = jnp.where(sub == _M_BLOCK_E, block_e, jnp.where(sub == _M_PAD_END, pend_wide, nused))
    meta_ref[...] = meta.astype(jnp.int32)


def _plan(cnt, idx):
    t = idx.shape[1]
    return pl.pallas_call(
        _plan_kernel,
        grid=(1,),
        in_specs=[pl.BlockSpec(cnt.shape, lambda i: (0, 0)),
                  pl.BlockSpec(idx.shape, lambda i: (0, 0))],
        out_specs=[pl.BlockSpec((8, t), lambda i: (0, 0)),
                   pl.BlockSpec((8, _META_LANES), lambda i: (0, 0))],
        out_shape=[jax.ShapeDtypeStruct((8, t), jnp.int32),
                   jax.ShapeDtypeStruct((8, _META_LANES), jnp.int32)],
        compiler_params=_cparams(1),
        name="plan",
    )(cnt, idx)


_SLAB = BM * _PACK_ROWS


def _dispatch_kernel(meta_ref, d_ref, xn_ref, xpad_ref, zbuf, sem, zsem):
    i = pl.program_id(0)
    td = xn_ref.shape[0] // _PACK_ROWS

    def zero_copy(e):
        pend = meta_ref[_M_PAD_END, _ROUTER_LANE0 + e]
        pstart = meta_ref[_M_PAD_END, _ROUTER_LANE0 + e - 1]
        first = pl.multiple_of((pend - BM) * _PACK_ROWS, _SLAB)
        cp = pltpu.make_async_copy(zbuf, xpad_ref.at[pl.ds(first, _SLAB), :], zsem)
        return pend > pstart, cp

    @pl.when(i == 0)
    def _():
        zbuf[...] = jnp.zeros(zbuf.shape, zbuf.dtype)
        for e in range(N_EXPERTS):
            nonempty, cp = zero_copy(e)
            pl.when(nonempty)(cp.start)
        for e in range(N_EXPERTS):
            nonempty, cp = zero_copy(e)
            pl.when(nonempty)(cp.wait)

        def tail_copy(b):
            return pltpu.make_async_copy(
                zbuf, xpad_ref.at[pl.ds(pl.multiple_of(b * _SLAB, _SLAB), _SLAB), :], zsem)

        first_unused = meta_ref[_M_NUSED, 0]
        n_blocks = xpad_ref.shape[0] // _SLAB
        lax.fori_loop(first_unused, n_blocks, lambda b, c: (tail_copy(b).start(), c)[1], 0)
        lax.fori_loop(first_unused, n_blocks, lambda b, c: (tail_copy(b).wait(), c)[1], 0)

    def row_copy(t, dst):
        src = pl.ds(t * _PACK_ROWS, _PACK_ROWS)
        dsl = pl.ds(pl.multiple_of(dst * _PACK_ROWS, _PACK_ROWS), _PACK_ROWS)
        return pltpu.make_async_copy(xn_ref.at[src, :], xpad_ref.at[dsl, :], sem)

    for t in range(td):
        row_copy(t, d_ref[0, t]).start(priority=0)
        row_copy(t, d_ref[1, t]).start(priority=1)

    def wait(t, carry):
        row_copy(0, 0).wait()
        row_copy(0, 0).wait()
        return carry

    lax.fori_loop(0, td, wait, 0, unroll=8)


def _dispatch(meta, dest, xn_packed, rows):
    t = dest.shape[1]
    td = TD
    return pl.pallas_call(
        _dispatch_kernel,
        grid_spec=pltpu.PrefetchScalarGridSpec(
            num_scalar_prefetch=1,
            grid=(t // td,),
            in_specs=[pl.BlockSpec((8, td), lambda i, m: (0, i), memory_space=pltpu.SMEM),
                      pl.BlockSpec((td * _PACK_ROWS, LANES), lambda i, m: (i, 0))],
            out_specs=pl.BlockSpec(memory_space=pl.ANY),
            scratch_shapes=[pltpu.VMEM((_SLAB, LANES), BF16),
                            pltpu.SemaphoreType.DMA(()),
                            pltpu.SemaphoreType.DMA(())]),
        out_shape=jax.ShapeDtypeStruct((rows * _PACK_ROWS, LANES), BF16),
        compiler_params=_cparams(1),
        name="dispatch",
    )(meta, dest, xn_packed)


def _expert_kernel(meta_ref, xpad_ref, wg_ref, wu_ref, wd_ref, ypad_ref, wg_s, wu_s, wd_s,
                   xbuf, ybuf, xsem, ysem):
    e = pl.program_id(0)
    nused = meta_ref[_M_NUSED, 0]
    lane_e = _ROUTER_LANE0 + e
    first_blk = lax.div(meta_ref[_M_PAD_END, lane_e - 1], BM)
    n_blk = lax.div(meta_ref[_M_PAD_END, lane_e], BM) - first_blk

    def block_rows(g):
        return pl.ds(pl.multiple_of(g * _SLAB, _SLAB), _SLAB)

    def x_copy(g, slot):
        return pltpu.make_async_copy(xpad_ref.at[block_rows(g), :], xbuf.at[slot], xsem.at[slot])

    def y_copy(g, slot):
        return pltpu.make_async_copy(ybuf.at[slot], ypad_ref.at[block_rows(g), :], ysem.at[slot])

    nx = xbuf.shape[0]
    ahead = nx - 1

    @pl.when(e == 0)
    def _():
        for g0 in range(ahead):
            @pl.when(g0 < nused)
            def _():
                x_copy(g0, g0).start(priority=1)

    @pl.when(n_blk > 0)
    def _():
        wg_s[...] = wg_ref[...].astype(BF16)
        wu_s[...] = wu_ref[...].astype(BF16)
        wd_s[...] = wd_ref[...].astype(BF16)

    def body(j, carry):
        g = first_blk + j
        xslot = lax.rem(g, nx)
        yslot = lax.rem(g, 2)
        x_copy(g, xslot).wait()

        @pl.when(g + ahead < nused)
        def _():
            x_copy(g + ahead, lax.rem(g + ahead, nx)).start(priority=1)

        xb = _load_slabs(xbuf.at[xslot], BM).astype(BF16)
        gate = jnp.dot(xb, wg_s[...], preferred_element_type=F32)
        up = jnp.dot(xb, wu_s[...], preferred_element_type=F32)
        hb = (gate * jax.nn.sigmoid(gate) * up).astype(BF16)
        y = jnp.dot(hb, wd_s[...], preferred_element_type=F32)

        @pl.when(g >= 2)
        def _():
            y_copy(g - 2, yslot).wait()

        _store_slabs(ybuf.at[yslot], y, BM)
        y_copy(g, yslot).start()
        return carry

    lax.fori_loop(0, n_blk, body, 0)

    @pl.when(e == pl.num_programs(0) - 1)
    def _():
        for back in (2, 1):
            @pl.when(nused >= back)
            def _():
                y_copy(nused - back, lax.rem(nused - back, 2)).wait()

        ybuf[0] = jnp.zeros(ybuf.shape[1:], ybuf.dtype)
        n_blocks = ypad_ref.shape[0] // _SLAB
        lax.fori_loop(nused, n_blocks, lambda b, c: (y_copy(b, 0).start(), c)[1], 0)
        lax.fori_loop(nused, n_blocks, lambda b, c: (y_copy(b, 0).wait(), c)[1], 0)


def _experts(meta, xpad, wg, wu, wd):
    n_exp, d, de = wg.shape
    wmap = lambda e, m: (e, 0, 0)
    return pl.pallas_call(
        _expert_kernel,
        grid_spec=pltpu.PrefetchScalarGridSpec(
            num_scalar_prefetch=1,
            grid=(n_exp,),
            in_specs=[pl.BlockSpec(memory_space=pl.ANY),
                      pl.BlockSpec((None, d, de), wmap),
                      pl.BlockSpec((None, d, de), wmap),
                      pl.BlockSpec((None, de, d), wmap)],
            out_specs=pl.BlockSpec(memory_space=pl.ANY),
            scratch_shapes=[pltpu.VMEM((d, de), BF16),
                            pltpu.VMEM((d, de), BF16),
                            pltpu.VMEM((de, d), BF16),
                            pltpu.VMEM((EXPERT_AHEAD + 1, _SLAB, LANES), BF16),
                            pltpu.VMEM((2, _SLAB, LANES), BF16),
                            pltpu.SemaphoreType.DMA((EXPERT_AHEAD + 1,)),
                            pltpu.SemaphoreType.DMA((2,))]),
        out_shape=jax.ShapeDtypeStruct(xpad.shape, BF16),
        compiler_params=_cparams(1),
        name="experts",
    )(meta, xpad, wg, wu, wd)


def _combine_kernel(d0c_ref, d1c_ref, d0n_ref, d1n_ref, x2_ref, route_ref, g_ref, ypad_ref, out_ref,
                    ybuf, sem,
                    stage0_s, stage1_s):
    i = pl.program_id(0)
    n = pl.num_programs(0)
    tc = x2_ref.shape[0]

    def row_copy(src, slot, which, t):
        ssl = pl.ds(pl.multiple_of(src * _PACK_ROWS, _PACK_ROWS), _PACK_ROWS)
        dsl = pl.ds(t * _PACK_ROWS, _PACK_ROWS)
        return pltpu.make_async_copy(ypad_ref.at[ssl, :], ybuf.at[slot, which, dsl, :],
                                     sem.at[slot])

    def issue(d0_ref, d1_ref, slot):
        for t in range(tc):
            row_copy(d0_ref[0, 0, t], slot, 0, t).start(priority=0)
            row_copy(d1_ref[0, 0, t], slot, 1, t).start(priority=1)

    slot = i % 2

    @pl.when(i == 0)
    def _():
        issue(d0c_ref, d1c_ref, 0)

    @pl.when(i + 1 < n)
    def _():
        issue(d0n_ref, d1n_ref, 1 - slot)

    def wait(t, carry):
        row_copy(0, slot, 0, 0).wait()
        row_copy(0, slot, 1, 0).wait()
        return carry
    lax.fori_loop(0, tc, wait, 0, unroll=8)

    route = route_ref[...]
    p0 = route[:, _R_P0:_R_P0 + 1]
    p1 = route[:, _R_P1:_R_P1 + 1]
    y0 = _load_slabs_staged(ybuf.at[slot, 0], tc, stage0_s)
    y1 = _load_slabs_staged(ybuf.at[slot, 1], tc, stage1_s)
    x3 = x2_ref[...] + p0 * y0 + p1 * y1
    out_ref[...] = _rms(x3, g_ref[...])


def _combine(dest, x2, route, g, ypad):
    t, d = x2.shape
    tc = TC
    nt = t // tc
    d0 = dest[0].reshape(nt, 1, tc)
    d1 = dest[1].reshape(nt, 1, tc)
    cur = pl.BlockSpec((1, 1, tc), lambda i: (i, 0, 0), memory_space=pltpu.SMEM)
    nxt = pl.BlockSpec((1, 1, tc), lambda i: (jnp.minimum(i + 1, nt - 1), 0, 0),
                       memory_space=pltpu.SMEM)
    return pl.pallas_call(
        _combine_kernel,
        grid=(nt,),
        in_specs=[cur, cur, nxt, nxt,
                  pl.BlockSpec((tc, d), lambda i: (i, 0)),
                  pl.BlockSpec((tc, LANES), lambda i: (i, 0)),
                  pl.BlockSpec((1, d), lambda i: (0, 0)),
                  pl.BlockSpec(memory_space=pl.ANY)],
        out_specs=pl.BlockSpec((tc, d), lambda i: (i, 0)),
        out_shape=jax.ShapeDtypeStruct((t, d), F32),
        scratch_shapes=[pltpu.VMEM((2, 2, tc * _PACK_ROWS, LANES), BF16),
                        pltpu.SemaphoreType.DMA((2,)),
                        pltpu.VMEM((tc * _PACK_ROWS, LANES), F32),
                        pltpu.VMEM((tc * _PACK_ROWS, LANES), F32)],
        compiler_params=_cparams(1),
        name="combine",
    )(d0, d1, d0, d1, x2, route, g, ypad)


def _pad_cols(a, width):
    return jnp.pad(a, ((0, 0), (0, width - a.shape[1])))


def kernel(x, mem, norm_mix, w_in, b_gates, conv_w, conv_b, att_sinks, norm_att_out, norm_ml_out,
           w_out, norm_cross, norm_mem, w_cq, w_ckv, w_co, norm_ffn, w_router_group,
           b_router_group, w_router_expert, b_router_expert, w_e_gate, w_e_up, w_e_down,
           norm_final):
    bsz, seq, d = x.shape
    t = bsz * seq
    depth = w_in.shape[0]
    assert depth == 1, "the final RMSNorm is fused into the last layer's combine kernel"
    xs = x.reshape(t, d)
    mem2d = mem.reshape(-1, d)

    for l in range(depth):
        bi = _pad_cols(b_gates[l][None, :ML_HEADS], LANES)
        bf = _pad_cols(b_gates[l][None, ML_HEADS:], LANES)
        w_r = _pad_cols(jnp.concatenate([w_router_group[l], w_router_expert[l]], axis=1), LANES)
        w_r_hi = w_r.astype(BF16)
        w_r_lo = (w_r - w_r_hi.astype(F32)).astype(BF16)
        w_r_packed = jnp.concatenate([w_r_hi, w_r_lo], axis=1)
        b_r = _pad_cols(jnp.concatenate([b_router_group[l], b_router_expert[l]])[None, :], LANES)

        qa, ka, va, qk, vm, og, gi, gf = _in_proj(xs, norm_mix[l][None, :],
                                                  jnp.swapaxes(w_in, 1, 2), l)
        att = _swa(qa, ka, va, att_sinks[l], norm_att_out[l][None, :], bsz, seq)
        hm = _mlstm(qk, vm, og, gi, gf, conv_w[l], conv_b[l][None, :], bi, bf,
                    norm_ml_out[l][None, :], bsz, seq)

        kmem, vmem_ = _kv_prep(mem2d, norm_mem[l][None, :], w_ckv[l].astype(BF16))
        x2, xn, route, idx, cnt = _mid(xs, att, hm, w_out[l].astype(BF16), norm_cross[l][None, :],
                                       w_cq[l].astype(BF16), kmem, vmem_, w_co[l].astype(BF16),
                                       norm_ffn[l][None, :], w_r_packed, b_r, seq)

        nbk = (t * TOP_K) // BM + N_EXPERTS
        assert nbk <= _META_LANES
        dest, meta = _plan(cnt, idx)
        xpad = _dispatch(meta, dest, xn, nbk * BM)
        ypad = _experts(meta, xpad, w_e_gate[l], w_e_up[l], w_e_down[l])
        xs = _combine(dest, x2, route, norm_final[None, :], ypad)
    return xs.reshape(bsz, seq, d)
```

```python
import functools
import math

import jax
import jax.numpy as jnp
from jax import lax
from jax.experimental import pallas as pl
from jax.experimental.pallas import tpu as pltpu

F32 = jnp.float32
BF16 = jnp.bfloat16
EPS = 1e-6
NEG_INF = float("-inf")

ATT_HEADS = 8
ATT_HEAD_DIM = 64
ATT_Q = ATT_HEADS * ATT_HEAD_DIM
ATT_KV = 128
WINDOW = 128
ML_HEADS = 4
ML_HEAD_DIM = 128
ML_W = ML_HEADS * ML_HEAD_DIM
CONV_WIDTH = 4
X_HEADS = 4
X_HEAD_DIM = 128
N_GROUPS = 4
EXPERTS_PER_GROUP = 8
N_EXPERTS = N_GROUPS * EXPERTS_PER_GROUP
TOP_K = 2

LANES = 128
TM_IN = 512
IN_PARTS = 2
SWA_QB = 2
ML_CHUNK = 256
ML_NB = 1
TM_MID = 1024
MID_PARTS = 4
TD = 1024
TC = 256
BM = 256
EXPERT_AHEAD = 3
VMEM_LIMIT = 56 * 1024 * 1024

_NT = (((1,), (1,)), ((), ()))
_TN = (((0,), (0,)), ((), ()))


def _rms(x, g):
    return x * lax.rsqrt(jnp.mean(x * x, axis=-1, keepdims=True) + EPS) * g


def _cparams(n_axes):
    return pltpu.CompilerParams(dimension_semantics=("arbitrary",) * n_axes,
                                vmem_limit_bytes=VMEM_LIMIT)


_SEG = {"qa": (0, 512), "ka": (512, 640), "va": (640, 768), "qk": (768, 1792),
        "vm": (1792, 2304), "om": (2304, 2816), "gi": (2816, 2944), "gf": (2944, 3072)}


_W_CHUNK = 256


def _in_proj_kernel(x_ref, g_ref, win_ref, qa_ref, ka_ref, va_ref, qk_ref, vm_ref, og_ref,
                    gi_ref, gf_ref, w_ref):
    @pl.when(pl.program_id(0) == 0)
    def _():
        n_main = _SEG["om"][1]
        for lo in range(0, n_main, _W_CHUNK):
            w_ref[:, lo:lo + _W_CHUNK] = jnp.transpose(win_ref[lo:lo + _W_CHUNK, :]).astype(BF16)
        gates = win_ref[n_main:n_main + 2 * ML_HEADS, :]
        row = lax.broadcasted_iota(jnp.int32, gates.shape, 0)
        pad = jnp.zeros((LANES - 2 * ML_HEADS, gates.shape[1]), F32)
        gi_rows = jnp.where(row < ML_HEADS, gates, 0.0)
        gf_rows = jnp.where(row < ML_HEADS, pltpu.roll(gates, ML_HEADS, axis=0), 0.0)
        for name, blk in (("gi", gi_rows), ("gf", gf_rows)):
            lo, hi = _SEG[name]
            w_ref[:, lo:hi] = jnp.transpose(jnp.concatenate([blk, pad], axis=0)).astype(BF16)

    tm = x_ref.shape[0] // IN_PARTS
    rows = [pl.ds(p * tm, tm) for p in range(IN_PARTS)]
    hb = [_rms(x_ref[r, :], g_ref[...]).astype(BF16) for r in rows]

    def seg(p, name):
        lo, hi = _SEG[name]
        return jnp.dot(hb[p], w_ref[:, lo:hi], preferred_element_type=F32)

    plain = {"qa": qa_ref, "ka": ka_ref, "va": va_ref, "qk": qk_ref, "vm": vm_ref, "gi": gi_ref,
             "gf": gf_ref}
    for p, r in enumerate(rows):
        for name, ref in plain.items():
            ref[r, :] = seg(p, name).astype(ref.dtype)
        og_ref[r, :] = jax.nn.sigmoid(seg(p, "om")).astype(og_ref.dtype)


def _in_proj(x2d, g, w_in_t, layer):
    t, d = x2d.shape
    tm = TM_IN
    widths = [(512, BF16), (128, BF16), (128, BF16), (1024, F32), (512, BF16), (512, BF16),
              (128, F32), (128, F32)]
    const = lambda i: (0, 0)
    return pl.pallas_call(
        _in_proj_kernel,
        grid=(t // tm,),
        in_specs=[pl.BlockSpec((tm, d), lambda i: (i, 0)),
                  pl.BlockSpec((1, d), const),
                  pl.BlockSpec((None,) + w_in_t.shape[1:], lambda i: (layer, 0, 0),
                               pipeline_mode=pl.Buffered(1))],
        out_specs=[pl.BlockSpec((tm, w), lambda i: (i, 0)) for w, _ in widths],
        out_shape=[jax.ShapeDtypeStruct((t, w), dt) for w, dt in widths],
        scratch_shapes=[pltpu.VMEM((d, _SEG["gf"][1]), BF16)],
        compiler_params=_cparams(1),
        name="in_proj",
    )(x2d, g, w_in_t)


def _swa_kernel(sink_ref, q_ref, kp_ref, kc_ref, vp_ref, vc_ref, gn_ref, o_ref, bias_s):
    b = pl.program_id(0)
    n = pl.program_id(1)

    @pl.when((b == 0) & (n == 0))
    def _():
        ti = lax.broadcasted_iota(jnp.int32, (WINDOW, 2 * WINDOW), 0)
        si = lax.broadcasted_iota(jnp.int32, (WINDOW, 2 * WINDOW), 1)
        dist = ti + WINDOW - si
        band = (dist >= 0) & (dist < WINDOW)
        distf = dist.astype(F32)
        for hidx in range(ATT_HEADS):
            slope = 2.0 ** (-8.0 * (hidx + 1) / ATT_HEADS)
            full = jnp.where(band, -slope * distf, NEG_INF)
            bias_s[1, hidx] = full
            bias_s[0, hidx] = jnp.where(si >= WINDOW, full, NEG_INF)

    scale = 1.0 / math.sqrt(ATT_HEAD_DIM)
    k_all = jnp.concatenate([kp_ref[...], kc_ref[...]], axis=0).astype(F32) * scale
    v_all = jnp.concatenate([vp_ref[...], vc_ref[...]], axis=0).astype(F32)
    lane = lax.broadcasted_iota(jnp.int32, k_all.shape, 1)
    low = lane < ATT_HEAD_DIM

    def lane_variants(a):
        ar = pltpu.roll(a, ATT_HEAD_DIM, axis=1)
        z = jnp.zeros_like(a)
        return ((jnp.where(low, a, z).astype(BF16), jnp.where(low, z, ar).astype(BF16)),
                (jnp.where(low, ar, z).astype(BF16), jnp.where(low, z, a).astype(BF16)))

    k_ops = lane_variants(k_all)
    v_ops = lane_variants(v_all)
    n_tiles = ATT_Q // LANES
    kv_of = lambda j: (2 * j) // (ATT_HEADS // 2)
    blocks = range(SWA_QB)
    rows = [slice(WINDOW * i, WINDOW * (i + 1)) for i in blocks]
    keys = [slice(WINDOW * i, WINDOW * (i + 2)) for i in blocks]
    bias_slot = [jnp.where(n == 0, 0, 1) if i == 0 else 1 for i in blocks]

    scores = [[[lax.dot_general(q_ref[rows[i], LANES * j:LANES * (j + 1)],
                                k_ops[kv_of(j)][par][keys[i]], _NT, preferred_element_type=F32)
                for par in range(2)] for j in range(n_tiles)] for i in blocks]
    probs = [[[None, None] for _ in range(n_tiles)] for _ in blocks]
    rinv = [[[None, None] for _ in range(n_tiles)] for _ in blocks]
    for i in blocks:
        for j in range(n_tiles):
            for par in range(2):
                hidx = 2 * j + par
                logits = scores[i][j][par] + bias_s[bias_slot[i], hidx]
                sink = sink_ref[hidx]
                mx = jnp.maximum(jnp.max(logits, axis=-1, keepdims=True), sink)
                e = jnp.exp(logits - mx)
                den = jnp.sum(e, axis=-1, keepdims=True) + jnp.exp(sink - mx)
                probs[i][j][par] = e.astype(BF16)
                rinv[i][j][par] = 1.0 / den
    lane_o = lax.broadcasted_iota(jnp.int32, (WINDOW, LANES), 1)
    for i in blocks:
        tiles = []
        for j in range(n_tiles):
            acc = (jnp.dot(probs[i][j][0], v_ops[kv_of(j)][0][keys[i]], preferred_element_type=F32)
                   + jnp.dot(probs[i][j][1], v_ops[kv_of(j)][1][keys[i]],
                             preferred_element_type=F32))
            tiles.append(acc * jnp.where(lane_o < ATT_HEAD_DIM, rinv[i][j][0], rinv[i][j][1]))
        att = jnp.concatenate(tiles, axis=1)
        o_ref[rows[i], :] = _rms(att, gn_ref[...]).astype(BF16)


def _swa(qa, ka, va, sinks, gn, bsz, seq):
    nb = seq // WINDOW
    ns = nb // SWA_QB
    t = bsz * seq
    cur = lambda b, n: (b * ns + n, 0)
    prev = lambda b, n: (b * nb + jnp.maximum(n * SWA_QB - 1, 0), 0)
    return pl.pallas_call(
        _swa_kernel,
        grid=(bsz, ns),
        in_specs=[pl.BlockSpec(memory_space=pltpu.SMEM),
                  pl.BlockSpec((SWA_QB * WINDOW, ATT_Q), cur),
                  pl.BlockSpec((WINDOW, ATT_KV), prev),
                  pl.BlockSpec((SWA_QB * WINDOW, ATT_KV), cur),
                  pl.BlockSpec((WINDOW, ATT_KV), prev),
                  pl.BlockSpec((SWA_QB * WINDOW, ATT_KV), cur),
                  pl.BlockSpec((1, ATT_Q), lambda b, n: (0, 0))],
        out_specs=pl.BlockSpec((SWA_QB * WINDOW, ATT_Q), cur),
        out_shape=jax.ShapeDtypeStruct((t, ATT_Q), BF16),
        scratch_shapes=[pltpu.VMEM((2, ATT_HEADS, WINDOW, 2 * WINDOW), F32)],
        compiler_params=_cparams(2),
        name="swa",
    )(sinks, qa, ka, ka, va, va, gn)


def _split3(a):
    hi = a.astype(BF16)
    r1 = a - hi.astype(F32)
    mid = r1.astype(BF16)
    lo = (r1 - mid.astype(F32)).astype(BF16)
    return hi, mid, lo


def _mlstm_kernel(qk_ref, v_ref, og_ref, gi_ref, gf_ref, cw_ref, cb_ref, bi_ref, bf_ref, gn_ref,
                  out_ref, xbuf, cn_s, m_s):
    c = pl.program_id(1)
    nseq, L = qk_ref.shape[0], qk_ref.shape[1]
    HD = ML_HEAD_DIM
    seqs = range(nseq)
    heads = range(ML_HEADS)
    pairs = [(s, h) for s in seqs for h in heads]
    sls = [slice(HD * h, HD * (h + 1)) for h in heads]

    @pl.when(c == 0)
    def _():
        xbuf[:, 0:8, :] = jnp.zeros((nseq, 8, xbuf.shape[2]), F32)
        cn_s[...] = jnp.zeros(cn_s.shape, F32)
        m_s[...] = jnp.zeros(m_s.shape, F32)

    ti = lax.broadcasted_iota(jnp.int32, (L, L), 0)
    si = lax.broadcasted_iota(jnp.int32, (L, L), 1)
    causal = si <= ti
    tril = jnp.where(causal, 1.0, 0.0).astype(BF16)
    ones_col = jnp.ones((L, HD), BF16)
    mean_mat = jnp.full((HD, HD), 1.0 / HD, BF16)

    gi, b = [], []
    for s in seqs:
        gi.append(gi_ref[s] + bi_ref[...])
        z = gf_ref[s] + bf_ref[...]
        lf = jnp.minimum(z, 0.0) - jnp.log1p(jnp.exp(-jnp.abs(z)))
        acc = jnp.zeros((L, LANES), F32)
        for part in _split3(lf):
            acc = acc + jnp.dot(tril, part, preferred_element_type=F32)
        b.append(acc)

    qb, kf = {}, {}
    for s in seqs:
        xbuf[s, 8:8 + L, :] = qk_ref[s]
        y = cb_ref[...] + cw_ref[CONV_WIDTH - 1:CONV_WIDTH, :] * xbuf[s, 8:8 + L, :]
        for j in range(CONV_WIDTH - 1):
            sh = CONV_WIDTH - 1 - j
            y = y + cw_ref[j:j + 1, :] * xbuf[s, 8 - sh:8 - sh + L, :]
        xbuf[s, 0:8, :] = xbuf[s, L:L + 8, :]
        qk = y * jax.nn.sigmoid(y)
        for h in heads:
            qb[s, h] = qk[:, sls[h]].astype(BF16)
            kf[s, h] = qk[:, ML_W + HD * h:ML_W + HD * (h + 1)] * (1.0 / math.sqrt(HD))
    kb = {p: kf[p].astype(BF16) for p in pairs}
    vext = {(s, h): jnp.concatenate([v_ref[s, :, sls[h]], ones_col], axis=1) for s, h in pairs}
    cn = {(s, h): cn_s[s * ML_HEADS + h] for s, h in pairs}
    qk_d = {p: lax.dot_general(qb[p], kb[p], _NT, preferred_element_type=F32) for p in pairs}
    qc = {p: jnp.dot(qb[p], cn[p].astype(BF16), preferred_element_type=F32) for p in pairs}

    m_inter, m_new, w, decay, rt = [], [], [], [], []
    for s in seqs:
        m_st = m_s[s]
        b_last = b[s][L - 1:L, :]
        m_inter.append(b[s] + m_st)
        log_w = b_last - b[s] + gi[s]
        m_new.append(jnp.maximum(b_last + m_st, jnp.max(log_w, axis=0, keepdims=True)))
        w.append(jnp.exp(log_w - m_new[s]))
        decay.append(jnp.exp(b_last + m_st - m_new[s]))
        rt.append(jnp.transpose(gi[s] - b[s]))

    s_b, kw_b, m_ts, a_inters = {}, {}, {}, {}
    for s, h in pairs:
        log_d = jnp.where(causal, b[s][:, h:h + 1] + rt[s][h:h + 1, :], NEG_INF)
        m_t = jnp.maximum(m_inter[s][:, h:h + 1], jnp.max(log_d, axis=1, keepdims=True))
        s_b[s, h] = (qk_d[s, h] * jnp.exp(log_d - m_t)).astype(BF16)
        kw_b[s, h] = (kf[s, h] * w[s][:, h:h + 1]).astype(BF16)
        m_ts[s, h] = m_t
        a_inters[s, h] = jnp.exp(m_inter[s][:, h:h + 1] - m_t)

    sv = {p: jnp.dot(s_b[p], vext[p], preferred_element_type=F32) for p in pairs}
    upd = {p: lax.dot_general(kw_b[p], vext[p], _TN, preferred_element_type=F32) for p in pairs}

    hm, sq = {}, {}
    for s, h in pairs:
        nd = sv[s, h] + a_inters[s, h] * qc[s, h]
        num = nd[:, :HD]
        den = nd[:, HD:]
        hh = num * (1.0 / jnp.maximum(jnp.abs(den), jnp.exp(-m_ts[s, h])))
        cn_s[s * ML_HEADS + h] = decay[s][:, h:h + 1] * cn[s, h] + upd[s, h]
        hm[s, h] = og_ref[s, :, sls[h]].astype(F32) * hh
        sq[s, h] = (hm[s, h] * hm[s, h]).astype(BF16)
    ms = {p: jnp.dot(sq[p], mean_mat, preferred_element_type=F32) for p in pairs}
    for s, h in pairs:
        y = hm[s, h] * lax.rsqrt(ms[s, h] + EPS) * gn_ref[:, sls[h]]
        out_ref[s, :, sls[h]] = y.astype(BF16)

    for s in seqs:
        m_s[s] = m_new[s]


def _mlstm(qk, vm, og, gi, gf, cw, cb, bi, bf, gn, bsz, seq):
    L = ML_CHUNK
    nc = seq // L
    nseq = ML_NB
    t = bsz * seq
    seq3 = lambda a: a.reshape(bsz, seq, a.shape[-1])
    row = lambda b, c: (b, c, 0)
    const = lambda b, c: (0, 0)
    out = pl.pallas_call(
        _mlstm_kernel,
        grid=(bsz // nseq, nc),
        in_specs=[pl.BlockSpec((nseq, L, 2 * ML_W), row),
                  pl.BlockSpec((nseq, L, ML_W), row),
                  pl.BlockSpec((nseq, L, ML_W), row),
                  pl.BlockSpec((nseq, L, LANES), row),
                  pl.BlockSpec((nseq, L, LANES), row),
                  pl.BlockSpec((CONV_WIDTH, 2 * ML_W), const),
                  pl.BlockSpec((1, 2 * ML_W), const),
                  pl.BlockSpec((1, LANES), const),
                  pl.BlockSpec((1, LANES), const),
                  pl.BlockSpec((1, ML_W), const)],
        out_specs=pl.BlockSpec((nseq, L, ML_W), row),
        out_shape=jax.ShapeDtypeStruct((bsz, seq, ML_W), BF16),
        scratch_shapes=[pltpu.VMEM((nseq, L + 8, 2 * ML_W), F32),
                        pltpu.VMEM((nseq * ML_HEADS, ML_HEAD_DIM, 2 * ML_HEAD_DIM), F32),
                        pltpu.VMEM((nseq, 1, LANES), F32)],
        compiler_params=_cparams(2),
        name="mlstm",
    )(seq3(qk), seq3(vm), seq3(og), seq3(gi), seq3(gf), cw, cb, bi, bf, gn)
    return out.reshape(t, ML_W)


def _kv_kernel(mem_ref, g_ref, w_ref, k_ref, v_ref):
    mn = _rms(mem_ref[...], g_ref[...]).astype(BF16)
    hw = k_ref.shape[1]
    k_ref[...] = jnp.dot(mn, w_ref[:, :hw].astype(BF16), preferred_element_type=F32).astype(BF16)
    v_ref[...] = jnp.dot(mn, w_ref[:, hw:].astype(BF16), preferred_element_type=F32).astype(BF16)


def _kv_prep(mem2d, g, w_ckv):
    r, d = mem2d.shape
    hw = w_ckv.shape[1] // 2
    tm = 256
    return pl.pallas_call(
        _kv_kernel,
        grid=(r // tm,),
        in_specs=[pl.BlockSpec((tm, d), lambda i: (i, 0)),
                  pl.BlockSpec((1, d), lambda i: (0, 0)),
                  pl.BlockSpec(w_ckv.shape, lambda i: (0, 0))],
        out_specs=[pl.BlockSpec((tm, hw), lambda i: (i, 0))] * 2,
        out_shape=[jax.ShapeDtypeStruct((r, hw), BF16)] * 2,
        compiler_params=_cparams(1),
        name="kv_prep",
    )(mem2d, g, w_ckv)


_R_E0, _R_E1, _R_RANK0, _R_RANK1, _R_P0, _R_P1 = range(6)
_ROUTER_LANE0 = N_GROUPS


_PACK_ROWS = 8


def _store_slabs(ref, v, rows, first=0):
    span = pl.ds(first * _PACK_ROWS, rows * _PACK_ROWS)
    ref[span, :] = v.reshape(rows * _PACK_ROWS, LANES).astype(ref.dtype)


def _load_slabs(ref, rows):
    return ref[...].astype(F32).reshape(rows, _PACK_ROWS * LANES)


def _store_slabs_staged(ref, v, rows, stage, first=0):
    base = first * _PACK_ROWS
    for r in range(_PACK_ROWS):
        stage[pl.ds(base + r, rows, stride=_PACK_ROWS), :] = v[:, LANES * r:LANES * (r + 1)]
    span = pl.ds(base, rows * _PACK_ROWS)
    ref[span, :] = stage[span, :].astype(ref.dtype)


def _load_slabs_staged(ref, rows, stage):
    stage[...] = ref[...].astype(F32)
    return jnp.concatenate(
        [stage[pl.ds(r, rows, stride=_PACK_ROWS), :] for r in range(_PACK_ROWS)], axis=1)


def _mid_kernel(x_ref, att_ref, hm_ref, wo32_ref, gc_ref, wq32_ref, k_ref, v_ref, wco32_ref, gf_ref,
                wr_ref, br_ref, x2_ref, xn_ref, route_ref, idx_ref, cnt_ref, carry_s, stage_s,
                wo_ref, wq_ref, wco_ref):
    i = pl.program_id(0)
    tm = x_ref.shape[0] // MID_PARTS
    parts = range(MID_PARTS)
    rows = [pl.ds(p * tm, tm) for p in parts]

    @pl.when(i == 0)
    def _():
        carry_s[...] = jnp.zeros(carry_s.shape, F32)
        for src, dst in ((wo32_ref, wo_ref), (wq32_ref, wq_ref), (wco32_ref, wco_ref)):
            for lo in range(0, src.shape[0], _W_CHUNK):
                dst[lo:lo + _W_CHUNK, :] = src[lo:lo + _W_CHUNK, :].astype(BF16)

    x1 = [x_ref[rows[p], :]
          + jnp.dot(att_ref[rows[p], :], wo_ref[0:ATT_Q, :], preferred_element_type=F32)
          + jnp.dot(hm_ref[rows[p], :], wo_ref[ATT_Q:, :], preferred_element_type=F32)
          for p in parts]

    xc = [_rms(x1[p], gc_ref[...]).astype(BF16) for p in parts]
    qb = [jnp.dot(xc[p], wq_ref[...], preferred_element_type=F32).astype(BF16) for p in parts]
    sls = [slice(X_HEAD_DIM * h, X_HEAD_DIM * (h + 1)) for h in range(X_HEADS)]
    sc = [[lax.dot_general(qb[p][:, sl], k_ref[:, sl], _NT, preferred_element_type=F32)
           for sl in sls] for p in parts]
    es = [[None] * X_HEADS for _ in parts]
    rinv = [[None] * X_HEADS for _ in parts]
    for p in parts:
        for h in range(X_HEADS):
            s = sc[p][h] * (1.0 / math.sqrt(X_HEAD_DIM))
            e = jnp.exp(s - jnp.max(s, axis=-1, keepdims=True))
            es[p][h] = e.astype(BF16)
            rinv[p][h] = 1.0 / jnp.sum(e, axis=-1, keepdims=True)
    o = [jnp.concatenate(
        [jnp.dot(es[p][h], v_ref[:, sls[h]], preferred_element_type=F32) * rinv[p][h]
         for h in range(X_HEADS)], axis=1).astype(BF16) for p in parts]
    x2 = [x1[p] + jnp.dot(o[p], wco_ref[...], preferred_element_type=F32) for p in parts]

    logits_p = []
    for p in parts:
        x2_ref[rows[p], :] = x2[p]
        xn = _rms(x2[p], gf_ref[...])
        xh = xn.astype(BF16)
        _store_slabs_staged(xn_ref, xn, tm, stage_s, first=p * tm)
        xl = (xn - xh.astype(F32)).astype(BF16)
        lg2 = jnp.dot(xh, wr_ref[...], preferred_element_type=F32)
        logits_p.append(lg2[:, :LANES] + lg2[:, LANES:]
                        + jnp.dot(xl, wr_ref[:, :LANES], preferred_element_type=F32) + br_ref[...])

    for p in parts:
        _route_part(logits_p[p], rows[p], p * tm, route_ref, idx_ref, cnt_ref, carry_s)


def _route_part(logits, rows, first, route_ref, idx_ref, cnt_ref, carry_s):
    tm = logits.shape[0]
    lane = lax.broadcasted_iota(jnp.int32, (tm, LANES), 1)
    lanef = lane.astype(F32)
    big = float(4 * LANES)
    gl = jnp.where(lane < N_GROUPS, logits, NEG_INF)
    gmax = jnp.max(gl, axis=-1, keepdims=True)
    gsel = jnp.min(jnp.where(gl == gmax, lanef, big), axis=-1, keepdims=True)
    gw = 1.0 / jnp.sum(jnp.exp(gl - gmax), axis=-1, keepdims=True)
    lo_lane = _ROUTER_LANE0 + EXPERTS_PER_GROUP * gsel
    in_group = (lanef >= lo_lane) & (lanef < lo_lane + EXPERTS_PER_GROUP)
    el = jnp.where(in_group, logits, NEG_INF)
    v0 = jnp.max(el, axis=-1, keepdims=True)
    i0 = jnp.min(jnp.where(el == v0, lanef, big), axis=-1, keepdims=True)
    el2 = jnp.where(lanef == i0, NEG_INF, el)
    v1 = jnp.max(el2, axis=-1, keepdims=True)
    i1 = jnp.min(jnp.where(el2 == v1, lanef, big), axis=-1, keepdims=True)
    tt = jnp.exp(v1 - v0)
    p0 = gw / (1.0 + tt)
    p1 = gw * tt / (1.0 + tt)

    sel0 = lanef == i0
    sel1 = lanef == i1
    mb = jnp.where(sel0 | sel1, 1.0, 0.0).astype(BF16)
    ti = lax.broadcasted_iota(jnp.int32, (tm, tm), 0)
    si = lax.broadcasted_iota(jnp.int32, (tm, tm), 1)
    strict = jnp.where(si < ti, 1.0, 0.0).astype(BF16)
    carry = carry_s[...]
    pref = jnp.dot(strict, mb, preferred_element_type=F32) + carry
    r0 = jnp.sum(jnp.where(sel0, pref, 0.0), axis=-1, keepdims=True)
    r1 = jnp.sum(jnp.where(sel1, pref, 0.0), axis=-1, keepdims=True)
    carry = carry + jnp.sum(mb.astype(F32), axis=0, keepdims=True)
    carry_s[...] = carry
    cnt_ref[...] = carry

    route = jnp.zeros((tm, LANES), F32)
    for idx, col in ((_R_E0, i0 - _ROUTER_LANE0), (_R_E1, i1 - _ROUTER_LANE0), (_R_RANK0, r0),
                     (_R_RANK1, r1), (_R_P0, p0), (_R_P1, p1)):
        route = jnp.where(lane == idx, col, route)
    route_ref[rows, :] = route
    idx_ref[:, pl.ds(first, tm)] = jnp.transpose(route)[0:8, :].astype(jnp.int32)


def _mid(x2d, att, hm, wo, gc, wq, kmem, vmem_, wco, gf, wr, br, seq):
    t, d = x2d.shape
    tm = TM_MID
    per_b = seq // tm
    rowmap = lambda i: (i, 0)
    const = lambda i: (0, 0)
    bmap = lambda i: (i // per_b, 0)
    mem_len = kmem.shape[0] // (t // seq)
    return pl.pallas_call(
        _mid_kernel,
        grid=(t // tm,),
        in_specs=[pl.BlockSpec((tm, d), rowmap),
                  pl.BlockSpec((tm, ATT_Q), rowmap),
                  pl.BlockSpec((tm, ML_W), rowmap),
                  pl.BlockSpec(wo.shape, const, pipeline_mode=pl.Buffered(1)),
                  pl.BlockSpec((1, d), const),
                  pl.BlockSpec(wq.shape, const, pipeline_mode=pl.Buffered(1)),
                  pl.BlockSpec((mem_len, kmem.shape[1]), bmap),
                  pl.BlockSpec((mem_len, vmem_.shape[1]), bmap),
                  pl.BlockSpec(wco.shape, const, pipeline_mode=pl.Buffered(1)),
                  pl.BlockSpec((1, d), const),
                  pl.BlockSpec(wr.shape, const),
                  pl.BlockSpec((1, LANES), const)],
        out_specs=[pl.BlockSpec((tm, d), rowmap),
                   pl.BlockSpec((tm * _PACK_ROWS, LANES), rowmap),
                   pl.BlockSpec((tm, LANES), rowmap),
                   pl.BlockSpec((8, tm), lambda i: (0, i)),
                   pl.BlockSpec((1, LANES), const)],
        out_shape=[jax.ShapeDtypeStruct((t, d), F32),
                   jax.ShapeDtypeStruct((t * _PACK_ROWS, LANES), BF16),
                   jax.ShapeDtypeStruct((t, LANES), F32),
                   jax.ShapeDtypeStruct((8, t), jnp.int32),
                   jax.ShapeDtypeStruct((1, LANES), F32)],
        scratch_shapes=[pltpu.VMEM((1, LANES), F32),
                        pltpu.VMEM((tm * _PACK_ROWS, LANES), F32),
                        pltpu.VMEM(wo.shape, BF16),
                        pltpu.VMEM(wq.shape, BF16),
                        pltpu.VMEM(wco.shape, BF16)],
        compiler_params=_cparams(1),
        name="mid",
    )(x2d, att, hm, wo, gc, wq, kmem, vmem_, wco, gf, wr, br)


_M_BLOCK_E, _M_PAD_END, _M_NUSED = range(3)
_META_LANES = 2 * LANES


def _plan_kernel(cnt_ref, idx_ref, dest_ref, meta_ref):
    cnt = cnt_ref[...]
    lane = lax.broadcasted_iota(jnp.int32, (1, LANES), 1)
    is_expert = (lane >= _ROUTER_LANE0) & (lane < _ROUTER_LANE0 + N_EXPERTS)
    nblk = jnp.where(is_expert, jnp.floor((cnt + (BM - 1)) * (1.0 / BM)), 0.0)
    jj = lax.broadcasted_iota(jnp.int32, (LANES, LANES), 0)
    kk = lax.broadcasted_iota(jnp.int32, (LANES, LANES), 1)
    upper = jnp.where(jj <= kk, 1.0, 0.0).astype(BF16)
    pend_blk = jnp.dot(jnp.broadcast_to(nblk, (8, LANES)).astype(BF16), upper,
                       preferred_element_type=F32)[0:1, :]
    pstart_rows = (pend_blk - nblk) * BM
    pend_rows = pend_blk * BM

    idx = idx_ref[...]
    off = jnp.zeros(idx.shape, F32)
    for e in range(N_EXPERTS):
        lane_e = _ROUTER_LANE0 + e
        off = jnp.where(idx == e, pstart_rows[:, lane_e:lane_e + 1], off)
    ranks = pltpu.roll(idx, idx.shape[0] - 2, axis=0)
    dest_ref[...] = ranks + off.astype(jnp.int32)

    blk = lax.broadcasted_iota(jnp.int32, (1, _META_LANES), 1).astype(F32)
    block_e = jnp.zeros((1, _META_LANES), F32)
    for e in range(N_EXPERTS):
        lane_e = _ROUTER_LANE0 + e
        block_e = block_e + jnp.where(pend_blk[:, lane_e:lane_e + 1] <= blk, 1.0, 0.0)
    block_e = jnp.minimum(block_e, N_EXPERTS - 1.0)
    last = _ROUTER_LANE0 + N_EXPERTS - 1
    nused = pend_blk[:, last:last + 1]
    pend_wide = jnp.concatenate([pend_rows, jnp.zeros((1, _META_LANES - LANES), F32)], axis=1)
    sub = lax.broadcasted_iota(jnp.int32, (8, _META_LANES), 0)
    meta = jnp.where(sub == _M_BLOCK_E, block_e, jnp.where(sub == _M_PAD_END, pend_wide, nused))
    meta_ref[...] = meta.astype(jnp.int32)


def _plan(cnt, idx):
    t = idx.shape[1]
    return pl.pallas_call(
        _plan_kernel,
        grid=(1,),
        in_specs=[pl.BlockSpec(cnt.shape, lambda i: (0, 0)),
                  pl.BlockSpec(idx.shape, lambda i: (0, 0))],
        out_specs=[pl.BlockSpec((8, t), lambda i: (0, 0)),
                   pl.BlockSpec((8, _META_LANES), lambda i: (0, 0))],
        out_shape=[jax.ShapeDtypeStruct((8, t), jnp.int32),
                   jax.ShapeDtypeStruct((8, _META_LANES), jnp.int32)],
        compiler_params=_cparams(1),
        name="plan",
    )(cnt, idx)


_SLAB = BM * _PACK_ROWS


def _dispatch_kernel(meta_ref, d_ref, xn_ref, xpad_ref, zbuf, sem, zsem):
    i = pl.program_id(0)
    td = xn_ref.shape[0] // _PACK_ROWS

    def zero_copy(e):
        pend = meta_ref[_M_PAD_END, _ROUTER_LANE0 + e]
        pstart = meta_ref[_M_PAD_END, _ROUTER_LANE0 + e - 1]
        first = pl.multiple_of((pend - BM) * _PACK_ROWS, _SLAB)
        cp = pltpu.make_async_copy(zbuf, xpad_ref.at[pl.ds(first, _SLAB), :], zsem)
        return pend > pstart, cp

    @pl.when(i == 0)
    def _():
        zbuf[...] = jnp.zeros(zbuf.shape, zbuf.dtype)
        for e in range(N_EXPERTS):
            nonempty, cp = zero_copy(e)
            pl.when(nonempty)(cp.start)
        for e in range(N_EXPERTS):
            nonempty, cp = zero_copy(e)
            pl.when(nonempty)(cp.wait)

        def tail_copy(b):
            return pltpu.make_async_copy(
                zbuf, xpad_ref.at[pl.ds(pl.multiple_of(b * _SLAB, _SLAB), _SLAB), :], zsem)

        first_unused = meta_ref[_M_NUSED, 0]
        n_blocks = xpad_ref.shape[0] // _SLAB
        lax.fori_loop(first_unused, n_blocks, lambda b, c: (tail_copy(b).start(), c)[1], 0)
        lax.fori_loop(first_unused, n_blocks, lambda b, c: (tail_copy(b).wait(), c)[1], 0)

    def row_copy(t, dst):
        src = pl.ds(t * _PACK_ROWS, _PACK_ROWS)
        dsl = pl.ds(pl.multiple_of(dst * _PACK_ROWS, _PACK_ROWS), _PACK_ROWS)
        return pltpu.make_async_copy(xn_ref.at[src, :], xpad_ref.at[dsl, :], sem)

    for t in range(td):
        row_copy(t, d_ref[0, t]).start(priority=0)
        row_copy(t, d_ref[1, t]).start(priority=1)

    def wait(t, carry):
        row_copy(0, 0).wait()
        row_copy(0, 0).wait()
        return carry

    lax.fori_loop(0, td, wait, 0, unroll=8)


def _dispatch(meta, dest, xn_packed, rows):
    t = dest.shape[1]
    td = TD
    return pl.pallas_call(
        _dispatch_kernel,
        grid_spec=pltpu.PrefetchScalarGridSpec(
            num_scalar_prefetch=1,
            grid=(t // td,),
            in_specs=[pl.BlockSpec((8, td), lambda i, m: (0, i), memory_space=pltpu.SMEM),
                      pl.BlockSpec((td * _PACK_ROWS, LANES), lambda i, m: (i, 0))],
            out_specs=pl.BlockSpec(memory_space=pl.ANY),
            scratch_shapes=[pltpu.VMEM((_SLAB, LANES), BF16),
                            pltpu.SemaphoreType.DMA(()),
                            pltpu.SemaphoreType.DMA(())]),
        out_shape=jax.ShapeDtypeStruct((rows * _PACK_ROWS, LANES), BF16),
        compiler_params=_cparams(1),
        name="dispatch",
    )(meta, dest, xn_packed)


def _expert_kernel(meta_ref, xpad_ref, wg_ref, wu_ref, wd_ref, ypad_ref, wg_s, wu_s, wd_s,
                   xbuf, ybuf, xsem, ysem):
    e = pl.program_id(0)
    nused = meta_ref[_M_NUSED, 0]
    lane_e = _ROUTER_LANE0 + e
    first_blk = lax.div(meta_ref[_M_PAD_END, lane_e - 1], BM)
    n_blk = lax.div(meta_ref[_M_PAD_END, lane_e], BM) - first_blk

    def block_rows(g):
        return pl.ds(pl.multiple_of(g * _SLAB, _SLAB), _SLAB)

    def x_copy(g, slot):
        return pltpu.make_async_copy(xpad_ref.at[block_rows(g), :], xbuf.at[slot], xsem.at[slot])

    def y_copy(g, slot):
        return pltpu.make_async_copy(ybuf.at[slot], ypad_ref.at[block_rows(g), :], ysem.at[slot])

    nx = xbuf.shape[0]
    ahead = nx - 1

    @pl.when(e == 0)
    def _():
        for g0 in range(ahead):
            @pl.when(g0 < nused)
            def _():
                x_copy(g0, g0).start(priority=1)

    @pl.when(n_blk > 0)
    def _():
        wg_s[...] = wg_ref[...].astype(BF16)
        wu_s[...] = wu_ref[...].astype(BF16)
        wd_s[...] = wd_ref[...].astype(BF16)

    def body(j, carry):
        g = first_blk + j
        xslot = lax.rem(g, nx)
        yslot = lax.rem(g, 2)
        x_copy(g, xslot).wait()

        @pl.when(g + ahead < nused)
        def _():
            x_copy(g + ahead, lax.rem(g + ahead, nx)).start(priority=1)

        xb = _load_slabs(xbuf.at[xslot], BM).astype(BF16)
        gate = jnp.dot(xb, wg_s[...], preferred_element_type=F32)
        up = jnp.dot(xb, wu_s[...], preferred_element_type=F32)
        hb = (gate * jax.nn.sigmoid(gate) * up).astype(BF16)
        y = jnp.dot(hb, wd_s[...], preferred_element_type=F32)

        @pl.when(g >= 2)
        def _():
            y_copy(g - 2, yslot).wait()

        _store_slabs(ybuf.at[yslot], y, BM)
        y_copy(g, yslot).start()
        return carry

    lax.fori_loop(0, n_blk, body, 0)

    @pl.when(e == pl.num_programs(0) - 1)
    def _():
        for back in (2, 1):
            @pl.when(nused >= back)
            def _():
                y_copy(nused - back, lax.rem(nused - back, 2)).wait()

        ybuf[0] = jnp.zeros(ybuf.shape[1:], ybuf.dtype)
        n_blocks = ypad_ref.shape[0] // _SLAB
        lax.fori_loop(nused, n_blocks, lambda b, c: (y_copy(b, 0).start(), c)[1], 0)
        lax.fori_loop(nused, n_blocks, lambda b, c: (y_copy(b, 0).wait(), c)[1], 0)


def _experts(meta, xpad, wg, wu, wd):
    n_exp, d, de = wg.shape
    wmap = lambda e, m: (e, 0, 0)
    return pl.pallas_call(
        _expert_kernel,
        grid_spec=pltpu.PrefetchScalarGridSpec(
            num_scalar_prefetch=1,
            grid=(n_exp,),
            in_specs=[pl.BlockSpec(memory_space=pl.ANY),
                      pl.BlockSpec((None, d, de), wmap),
                      pl.BlockSpec((None, d, de), wmap),
                      pl.BlockSpec((None, de, d), wmap)],
            out_specs=pl.BlockSpec(memory_space=pl.ANY),
            scratch_shapes=[pltpu.VMEM((d, de), BF16),
                            pltpu.VMEM((d, de), BF16),
                            pltpu.VMEM((de, d), BF16),
                            pltpu.VMEM((EXPERT_AHEAD + 1, _SLAB, LANES), BF16),
                            pltpu.VMEM((2, _SLAB, LANES), BF16),
                            pltpu.SemaphoreType.DMA((EXPERT_AHEAD + 1,)),
                            pltpu.SemaphoreType.DMA((2,))]),
        out_shape=jax.ShapeDtypeStruct(xpad.shape, BF16),
        compiler_params=_cparams(1),
        name="experts",
    )(meta, xpad, wg, wu, wd)


def _combine_kernel(d0c_ref, d1c_ref, d0n_ref, d1n_ref, x2_ref, route_ref, g_ref, ypad_ref, out_ref,
                    ybuf, sem,
                    stage0_s, stage1_s):
    i = pl.program_id(0)
    n = pl.num_programs(0)
    tc = x2_ref.shape[0]

    def row_copy(src, slot, which, t):
        ssl = pl.ds(pl.multiple_of(src * _PACK_ROWS, _PACK_ROWS), _PACK_ROWS)
        dsl = pl.ds(t * _PACK_ROWS, _PACK_ROWS)
        return pltpu.make_async_copy(ypad_ref.at[ssl, :], ybuf.at[slot, which, dsl, :],
                                     sem.at[slot])

    def issue(d0_ref, d1_ref, slot):
        for t in range(tc):
            row_copy(d0_ref[0, 0, t], slot, 0, t).start(priority=0)
            row_copy(d1_ref[0, 0, t], slot, 1, t).start(priority=1)

    slot = i % 2

    @pl.when(i == 0)
    def _():
        issue(d0c_ref, d1c_ref, 0)

    @pl.when(i + 1 < n)
    def _():
        issue(d0n_ref, d1n_ref, 1 - slot)

    def wait(t, carry):
        row_copy(0, slot, 0, 0).wait()
        row_copy(0, slot, 1, 0).wait()
        return carry
    lax.fori_loop(0, tc, wait, 0, unroll=8)

    route = route_ref[...]
    p0 = route[:, _R_P0:_R_P0 + 1]
    p1 = route[:, _R_P1:_R_P1 + 1]
    y0 = _load_slabs_staged(ybuf.at[slot, 0], tc, stage0_s)
    y1 = _load_slabs_staged(ybuf.at[slot, 1], tc, stage1_s)
    x3 = x2_ref[...] + p0 * y0 + p1 * y1
    out_ref[...] = _rms(x3, g_ref[...])


def _combine(dest, x2, route, g, ypad):
    t, d = x2.shape
    tc = TC
    nt = t // tc
    d0 = dest[0].reshape(nt, 1, tc)
    d1 = dest[1].reshape(nt, 1, tc)
    cur = pl.BlockSpec((1, 1, tc), lambda i: (i, 0, 0), memory_space=pltpu.SMEM)
    nxt = pl.BlockSpec((1, 1, tc), lambda i: (jnp.minimum(i + 1, nt - 1), 0, 0),
                       memory_space=pltpu.SMEM)
    return pl.pallas_call(
        _combine_kernel,
        grid=(nt,),
        in_specs=[cur, cur, nxt, nxt,
                  pl.BlockSpec((tc, d), lambda i: (i, 0)),
                  pl.BlockSpec((tc, LANES), lambda i: (i, 0)),
                  pl.BlockSpec((1, d), lambda i: (0, 0)),
                  pl.BlockSpec(memory_space=pl.ANY)],
        out_specs=pl.BlockSpec((tc, d), lambda i: (i, 0)),
        out_shape=jax.ShapeDtypeStruct((t, d), F32),
        scratch_shapes=[pltpu.VMEM((2, 2, tc * _PACK_ROWS, LANES), BF16),
                        pltpu.SemaphoreType.DMA((2,)),
                        pltpu.VMEM((tc * _PACK_ROWS, LANES), F32),
                        pltpu.VMEM((tc * _PACK_ROWS, LANES), F32)],
        compiler_params=_cparams(1),
        name="combine",
    )(d0, d1, d0, d1, x2, route, g, ypad)


def _pad_cols(a, width):
    return jnp.pad(a, ((0, 0), (0, width - a.shape[1])))


def kernel(x, mem, norm_mix, w_in, b_gates, conv_w, conv_b, att_sinks, norm_att_out, norm_ml_out,
           w_out, norm_cross, norm_mem, w_cq, w_ckv, w_co, norm_ffn, w_router_group,
           b_router_group, w_router_expert, b_router_expert, w_e_gate, w_e_up, w_e_down,
           norm_final):
    bsz, seq, d = x.shape
    t = bsz * seq
    depth = w_in.shape[0]
    assert depth == 1, "the final RMSNorm is fused into the last layer's combine kernel"
    xs = x.reshape(t, d)
    mem2d = mem.reshape(-1, d)

    for l in range(depth):
        bi = _pad_cols(b_gates[l][None, :ML_HEADS], LANES)
        bf = _pad_cols(b_gates[l][None, ML_HEADS:], LANES)
        w_r = _pad_cols(jnp.concatenate([w_router_group[l], w_router_expert[l]], axis=1), LANES)
        w_r_hi = w_r.astype(BF16)
        w_r_lo = (w_r - w_r_hi.astype(F32)).astype(BF16)
        w_r_packed = jnp.concatenate([w_r_hi, w_r_lo], axis=1)
        b_r = _pad_cols(jnp.concatenate([b_router_group[l], b_router_expert[l]])[None, :], LANES)

        qa, ka, va, qk, vm, og, gi, gf = _in_proj(xs, norm_mix[l][None, :],
                                                  jnp.swapaxes(w_in, 1, 2), l)
        att = _swa(qa, ka, va, att_sinks[l], norm_att_out[l][None, :], bsz, seq)
        hm = _mlstm(qk, vm, og, gi, gf, conv_w[l], conv_b[l][None, :], bi, bf,
                    norm_ml_out[l][None, :], bsz, seq)

        kmem, vmem_ = _kv_prep(mem2d, norm_mem[l][None, :], w_ckv[l])
        x2, xn, route, idx, cnt = _mid(xs, att, hm, w_out[l], norm_cross[l][None, :],
                                       w_cq[l], kmem, vmem_, w_co[l],
                                       norm_ffn[l][None, :], w_r_packed, b_r, seq)

        nbk = (t * TOP_K) // BM + N_EXPERTS
        assert nbk <= _META_LANES
        dest, meta = _plan(cnt, idx)
        xpad = _dispatch(meta, dest, xn, nbk * BM)
        ypad = _experts(meta, xpad, w_e_gate[l], w_e_up[l], w_e_down[l])
        xs = _combine(dest, x2, route, norm_final[None, :], ypad)
    return xs.reshape(bsz, seq, d)
```
